```python
import jax, jax.numpy as jnp
from jax import lax
import numpy as np

D_MODEL = 1024
BATCH = 4
SEQ = 4096
DEPTH = 2
DEC_BATCH = 32
DEC_SEQ = 1
PAST_LEN = 8192
PAGE_SIZE = 128

N_EVEN = (DEPTH + 1) // 2
N_ODD = DEPTH // 2
MIX_WIDTH = D_MODEL
A_WIDTH = MIX_WIDTH // 2
B_WIDTH = MIX_WIDTH - A_WIDTH
POOL_WINDOWS = (2, 4, 8, 16)
POOL_GROUP = A_WIDTH // len(POOL_WINDOWS)
POOL_HIST = max(POOL_WINDOWS) - 1
CONV_B_WIDTH = 31
N_HEADS = 8
HEAD_DIM = 64
C_WIDTH = N_HEADS * HEAD_DIM
D_WIDTH = MIX_WIDTH - C_WIDTH
CONV_D_WIDTH = 3
Q_BLOCK = 128
D_FF = 2816
N_EXPERTS = 8
TOP_K = 2
D_FF_EXPERT = 3584
ALPHA = (2 * DEPTH) ** 0.25
BETA = (8 * DEPTH) ** -0.25
FORGET_BIAS = 3.0
LN_EPS = 1e-5
EVEN_IN = A_WIDTH + 2 * B_WIDTH
ODD_IN = 3 * C_WIDTH + N_HEADS + 3 * D_WIDTH

kernel_name = 'hybrid_pool_conformer_fox_shortconv_decode_step'


def layer_norm(x, g, b):
    xf = x.astype(jnp.float32)
    mu = jnp.mean(xf, axis=-1, keepdims=True)
    var = jnp.mean(jnp.square(xf - mu), axis=-1, keepdims=True)
    y = (xf - mu) * lax.rsqrt(var + LN_EPS) * g.astype(jnp.float32) + b.astype(jnp.float32)
    return y.astype(x.dtype)


def causal_dwconv(u_ext, w):
    return lax.conv_general_dilated(u_ext, w[:, None, :], window_strides=(1,), padding='VALID',
                                    dimension_numbers=('NWC', 'WIO', 'NWC'),
                                    feature_group_count=u_ext.shape[-1])


def multiscale_pool(a_ext, n_hist, first_pos, pool_w, pool_scale):
    T = a_ext.shape[1] - n_hist
    cs = jnp.cumsum(jnp.pad(a_ext.astype(jnp.float32), ((0, 0), (1, 0), (0, 0))), axis=1)
    hi = n_hist + jnp.arange(T) + 1
    pos = first_pos + jnp.arange(T)
    cur = a_ext[:, n_hist:].astype(jnp.float32)
    outs = []
    for g, w in enumerate(POOL_WINDOWS):
        sl = slice(g * POOL_GROUP, (g + 1) * POOL_GROUP)
        csg = cs[..., sl]
        lo = jnp.maximum(hi - w, 0)
        win = jnp.take(csg, hi, axis=1) - jnp.take(csg, lo, axis=1)
        cnt = jnp.minimum(pos + 1, w).astype(jnp.float32)
        pooled = (win / cnt[None, :, None] - cur[..., sl]).astype(a_ext.dtype)
        outs.append(pooled @ pool_w[g])
    return jnp.concatenate(outs, axis=-1) * pool_scale


def fox_attend(q, k, v, Fq, Fk, q_pos, k_pos):
    s = jnp.einsum('bqhd,bkhd->bhqk', q, k).astype(jnp.float32) * (HEAD_DIM ** -0.5)
    s = s + jnp.transpose(Fq, (0, 2, 1))[..., :, None] - jnp.transpose(Fk, (0, 2, 1))[..., None, :]
    s = jnp.where(k_pos[None, :] <= q_pos[:, None], s, -jnp.inf)
    p = jax.nn.softmax(s, axis=-1).astype(v.dtype)
    return jnp.einsum('bhqk,bkhd->bqhd', p, v)


def fox_prompt(q, k, v, logf):
    B, T, H, Dh = q.shape
    F = jnp.cumsum(logf, axis=1)
    nb = T // Q_BLOCK
    qb = q.reshape(B, nb, Q_BLOCK, H, Dh).transpose(1, 0, 2, 3, 4)
    Fb = F.reshape(B, nb, Q_BLOCK, H).transpose(1, 0, 2, 3)
    starts = jnp.arange(nb) * Q_BLOCK
    k_pos = jnp.arange(T)

    def block(args):
        qi, Fi, st = args
        return fox_attend(qi, k, v, Fi, F, st + jnp.arange(Q_BLOCK), k_pos)

    o = lax.map(block, (qb, Fb, starts))
    return o.transpose(1, 0, 2, 3, 4).reshape(B, T, H, Dh)


def fox_sample(q, k_new, v_new, logf_new, k_past, v_past, logf_past):
    P = k_past.shape[1]
    T = q.shape[1]
    F = jnp.cumsum(jnp.concatenate([logf_past.astype(jnp.float32), logf_new], axis=1), axis=1)
    s = jnp.concatenate([jnp.einsum('bqhd,bkhd->bhqk', q, k_past),
                         jnp.einsum('bqhd,bkhd->bhqk', q, k_new)], axis=-1).astype(jnp.float32)
    s = s * (HEAD_DIM ** -0.5)
    s = s + jnp.transpose(F[:, P:], (0, 2, 1))[..., :, None] - jnp.transpose(F, (0, 2, 1))[..., None, :]
    q_pos = P + jnp.arange(T)
    k_pos = jnp.arange(P + T)
    s = jnp.where(k_pos[None, :] <= q_pos[:, None], s, -jnp.inf)
    p = jax.nn.softmax(s, axis=-1).astype(v_new.dtype)
    return (jnp.einsum('bhqk,bkhd->bqhd', p[..., :P], v_past)
            + jnp.einsum('bhqk,bkhd->bqhd', p[..., P:], v_new))


def even_mixer(x, pool_hist, conv_hist, first_pos, w_in, pool_w, pool_scale, conv_w, conv_bias, ln_g, ln_b, w_out):
    proj = x @ w_in
    a = proj[..., :A_WIDTH]
    val = proj[..., A_WIDTH:A_WIDTH + B_WIDTH]
    gate = proj[..., A_WIDTH + B_WIDTH:]
    u = val * jax.nn.sigmoid(gate)
    if pool_hist is None:
        a_ext, n_hist = a, 0
    else:
        a_ext, n_hist = jnp.concatenate([pool_hist, a], axis=1), pool_hist.shape[1]
    y_a = multiscale_pool(a_ext, n_hist, first_pos, pool_w, pool_scale)
    if conv_hist is None:
        conv_hist = jnp.zeros((x.shape[0], CONV_B_WIDTH - 1, B_WIDTH), u.dtype)
    u_ext = jnp.concatenate([conv_hist, u], axis=1)
    y_b = jax.nn.silu(layer_norm(causal_dwconv(u_ext, conv_w) + conv_bias, ln_g, ln_b))
    out = jnp.concatenate([y_a, y_b], axis=-1) @ w_out
    return out, a_ext[:, -POOL_HIST:], u_ext[:, -(CONV_B_WIDTH - 1):]


def odd_mixer(x, past, conv_hist, w_in, forget_bias, conv_w, w_out):
    Bsz, T, _ = x.shape
    proj = x @ w_in
    cuts = [C_WIDTH, 2 * C_WIDTH, 3 * C_WIDTH, 3 * C_WIDTH + N_HEADS,
            3 * C_WIDTH + N_HEADS + D_WIDTH, 3 * C_WIDTH + N_HEADS + 2 * D_WIDTH]
    q, k, v, fl, h, bg, cg = jnp.split(proj, cuts, axis=-1)
    q = q.reshape(Bsz, T, N_HEADS, HEAD_DIM)
    k = k.reshape(Bsz, T, N_HEADS, HEAD_DIM)
    v = v.reshape(Bsz, T, N_HEADS, HEAD_DIM)
    logf = jax.nn.log_sigmoid(fl.astype(jnp.float32) + forget_bias.astype(jnp.float32))
    if past is None:
        o = fox_prompt(q, k, v, logf)
        conv_hist = jnp.zeros((Bsz, CONV_D_WIDTH - 1, D_WIDTH), h.dtype)
    else:
        o = fox_sample(q, k, v, logf, *past)
    u_ext = jnp.concatenate([conv_hist, cg * h], axis=1)
    y_d = bg * causal_dwconv(u_ext, conv_w)
    out = jnp.concatenate([o.reshape(Bsz, T, C_WIDTH), y_d], axis=-1) @ w_out
    return out, k, v, logf.astype(x.dtype), u_ext[:, -(CONV_D_WIDTH - 1):]


def swiglu(x, w1, w3, w2):
    return (jax.nn.silu(x @ w1) * (x @ w3)) @ w2


def moe_swiglu(x, router_w, w1, w3, w2):
    logits = (x @ router_w).astype(jnp.float32)
    top_v, top_i = lax.top_k(logits, TOP_K)
    gates = jax.nn.softmax(top_v, axis=-1)
    dense_gate = jnp.sum(jax.nn.one_hot(top_i, N_EXPERTS, dtype=jnp.float32) * gates[..., None], axis=-2)
    dense_gate = dense_gate.astype(x.dtype)
    out = jnp.zeros_like(x)
    for e in range(N_EXPERTS):
        out = out + dense_gate[..., e:e + 1] * swiglu(x, w1[e], w3[e], w2[e])
    return out


def setup_inputs(seed: int = 0) -> dict:
    key = jax.random.key(seed)
    ks = iter(jax.random.split(key, 64))

    def nrm(shape, scale):
        return jax.random.normal(next(ks), shape, jnp.float32) * scale

    n_pages = PAST_LEN // PAGE_SIZE
    n_pool = (DEC_BATCH * n_pages * 5) // 4
    page_table = jax.random.permutation(next(ks), n_pool)[:DEC_BATCH * n_pages]
    page_table = page_table.reshape(DEC_BATCH, n_pages).astype(jnp.int32)
    return {
        'x_prompt': nrm((BATCH, SEQ, D_MODEL), 1.0),
        'x_sample': nrm((DEC_BATCH, DEC_SEQ, D_MODEL), 1.0),
        'state_pool': nrm((N_EVEN, DEC_BATCH, POOL_HIST, A_WIDTH), 1.0),
        'state_conv_b': nrm((N_EVEN, DEC_BATCH, CONV_B_WIDTH - 1, B_WIDTH), 0.5),
        'cache_k': nrm((N_ODD, n_pool, PAGE_SIZE, N_HEADS, HEAD_DIM), 1.0),
        'cache_v': nrm((N_ODD, n_pool, PAGE_SIZE, N_HEADS, HEAD_DIM), 1.0),
        'cache_logf': jax.nn.log_sigmoid(FORGET_BIAS + nrm((N_ODD, n_pool, PAGE_SIZE, N_HEADS), 1.0)),
        'state_conv_d': nrm((N_ODD, DEC_BATCH, CONV_D_WIDTH - 1, D_WIDTH), 0.5),
        'page_table': page_table,
        'w_in_even': nrm((N_EVEN, D_MODEL, EVEN_IN), D_MODEL ** -0.5),
        'pool_w': nrm((N_EVEN, len(POOL_WINDOWS), POOL_GROUP, POOL_GROUP), POOL_GROUP ** -0.5),
        'pool_scale': 1.0 + nrm((N_EVEN, A_WIDTH), 0.1),
        'conv_b_w': nrm((N_EVEN, CONV_B_WIDTH, B_WIDTH), CONV_B_WIDTH ** -0.5),
        'conv_b_bias': nrm((N_EVEN, B_WIDTH), 0.01),
        'conv_ln_g': 1.0 + nrm((N_EVEN, B_WIDTH), 0.05),
        'conv_ln_b': nrm((N_EVEN, B_WIDTH), 0.01),
        'w_out_even': nrm((N_EVEN, MIX_WIDTH, D_MODEL), BETA * MIX_WIDTH ** -0.5),
        'ln_mix_even_g': 1.0 + nrm((N_EVEN, D_MODEL), 0.05),
        'ln_mix_even_b': nrm((N_EVEN, D_MODEL), 0.01),
        'ffn_w1': nrm((N_EVEN, D_MODEL, D_FF), D_MODEL ** -0.5),
        'ffn_w3': nrm((N_EVEN, D_MODEL, D_FF), D_MODEL ** -0.5),
        'ffn_w2': nrm((N_EVEN, D_FF, D_MODEL), BETA * D_FF ** -0.5),
        'ln_ffn_even_g': 1.0 + nrm((N_EVEN, D_MODEL), 0.05),
        'ln_ffn_even_b': nrm((N_EVEN, D_MODEL), 0.01),
        'w_in_odd': nrm((N_ODD, D_MODEL, ODD_IN), D_MODEL ** -0.5),
        'forget_bias': FORGET_BIAS + nrm((N_ODD, N_HEADS), 0.5),
        'conv_d_w': nrm((N_ODD, CONV_D_WIDTH, D_WIDTH), CONV_D_WIDTH ** -0.5),
        'w_out_odd': nrm((N_ODD, MIX_WIDTH, D_MODEL), BETA * MIX_WIDTH ** -0.5),
        'ln_mix_odd_g': 1.0 + nrm((N_ODD, D_MODEL), 0.05),
        'ln_mix_odd_b': nrm((N_ODD, D_MODEL), 0.01),
        'router_w': nrm((N_ODD, D_MODEL, N_EXPERTS), D_MODEL ** -0.5),
        'moe_w1': nrm((N_ODD, N_EXPERTS, D_MODEL, D_FF_EXPERT), D_MODEL ** -0.5),
        'moe_w3': nrm((N_ODD, N_EXPERTS, D_MODEL, D_FF_EXPERT), D_MODEL ** -0.5),
        'moe_w2': nrm((N_ODD, N_EXPERTS, D_FF_EXPERT, D_MODEL), BETA * D_FF_EXPERT ** -0.5),
        'ln_ffn_odd_g': 1.0 + nrm((N_ODD, D_MODEL), 0.05),
        'ln_ffn_odd_b': nrm((N_ODD, D_MODEL), 0.01),
    }


def reference(x_prompt, x_sample, state_pool, state_conv_b, cache_k, cache_v, cache_logf, state_conv_d,
              page_table, w_in_even, pool_w, pool_scale, conv_b_w, conv_b_bias, conv_ln_g, conv_ln_b,
              w_out_even, ln_mix_even_g, ln_mix_even_b, ffn_w1, ffn_w3, ffn_w2, ln_ffn_even_g, ln_ffn_even_b,
              w_in_odd, forget_bias, conv_d_w, w_out_odd, ln_mix_odd_g, ln_mix_odd_b, router_w,
              moe_w1, moe_w3, moe_w2, ln_ffn_odd_g, ln_ffn_odd_b):
    dec_b = page_table.shape[0]
    past_len = page_table.shape[1] * cache_k.shape[2]
    xp, xs = x_prompt, x_sample
    pool_p, pool_s, convb_p, convb_s = [], [], [], []
    k_p, k_s, v_p, v_s, lf_p, lf_s, convd_p, convd_s = [], [], [], [], [], [], [], []
    for layer in range(DEPTH):
        j = layer // 2
        if layer % 2 == 0:
            ew = (w_in_even[j], pool_w[j], pool_scale[j], conv_b_w[j], conv_b_bias[j],
                  conv_ln_g[j], conv_ln_b[j], w_out_even[j])
            mp, sp_pool, sp_conv = even_mixer(xp, None, None, 0, *ew)
            ms, ss_pool, ss_conv = even_mixer(xs, state_pool[j], state_conv_b[j], past_len, *ew)
            pool_p.append(sp_pool); pool_s.append(ss_pool)
            convb_p.append(sp_conv); convb_s.append(ss_conv)
            xp = layer_norm(ALPHA * xp + mp, ln_mix_even_g[j], ln_mix_even_b[j])
            xs = layer_norm(ALPHA * xs + ms, ln_mix_even_g[j], ln_mix_even_b[j])
            xp = layer_norm(ALPHA * xp + swiglu(xp, ffn_w1[j], ffn_w3[j], ffn_w2[j]), ln_ffn_even_g[j], ln_ffn_even_b[j])
            xs = layer_norm(ALPHA * xs + swiglu(xs, ffn_w1[j], ffn_w3[j], ffn_w2[j]), ln_ffn_even_g[j], ln_ffn_even_b[j])
        else:
            k_past = cache_k[j][page_table].reshape(dec_b, past_len, N_HEADS, HEAD_DIM)
            v_past = cache_v[j][page_table].reshape(dec_b, past_len, N_HEADS, HEAD_DIM)
            lf_past = cache_logf[j][page_table].reshape(dec_b, past_len, N_HEADS)
            ow = (w_in_odd[j], forget_bias[j], conv_d_w[j], w_out_odd[j])
            mp, kp, vp, lp, cp = odd_mixer(xp, None, None, *ow)
            ms, ks_, vs_, ls_, cs_ = odd_mixer(xs, (k_past, v_past, lf_past), state_conv_d[j], *ow)
            k_p.append(kp); k_s.append(ks_); v_p.append(vp); v_s.append(vs_)
            lf_p.append(lp); lf_s.append(ls_); convd_p.append(cp); convd_s.append(cs_)
            xp = layer_norm(ALPHA * xp + mp, ln_mix_odd_g[j], ln_mix_odd_b[j])
            xs = layer_norm(ALPHA * xs + ms, ln_mix_odd_g[j], ln_mix_odd_b[j])
            xp = layer_norm(ALPHA * xp + moe_swiglu(xp, router_w[j], moe_w1[j], moe_w3[j], moe_w2[j]), ln_ffn_odd_g[j], ln_ffn_odd_b[j])
            xs = layer_norm(ALPHA * xs + moe_swiglu(xs, router_w[j], moe_w1[j], moe_w3[j], moe_w2[j]), ln_ffn_odd_g[j], ln_ffn_odd_b[j])
    return (xp, xs,
            jnp.stack(pool_p), jnp.stack(pool_s), jnp.stack(convb_p), jnp.stack(convb_s),
            jnp.stack(k_p), jnp.stack(k_s), jnp.stack(v_p), jnp.stack(v_s),
            jnp.stack(lf_p), jnp.stack(lf_s), jnp.stack(convd_p), jnp.stack(convd_s))
```

```python
import functools

import numpy as np
import jax
import jax.numpy as jnp
from jax import lax
from jax.experimental import pallas as pl
from jax.experimental.pallas import tpu as pltpu

F32 = jnp.float32
BF16 = jnp.bfloat16
I32 = jnp.int32

LN_EPS = 1e-5
POOL_WINDOWS = (2, 4, 8, 16)
TOP_K = 2
NEG = -1e30

V7X_VMEM_BYTES = 64 * 1024 * 1024
V7X_LANES = 128
V7X_SUBLANES = 8
BF16_ROWS_PER_TILE = 2 * V7X_SUBLANES

VMEM_LIMIT = V7X_VMEM_BYTES - 8 * 1024 * 1024

TOKEN_TILE = 512
EXPERT_TILE = 512
EXPERT_FF_CHUNK = 512
FFN_CHUNK = 256
CONV_ROWS = 64
POOL_HIST_ROWS = 16
CONV_B_HIST_ROWS = 32
CONV_D_HIST_ROWS = 8
PAGES_PER_STEP = 16
SEG_PAD = BF16_ROWS_PER_TILE
SEG_BIG = 128
LOCAL_ROWS = TOP_K * TOKEN_TILE + 8 * SEG_PAD


def _cparams(sem):
    return pltpu.CompilerParams(dimension_semantics=sem, vmem_limit_bytes=VMEM_LIMIT)


def _dot(a, b):
    return jnp.dot(a, b, preferred_element_type=F32)


def _dot_nt(a, b):
    return lax.dot_general(a, b, (((1,), (1,)), ((), ())), preferred_element_type=F32)


def _dot_tn(a, b):
    return lax.dot_general(a, b, (((0,), (0,)), ((), ())), preferred_element_type=F32)


def _ln(z, g, b):
    mu = jnp.mean(z, axis=-1, keepdims=True)
    d = z - mu
    var = jnp.mean(d * d, axis=-1, keepdims=True)
    return d * lax.rsqrt(var + LN_EPS) * g + b


def _silu(x):
    return x * jax.nn.sigmoid(x)


def _log_sigmoid(z):
    return jnp.minimum(z, 0.0) - jnp.log1p(jnp.exp(-jnp.abs(z)))


def _split3(x):
    hi = x.astype(BF16)
    r = x - hi.astype(F32)
    mid = r.astype(BF16)
    lo = (r - mid.astype(F32)).astype(BF16)
    return hi, mid, lo


def _const_spec(shape):
    nd = len(shape)
    return pl.BlockSpec(shape, lambda *_: (0,) * nd)


def _resident_spec(shape):
    nd = len(shape)
    return pl.BlockSpec(shape, lambda *_: (0,) * nd, pipeline_mode=pl.Buffered(1))


def _pool_groups(a, hist_fn, pos, pw_ref):
    outs = []
    gw = a.shape[1] // len(POOL_WINDOWS)
    for g, w in enumerate(POOL_WINDOWS):
        c0 = g * gw
        cur = a[:, c0:c0 + gw]
        win = cur
        for k in range(1, w):
            win = win + hist_fn(k, c0, gw)
        cnt = jnp.minimum(pos + 1, w).astype(F32)
        pooled = win / cnt - cur
        outs.append(_dot(pooled.astype(BF16), pw_ref[g]))
    return jnp.concatenate(outs, axis=-1)


def _even_prompt_kernel(alpha, x_ref, win_ref, pw_ref, ps_ref, cw_ref, cb_ref, cg_ref, cbeta_ref,
                        wout_ref, g_ref, b_ref, y_ref, ah_ref, uh_ref, aext, uext, ybuf):
    t = pl.program_id(1)
    tT = x_ref.shape[0]
    aw = ps_ref.shape[-1]
    bw = cw_ref.shape[-1]
    taps = cw_ref.shape[0]
    AH, UH = POOL_HIST_ROWS, CONV_B_HIST_ROWS

    @pl.when(t == 0)
    def _():
        aext[0:AH, :] = jnp.zeros((AH, aw), F32)
        uext[0:UH, :] = jnp.zeros((UH, bw), F32)

    @pl.when(t > 0)
    def _():
        aext[0:AH, :] = aext[tT:tT + AH, :]
        uext[0:UH, :] = uext[tT:tT + UH, :]

    x = x_ref[...]
    proj = _dot(x.astype(BF16), win_ref[...])
    a = proj[:, :aw]
    u = proj[:, aw:aw + bw] * jax.nn.sigmoid(proj[:, aw + bw:])
    aext[AH:, :] = a
    uext[UH:, :] = u

    pos = t * tT + lax.broadcasted_iota(I32, (tT, 1), 0)
    ya = _pool_groups(a, lambda k, c0, gw: aext[AH - k:AH - k + tT, c0:c0 + gw], pos, pw_ref)
    ya = ya * ps_ref[...]

    base = UH - (taps - 1)
    for r0 in range(0, tT, CONV_ROWS):
        acc = uext[base + r0:base + r0 + CONV_ROWS, :] * cw_ref[0:1, :]
        for k in range(1, taps):
            acc = acc + uext[base + k + r0:base + k + r0 + CONV_ROWS, :] * cw_ref[k:k + 1, :]
        yb = _silu(_ln(acc + cb_ref[...], cg_ref[...], cbeta_ref[...]))
        ybuf[r0:r0 + CONV_ROWS, :] = yb.astype(BF16)

    mix = _dot(ya.astype(BF16), wout_ref[0:aw, :]) + _dot(ybuf[...], wout_ref[aw:, :])
    y_ref[...] = _ln(alpha * x + mix, g_ref[...], b_ref[...])
    ah_ref[...] = aext[tT:tT + AH, :]
    uh_ref[...] = uext[tT:tT + UH, :]


def _even_prompt(alpha, x, win, pw, ps, cw, cb, cg, cbeta, wout, g, b):
    B, T, D = x.shape
    tT = min(TOKEN_TILE, T)
    aw, bw = ps.shape[-1], cw.shape[-1]
    AH, UH = POOL_HIST_ROWS, CONV_B_HIST_ROWS
    consts = (win, pw, ps, cw, cb, cg, cbeta, wout, g, b)
    return pl.pallas_call(
        functools.partial(_even_prompt_kernel, alpha),
        grid=(B, T // tT),
        in_specs=[pl.BlockSpec((None, tT, D), lambda bi, ti: (bi, ti, 0))]
        + [_const_spec(c.shape) for c in consts],
        out_specs=[pl.BlockSpec((None, tT, D), lambda bi, ti: (bi, ti, 0)),
                   pl.BlockSpec((None, AH, aw), lambda bi, ti: (bi, 0, 0)),
                   pl.BlockSpec((None, UH, bw), lambda bi, ti: (bi, 0, 0))],
        out_shape=[jax.ShapeDtypeStruct((B, T, D), F32),
                   jax.ShapeDtypeStruct((B, AH, aw), F32),
                   jax.ShapeDtypeStruct((B, UH, bw), F32)],
        scratch_shapes=[pltpu.VMEM((AH + tT, aw), F32), pltpu.VMEM((UH + tT, bw), F32),
                        pltpu.VMEM((tT, bw), BF16)],
        compiler_params=_cparams(("arbitrary", "arbitrary")),
        name="even_mixer_prompt",
    )(x, *consts)


def _even_sample_kernel(alpha, first_pos, x_ref, sp_ref, sc_ref, win_ref, pw_ref, ps_ref, cw_ref, cb_ref,
                        cg_ref, cbeta_ref, wout_ref, g_ref, b_ref, y_ref, spo_ref, sco_ref):
    aw = ps_ref.shape[-1]
    bw = cw_ref.shape[-1]
    taps = cw_ref.shape[0]
    nph = sp_ref.shape[0]
    nch = sc_ref.shape[0]
    x = x_ref[...]
    proj = _dot(x.astype(BF16), win_ref[...])
    a = proj[:, :aw]
    u = proj[:, aw:aw + bw] * jax.nn.sigmoid(proj[:, aw + bw:])

    pos = jnp.full((x.shape[0], 1), first_pos, I32)
    ya = _pool_groups(a, lambda k, c0, gw: sp_ref[nph - k, :, c0:c0 + gw], pos, pw_ref)
    ya = ya * ps_ref[...]

    acc = u * cw_ref[taps - 1:taps, :]
    for k in range(taps - 1):
        acc = acc + sc_ref[k + nch - (taps - 1)] * cw_ref[k:k + 1, :]
    yb = _silu(_ln(acc + cb_ref[...], cg_ref[...], cbeta_ref[...]))

    mix = _dot(ya.astype(BF16), wout_ref[0:aw, :]) + _dot(yb.astype(BF16), wout_ref[aw:, :])
    y_ref[...] = _ln(alpha * x + mix, g_ref[...], b_ref[...])
    for j in range(nph - 1):
        spo_ref[j] = sp_ref[j + 1]
    spo_ref[nph - 1] = a
    for j in range(nch - 1):
        sco_ref[j] = sc_ref[j + 1]
    sco_ref[nch - 1] = u


def _even_sample(alpha, first_pos, x, sp_t, sc_t, win, pw, ps, cw, cb, cg, cbeta, wout, g, b):
    ins = (x, sp_t, sc_t, win, pw, ps, cw, cb, cg, cbeta, wout, g, b)
    return pl.pallas_call(
        functools.partial(_even_sample_kernel, alpha, first_pos),
        grid=(1,),
        in_specs=[_const_spec(c.shape) for c in ins],
        out_specs=[_const_spec(x.shape), _const_spec(sp_t.shape), _const_spec(sc_t.shape)],
        out_shape=[jax.ShapeDtypeStruct(x.shape, F32), jax.ShapeDtypeStruct(sp_t.shape, F32),
                   jax.ShapeDtypeStruct(sc_t.shape, F32)],
        compiler_params=_cparams(("arbitrary",)),
        name="even_mixer_sample",
    )(*ins)


def _ffn_kernel(alpha, x_ref, w1_ref, w3_ref, w2_ref, g_ref, b_ref, y_ref, h_ref):
    x = x_ref[...]
    xb = x.astype(BF16)
    ff = w1_ref.shape[1]
    for c in range(0, ff, FFN_CHUNK):
        h1 = _dot(xb, w1_ref[:, c:c + FFN_CHUNK])
        h3 = _dot(xb, w3_ref[:, c:c + FFN_CHUNK])
        h_ref[:, c:c + FFN_CHUNK] = (_silu(h1) * h3).astype(BF16)
    y = _dot(h_ref[...], w2_ref[...])
    y_ref[...] = _ln(alpha * x + y, g_ref[...], b_ref[...])


def _ffn(alpha, x, w1, w3, w2, g, b):
    n, d = x.shape
    tm = min(TOKEN_TILE, n)
    ff = w1.shape[1]
    assert ff % FFN_CHUNK == 0 and n % tm == 0
    return pl.pallas_call(
        functools.partial(_ffn_kernel, alpha),
        grid=(n // tm,),
        in_specs=[pl.BlockSpec((tm, d), lambda i: (i, 0)),
                  _resident_spec(w1.shape), _resident_spec(w3.shape), _resident_spec(w2.shape),
                  _const_spec(g.shape), _const_spec(b.shape)],
        out_specs=pl.BlockSpec((tm, d), lambda i: (i, 0)),
        out_shape=jax.ShapeDtypeStruct((n, d), F32),
        scratch_shapes=[pltpu.VMEM((tm, ff), BF16)],
        compiler_params=_cparams(("arbitrary",)),
        name="ffn_swiglu",
    )(x, w1, w3, w2, g, b)


def _head_select_mats(n_heads, head_dim):
    hw = V7X_LANES
    one_col = 3 * n_heads
    sq = np.zeros((hw, n_heads * hw), np.float32)
    sk = np.zeros((hw, n_heads * hw), np.float32)
    for h in range(n_heads):
        o = h * hw + head_dim
        for part in range(3):
            sq[part * n_heads + h, o + part] = 1.0
            sq[one_col, o + 3 + part] = 1.0
            sk[one_col, o + part] = 1.0
            sk[part * n_heads + h, o + 3 + part] = -1.0
    return jnp.asarray(sq, BF16), jnp.asarray(sk, BF16)


def _odd_prompt_kernel(n_heads, head_dim, x_ref, wqkv_ref, wf_ref, whbc_ref, fb_ref, cw_ref, tri_ref,
                       sq_ref, sk_ref, k_ref, v_ref, lf_ref, qa_ref, ka_ref, va_ref, yd_ref, cd_ref,
                       gext, fcarry):
    t = pl.program_id(1)
    tT = x_ref.shape[0]
    cwid = n_heads * head_dim
    dwid = cw_ref.shape[-1]
    taps = cw_ref.shape[0]
    GH = CONV_D_HIST_ROWS
    hw = V7X_LANES

    @pl.when(t == 0)
    def _():
        gext[0:GH, :] = jnp.zeros((GH, dwid), F32)
        fcarry[...] = jnp.zeros(fcarry.shape, F32)

    @pl.when(t > 0)
    def _():
        gext[0:GH, :] = gext[tT:tT + GH, :]

    xb = x_ref[...].astype(BF16)
    qkv = _dot(xb, wqkv_ref[...])
    q = qkv[:, :cwid]
    k = qkv[:, cwid:2 * cwid]
    v = qkv[:, 2 * cwid:]
    k_ref[...] = k
    v_ref[...] = v

    lane = lax.broadcasted_iota(I32, (tT, hw), 1)
    fl = _dot(xb, wf_ref[...])
    logf = jnp.where(lane < n_heads, _log_sigmoid(fl + fb_ref[...]), 0.0)
    lf_ref[...] = logf[:, :n_heads]

    tri = tri_ref[...]
    l_hi, l_mid, l_lo = _split3(logf)
    F = _dot(tri, l_hi) + _dot(tri, l_mid) + _dot(tri, l_lo) + fcarry[...]
    fcarry[...] = F[tT - 1:tT, :]

    f_hi, f_mid, f_lo = _split3(F)
    fparts = (f_hi.astype(F32) + pltpu.roll(f_mid.astype(F32), n_heads, axis=1)
              + pltpu.roll(f_lo.astype(F32), 2 * n_heads, axis=1)
              + jnp.where(lane == 3 * n_heads, 1.0, 0.0)).astype(BF16)
    xq = _dot(fparts, sq_ref[...])
    xk = _dot(fparts, sk_ref[...])
    vone = jnp.where(lane == head_dim, 1.0, 0.0)
    scale = head_dim ** -0.5
    for h in range(n_heads):
        p = (h * head_dim) // hw
        qp = q[:, p * hw:(p + 1) * hw] * scale
        kp = k[:, p * hw:(p + 1) * hw]
        vp = v[:, p * hw:(p + 1) * hw]
        if (h * head_dim) % hw:
            sh = hw - (h * head_dim) % hw
            qp = pltpu.roll(qp, sh, axis=1)
            kp = pltpu.roll(kp, sh, axis=1)
            vp = pltpu.roll(vp, sh, axis=1)
        qa_ref[h] = jnp.where(lane < head_dim, qp, xq[:, h * hw:(h + 1) * hw]).astype(BF16)
        ka_ref[h] = jnp.where(lane < head_dim, kp, xk[:, h * hw:(h + 1) * hw]).astype(BF16)
        va_ref[h] = jnp.where(lane < head_dim, vp, vone).astype(BF16)

    hbc = _dot(xb, whbc_ref[...])
    hh = hbc[:, :dwid]
    bg = hbc[:, dwid:2 * dwid]
    cg = hbc[:, 2 * dwid:]
    g = cg * hh
    gext[GH:, :] = g
    conv = g * cw_ref[taps - 1:taps, :]
    for kk in range(taps - 1):
        back = taps - 1 - kk
        conv = conv + gext[GH - back:GH - back + tT, :] * cw_ref[kk:kk + 1, :]
    yd_ref[...] = (bg * conv).astype(BF16)
    cd_ref[...] = gext[tT:tT + GH, :]


def _odd_prompt(n_heads, head_dim, x, wqkv, wf, whbc, fb, cw):
    B, T, D = x.shape
    tT = min(TOKEN_TILE, T)
    cwid = n_heads * head_dim
    dwid = cw.shape[-1]
    hw = V7X_LANES
    GH = CONV_D_HIST_ROWS
    tri = jnp.tri(tT, dtype=BF16)
    sq, sk = _head_select_mats(n_heads, head_dim)
    consts = (wqkv, wf, whbc, fb, cw, tri, sq, sk)
    tok = lambda w: pl.BlockSpec((None, tT, w), lambda bi, ti: (bi, ti, 0))
    head = pl.BlockSpec((None, n_heads, tT, hw), lambda bi, ti: (bi, 0, ti, 0))
    return pl.pallas_call(
        functools.partial(_odd_prompt_kernel, n_heads, head_dim),
        grid=(B, T // tT),
        in_specs=[tok(D)] + [_const_spec(c.shape) for c in consts],
        out_specs=[tok(cwid), tok(cwid), tok(n_heads), head, head, head, tok(dwid),
                   pl.BlockSpec((None, GH, dwid), lambda bi, ti: (bi, 0, 0))],
        out_shape=[jax.ShapeDtypeStruct((B, T, cwid), F32), jax.ShapeDtypeStruct((B, T, cwid), F32),
                   jax.ShapeDtypeStruct((B, T, n_heads), F32),
                   jax.ShapeDtypeStruct((B, n_heads, T, hw), BF16),
                   jax.ShapeDtypeStruct((B, n_heads, T, hw), BF16),
                   jax.ShapeDtypeStruct((B, n_heads, T, hw), BF16),
                   jax.ShapeDtypeStruct((B, T, dwid), BF16),
                   jax.ShapeDtypeStruct((B, GH, dwid), F32)],
        scratch_shapes=[pltpu.VMEM((GH + tT, dwid), F32), pltpu.VMEM((1, hw), F32)],
        compiler_params=_cparams(("arbitrary", "arbitrary")),
        name="odd_proj_prompt",
    )(x, *consts)


def _odd_sample_kernel(n_heads, head_dim, x_ref, sd_ref, wqkv_ref, wf_ref, whbc_ref, fb_ref, cw_ref,
                       q_ref, k_ref, v_ref, lf_ref, yd_ref, sdo_ref):
    cwid = n_heads * head_dim
    dwid = cw_ref.shape[-1]
    taps = cw_ref.shape[0]
    nh = sd_ref.shape[0]
    xb = x_ref[...].astype(BF16)
    qkv = _dot(xb, wqkv_ref[...])
    q_ref[...] = qkv[:, :cwid]
    k_ref[...] = qkv[:, cwid:2 * cwid]
    v_ref[...] = qkv[:, 2 * cwid:]
    fl = _dot(xb, wf_ref[...])
    lf_ref[...] = _log_sigmoid(fl + fb_ref[...])
    hbc = _dot(xb, whbc_ref[...])
    g = hbc[:, 2 * dwid:] * hbc[:, :dwid]
    conv = g * cw_ref[taps - 1:taps, :]
    for kk in range(taps - 1):
        conv = conv + sd_ref[kk + nh - (taps - 1)] * cw_ref[kk:kk + 1, :]
    yd_ref[...] = (hbc[:, dwid:2 * dwid] * conv).astype(BF16)
    for j in range(nh - 1):
        sdo_ref[j] = sd_ref[j + 1]
    sdo_ref[nh - 1] = g


def _odd_sample(n_heads, head_dim, x, sd_t, wqkv, wf, whbc, fb, cw):
    n = x.shape[0]
    cwid = n_heads * head_dim
    dwid = cw.shape[-1]
    ins = (x, sd_t, wqkv, wf, whbc, fb, cw)
    shapes = [((n, cwid), F32), ((n, cwid), F32), ((n, cwid), F32), ((n, V7X_LANES), F32),
              ((n, dwid), BF16), (sd_t.shape, F32)]
    return pl.pallas_call(
        functools.partial(_odd_sample_kernel, n_heads, head_dim),
        grid=(1,),
        in_specs=[_const_spec(c.shape) for c in ins],
        out_specs=[_const_spec(s) for s, _ in shapes],
        out_shape=[jax.ShapeDtypeStruct(s, d) for s, d in shapes],
        compiler_params=_cparams(("arbitrary",)),
        name="odd_proj_sample",
    )(*ins)


def _fox_prompt_kernel(head_dim, qa_ref, ka_ref, va_ref, o_ref):
    i = pl.program_id(2)
    tq = qa_ref.shape[1]
    hw = V7X_LANES
    lane = lax.broadcasted_iota(I32, (tq, hw), 1)
    row = lax.broadcasted_iota(I32, (tq, tq), 0)
    col = lax.broadcasted_iota(I32, (tq, tq), 1)
    outs = []
    for hh in range(qa_ref.shape[0]):
        q = qa_ref[hh]

        def step(j, carry, masked):
            m, acc = carry
            start = pl.multiple_of(j * tq, tq)
            kt = ka_ref[hh, pl.ds(start, tq), :]
            vt = va_ref[hh, pl.ds(start, tq), :]
            s = _dot_nt(q, kt)
            if masked:
                s = jnp.where(col <= row, s, NEG)
            m_new = jnp.maximum(m, jnp.max(s, axis=-1, keepdims=True))
            p = jnp.exp(s - m_new)
            acc = acc * jnp.exp(m - m_new) + _dot(p.astype(BF16), vt)
            return m_new, acc

        carry = (jnp.full((tq, 1), NEG, F32), jnp.zeros((tq, hw), F32))
        carry = lax.fori_loop(0, i, lambda j, c: step(j, c, False), carry)
        _, acc = step(i, carry, True)
        denom = jnp.sum(jnp.where(lane == head_dim, acc, 0.0), axis=-1, keepdims=True)
        outs.append(acc / denom)
    o = outs[0]
    for hh in range(1, len(outs)):
        o = jnp.where(lane < hh * head_dim, o, pltpu.roll(outs[hh], hh * head_dim, axis=1))
    o_ref[...] = o.astype(o_ref.dtype)


def _fox_prompt(head_dim, qa, ka, va):
    B, H, T, hw = qa.shape
    tq = min(TOKEN_TILE, T)
    hp = hw // head_dim
    return pl.pallas_call(
        functools.partial(_fox_prompt_kernel, head_dim),
        grid=(B, H // hp, T // tq),
        in_specs=[pl.BlockSpec((None, hp, tq, hw), lambda b, p, i: (b, p, i, 0)),
                  pl.BlockSpec((None, hp, T, hw), lambda b, p, i: (b, p, 0, 0)),
                  pl.BlockSpec((None, hp, T, hw), lambda b, p, i: (b, p, 0, 0))],
        out_specs=pl.BlockSpec((None, tq, hw), lambda b, p, i: (b, i, p)),
        out_shape=jax.ShapeDtypeStruct((B, T, H * head_dim), BF16),
        compiler_params=_cparams(("arbitrary", "arbitrary", "arbitrary")),
        name="fox_attention_prompt",
    )(qa, ka, va)


def _fox_sample_kernel(npp, pt_ref, *refs):
    k_refs = refs[0:npp]
    v_refs = refs[npp:2 * npp]
    lf_refs = refs[2 * npp:3 * npp]
    qt_ref, knt_ref, vnt_ref, lfn_ref, u2_ref, o_ref, m_s, l_s, acc_s, carry_s = refs[3 * npp:]
    i = pl.program_id(1)
    n_steps = pl.num_programs(1)
    page, H, dh = k_refs[0].shape
    rows = page * H

    @pl.when(i == 0)
    def _():
        m_s[...] = jnp.full(m_s.shape, NEG, F32)
        l_s[...] = jnp.zeros(l_s.shape, F32)
        acc_s[...] = jnp.zeros(acc_s.shape, F32)
        carry_s[...] = lfn_ref[...]

    qt = qt_ref[...]
    u2 = u2_ref[...]
    r_h = lax.broadcasted_iota(I32, (rows, H), 0) % H
    c_h = lax.broadcasted_iota(I32, (rows, H), 1)
    diag = r_h == c_h
    m = m_s[...]
    l = l_s[...]
    acc = acc_s[...]
    carry = carry_s[...]
    for j in range(npp):
        k2 = k_refs[j][...].reshape(rows, dh).astype(BF16)
        v2 = v_refs[j][...].reshape(rows, dh).astype(BF16)
        lf = lf_refs[j][...]
        l_hi, l_mid, l_lo = _split3(lf)
        s = _dot(k2, qt) + _dot(u2, l_hi) + _dot(u2, l_mid) + _dot(u2, l_lo) + carry
        s = jnp.where(diag, s, NEG)
        m_new = jnp.maximum(m, jnp.max(s, axis=0, keepdims=True))
        p = jnp.where(diag, jnp.exp(s - m_new), 0.0)
        c = jnp.exp(m - m_new)
        l = l * c + jnp.sum(p, axis=0, keepdims=True)
        acc = acc * c + _dot_tn(v2, p.astype(BF16))
        m = m_new
        carry = carry + jnp.sum(lf, axis=0, keepdims=True)
    m_s[...] = m
    l_s[...] = l
    acc_s[...] = acc
    carry_s[...] = carry

    @pl.when(i == n_steps - 1)
    def _():
        s_new = jnp.sum(qt.astype(F32) * knt_ref[...], axis=0, keepdims=True)
        m_fin = jnp.maximum(m, s_new)
        c = jnp.exp(m - m_fin)
        p_new = jnp.exp(s_new - m_fin)
        o_ref[...] = (acc * c + p_new * vnt_ref[...]) / (l * c + p_new)


def _fox_sample(page_table, cache_k, cache_v, cache_lf, qt, knt, vnt, lfn):
    nb, n_pages = page_table.shape
    _, page, H, dh = cache_k.shape
    npp = min(PAGES_PER_STEP, n_pages)
    assert n_pages % npp == 0
    n_steps = n_pages // npp
    rows = page * H
    tpos = np.arange(rows) // H
    u2 = jnp.asarray((np.arange(page)[None, :] > tpos[:, None]).astype(np.float32), BF16)

    def page_spec(shape, j):
        nd = len(shape)
        return pl.BlockSpec((None,) + tuple(shape),
                            lambda b, i, pt, j=j: (pt[b, n_pages - 1 - (i * npp + j)],) + (0,) * nd)

    seq = lambda shape: pl.BlockSpec((None,) + tuple(shape), lambda b, i, pt: (b,) + (0,) * len(shape))
    in_specs = ([page_spec((page, H, dh), j) for j in range(npp)]
                + [page_spec((page, H, dh), j) for j in range(npp)]
                + [page_spec((page, H), j) for j in range(npp)]
                + [seq((dh, H)), seq((dh, H)), seq((dh, H)), seq((1, H)),
                   pl.BlockSpec(u2.shape, lambda b, i, pt: (0, 0))])
    grid_spec = pltpu.PrefetchScalarGridSpec(
        num_scalar_prefetch=1, grid=(nb, n_steps), in_specs=in_specs,
        out_specs=seq((dh, H)),
        scratch_shapes=[pltpu.VMEM((1, H), F32), pltpu.VMEM((1, H), F32), pltpu.VMEM((dh, H), F32),
                        pltpu.VMEM((1, H), F32)])
    return pl.pallas_call(
        functools.partial(_fox_sample_kernel, npp),
        grid_spec=grid_spec,
        out_shape=jax.ShapeDtypeStruct((nb, dh, H), F32),
        compiler_params=_cparams(("arbitrary", "arbitrary")),
        name="fox_attention_sample",
    )(page_table, *([cache_k] * npp), *([cache_v] * npp), *([cache_lf] * npp), qt, knt, vnt, lfn, u2)


def _top2(logits, axis, n_experts):
    idx = lax.broadcasted_iota(I32, logits.shape, axis)
    big = logits.shape[axis]
    lg = jnp.where(idx < n_experts, logits, -jnp.inf)
    v1 = jnp.max(lg, axis=axis, keepdims=True)
    i1 = jnp.min(jnp.where(lg == v1, idx, big), axis=axis, keepdims=True)
    lg2 = jnp.where(idx == i1, -jnp.inf, lg)
    v2 = jnp.max(lg2, axis=axis, keepdims=True)
    i2 = jnp.min(jnp.where(lg2 == v2, idx, big), axis=axis, keepdims=True)
    return idx, i1, i2, v1, v2


def _odd_out_kernel(alpha, n_experts, n_valid, has_alias, *refs):
    if has_alias:
        refs = refs[3:]
    o_ref, yd_ref, x_ref, wout_ref, g_ref, b_ref, rw_ref, x3_ref, x3b_ref, lg_ref, cnt_ref = refs
    n = o_ref.shape[0]
    tm = x3b_ref.shape[0]
    cwid = o_ref.shape[1]
    mix = _dot(o_ref[...].astype(BF16), wout_ref[0:cwid, :]) + _dot(yd_ref[...], wout_ref[cwid:, :])
    x3 = _ln(alpha * x_ref[...] + mix, g_ref[...], b_ref[...])
    x3b = x3.astype(BF16)
    logits = _dot(x3b, rw_ref[...])
    x3_ref[...] = x3
    if n < tm:
        x3b_ref[...] = jnp.zeros(x3b_ref.shape, BF16)
        lg_ref[...] = jnp.zeros(lg_ref.shape, F32)
    x3b_ref[0:n, :] = x3b
    lg_ref[0:n, :] = logits
    idx, i1, i2, _, _ = _top2(logits, 1, n_experts)
    valid = lax.broadcasted_iota(I32, (n, 1), 0) < n_valid
    mask = jnp.where(((idx == i1) | (idx == i2)) & valid, 1.0, 0.0)
    cnt_ref[...] = jnp.sum(mask, axis=0, keepdims=True)


def _odd_out(alpha, n_experts, tile0, n_tiles_total, o, yd, x, wout, g, b, rw, alias=None):
    n, d = x.shape
    tm = TOKEN_TILE
    nt = max(n // tm, 1)
    blk = min(n, tm)
    ntot = n_tiles_total * tm
    hw = V7X_LANES
    row = lambda w: pl.BlockSpec((blk, w), lambda i: (i, 0))
    ins = [o, yd, x, wout, g, b, rw]
    in_specs = [row(o.shape[1]), row(yd.shape[1]), row(d)] + [_const_spec(c.shape) for c in (wout, g, b, rw)]
    aliases = {}
    if alias is not None:
        ins = list(alias) + ins
        in_specs = [pl.BlockSpec(memory_space=pl.ANY)] * 3 + in_specs
        aliases = {0: 1, 1: 2, 2: 3}
    return pl.pallas_call(
        functools.partial(_odd_out_kernel, alpha, n_experts, n, alias is not None),
        grid=(nt,),
        in_specs=in_specs,
        out_specs=[row(d),
                   pl.BlockSpec((tm, d), lambda i: (tile0 + i, 0)),
                   pl.BlockSpec((tm, hw), lambda i: (tile0 + i, 0)),
                   pl.BlockSpec((None, 1, hw), lambda i: (tile0 + i, 0, 0))],
        out_shape=[jax.ShapeDtypeStruct((n, d), F32), jax.ShapeDtypeStruct((ntot, d), BF16),
                   jax.ShapeDtypeStruct((ntot, hw), F32),
                   jax.ShapeDtypeStruct((n_tiles_total, 1, hw), F32)],
        input_output_aliases=aliases,
        compiler_params=_cparams(("arbitrary",)),
        name="odd_out_router",
    )(*ins)


def _segment_copies(tile, n_experts, npc_ref, loff_ref, goff_ref, make_copy):
    ratio = SEG_BIG // SEG_PAD
    for e in range(n_experts):
        n = npc_ref[tile * n_experts + e]
        lo = loff_ref[tile * n_experts + e]
        go = goff_ref[tile * n_experts + e]
        nbig = n // ratio

        def big(j, _, lo=lo, go=go):
            make_copy(pl.multiple_of(lo + j * SEG_BIG, SEG_PAD), pl.multiple_of(go + j * SEG_BIG, SEG_PAD), SEG_BIG)
            return 0

        def small(j, _, lo=lo, go=go, nbig=nbig):
            off = nbig * SEG_BIG + j * SEG_PAD
            make_copy(pl.multiple_of(lo + off, SEG_PAD), pl.multiple_of(go + off, SEG_PAD), SEG_PAD)
            return 0

        lax.fori_loop(0, nbig, big, 0)
        lax.fori_loop(0, n - nbig * ratio, small, 0)


def _dispatch_kernel(n_experts, n_valid, npc_ref, loff_ref, goff_ref, x_ref, lt_ref, lofft_ref, triu_ref,
                     xs_in_ref, xs_ref, stage, sem):
    del xs_in_ref
    tile = pl.program_id(0)
    tt = x_ref.shape[0]
    rl = stage.shape[0]
    sub, i1, i2, _, _ = _top2(lt_ref[...], 0, n_experts)
    valid = (tile * tt + lax.broadcasted_iota(I32, (1, tt), 1)) < n_valid
    sel1 = (sub == i1) & valid
    sel2 = (sub == i2) & valid
    mask = jnp.where(sel1 | sel2, 1.0, 0.0)
    rank = _dot(mask.astype(BF16), triu_ref[...])
    loc = rank + jnp.concatenate([lofft_ref[...]] * (tt // V7X_LANES), axis=1)
    lr1 = jnp.sum(jnp.where(sel1, loc, 0.0), axis=0, keepdims=True).astype(I32)
    lr2 = jnp.sum(jnp.where(sel2, loc, 0.0), axis=0, keepdims=True).astype(I32)
    lr1 = jnp.where(valid, lr1, -1)
    lr2 = jnp.where(valid, lr2, -1)
    r = lax.broadcasted_iota(I32, (rl, tt), 0)
    onehot = jnp.where((r == lr1) | (r == lr2), 1.0, 0.0).astype(BF16)
    stage[...] = _dot(onehot, x_ref[...]).astype(BF16)

    def copy(lo, go, nrows):
        return pltpu.make_async_copy(stage.at[pl.ds(lo, nrows), :], xs_ref.at[pl.ds(go, nrows), :], sem)

    _segment_copies(tile, n_experts, npc_ref, loff_ref, goff_ref, lambda lo, go, n: copy(lo, go, n).start())
    _segment_copies(tile, n_experts, npc_ref, loff_ref, goff_ref, lambda lo, go, n: copy(lo, go, n).wait())


def _dispatch(n_experts, n_valid, npc, loff, goff, x3b, lt, lofft, xs_zero):
    ntot, d = x3b.shape
    tt = TOKEN_TILE
    triu = jnp.asarray(np.triu(np.ones((tt, tt), np.float32), 1), BF16)
    grid_spec = pltpu.PrefetchScalarGridSpec(
        num_scalar_prefetch=3, grid=(ntot // tt,),
        in_specs=[pl.BlockSpec((tt, d), lambda i, *_: (i, 0)),
                  pl.BlockSpec((n_experts, tt), lambda i, *_: (0, i)),
                  pl.BlockSpec((None, n_experts, V7X_LANES), lambda i, *_: (i, 0, 0)),
                  pl.BlockSpec(triu.shape, lambda i, *_: (0, 0)),
                  pl.BlockSpec(memory_space=pl.ANY)],
        out_specs=pl.BlockSpec(memory_space=pl.ANY),
        scratch_shapes=[pltpu.VMEM((LOCAL_ROWS, d), BF16), pltpu.SemaphoreType.DMA(())])
    return pl.pallas_call(
        functools.partial(_dispatch_kernel, n_experts, n_valid),
        grid_spec=grid_spec,
        out_shape=jax.ShapeDtypeStruct(xs_zero.shape, BF16),
        input_output_aliases={7: 0},
        compiler_params=_cparams(("arbitrary",)),
        name="moe_dispatch",
    )(npc, loff, goff, x3b, lt, lofft, triu, xs_zero)


def _experts_kernel(te_ref, tv_ref, xr_ref, x_ref, w1_ref, w3_ref, w2_ref, yh_ref, yl_ref, acc):
    del te_ref, xr_ref
    i = pl.program_id(0)
    c = pl.program_id(1)
    last = pl.num_programs(1) - 1
    valid = tv_ref[i] > 0

    @pl.when(valid)
    def _():
        x = x_ref[...]
        h = (_silu(_dot(x, w1_ref[...])) * _dot(x, w3_ref[...])).astype(BF16)
        part = _dot(h, w2_ref[...])

        @pl.when(c == 0)
        def _():
            acc[...] = part

        @pl.when(c > 0)
        def _():
            acc[...] += part

        @pl.when(c == last)
        def _():
            y = acc[...]
            hi = y.astype(BF16)
            yh_ref[...] = hi
            yl_ref[...] = (y - hi.astype(F32)).astype(BF16)

    @pl.when(jnp.logical_not(valid) & (c == last))
    def _():
        yh_ref[...] = jnp.zeros(yh_ref.shape, BF16)
        yl_ref[...] = jnp.zeros(yl_ref.shape, BF16)


def _experts(te, tv, xr, xs, w1, w3, w2):
    rmax, d = xs.shape
    tm = EXPERT_TILE
    ff = w1.shape[-1]
    fc = EXPERT_FF_CHUNK
    nch = ff // fc
    assert ff % fc == 0 and rmax % tm == 0

    def chunk(i, c, tv):
        return jnp.where(tv[i] > 0, c, nch - 1)

    grid_spec = pltpu.PrefetchScalarGridSpec(
        num_scalar_prefetch=3, grid=(rmax // tm, nch),
        in_specs=[pl.BlockSpec((tm, d), lambda i, c, te, tv, xr: (xr[i], 0)),
                  pl.BlockSpec((None, d, fc), lambda i, c, te, tv, xr: (te[i], 0, chunk(i, c, tv))),
                  pl.BlockSpec((None, d, fc), lambda i, c, te, tv, xr: (te[i], 0, chunk(i, c, tv))),
                  pl.BlockSpec((None, fc, d), lambda i, c, te, tv, xr: (te[i], chunk(i, c, tv), 0))],
        out_specs=[pl.BlockSpec((tm, d), lambda i, c, te, tv, xr: (i, 0)),
                   pl.BlockSpec((tm, d), lambda i, c, te, tv, xr: (i, 0))],
        scratch_shapes=[pltpu.VMEM((tm, d), F32)])
    return pl.pallas_call(
        _experts_kernel,
        grid_spec=grid_spec,
        out_shape=[jax.ShapeDtypeStruct((rmax, d), BF16), jax.ShapeDtypeStruct((rmax, d), BF16)],
        compiler_params=_cparams(("arbitrary", "arbitrary")),
        name="moe_experts",
    )(te, tv, xr, xs, w1, w3, w2)


def _combine_kernel(alpha, n_experts, n_valid, n_prompt_tiles, npc_ref, loff_ref, goff_ref, lg_ref, xp_ref,
                    xs_ref, loffr_ref, tril_ref, g_ref, b_ref, yh_hbm, yl_hbm, yp_ref, ysm_ref,
                    sth, stl, sem):
    tile = pl.program_id(0)
    tt = lg_ref.shape[0]
    rl = sth.shape[0]
    d = sth.shape[1]

    def copies(fn):
        _segment_copies(tile, n_experts, npc_ref, loff_ref, goff_ref,
                        lambda lo, go, n: fn(pltpu.make_async_copy(yh_hbm.at[pl.ds(go, n), :],
                                                                  sth.at[pl.ds(lo, n), :], sem.at[0])))
        _segment_copies(tile, n_experts, npc_ref, loff_ref, goff_ref,
                        lambda lo, go, n: fn(pltpu.make_async_copy(yl_hbm.at[pl.ds(go, n), :],
                                                                  stl.at[pl.ds(lo, n), :], sem.at[1])))

    copies(lambda cp: cp.start())

    last = tile * n_experts + n_experts - 1
    used = loff_ref[last] // SEG_PAD + npc_ref[last]

    def clear(j, _):
        o = pl.multiple_of(j * SEG_PAD, SEG_PAD)
        sth[pl.ds(o, SEG_PAD), :] = jnp.zeros((SEG_PAD, d), BF16)
        stl[pl.ds(o, SEG_PAD), :] = jnp.zeros((SEG_PAD, d), BF16)
        return 0

    lax.fori_loop(used, rl // SEG_PAD, clear, 0)

    lane, i1, i2, v1, v2 = _top2(lg_ref[...], 1, n_experts)
    valid = (tile * tt + lax.broadcasted_iota(I32, (tt, 1), 0)) < n_valid
    sel1 = (lane == i1) & valid
    sel2 = (lane == i2) & valid
    mask = jnp.where(sel1 | sel2, 1.0, 0.0)
    rank = _dot(tril_ref[...], mask.astype(BF16))
    loc = rank + jnp.concatenate([loffr_ref[...]] * (tt // V7X_SUBLANES), axis=0)
    lr1 = jnp.sum(jnp.where(sel1, loc, 0.0), axis=1, keepdims=True).astype(I32)
    lr2 = jnp.sum(jnp.where(sel2, loc, 0.0), axis=1, keepdims=True).astype(I32)
    lr1 = jnp.where(valid, lr1, -1)
    lr2 = jnp.where(valid, lr2, -1)
    e21 = jnp.exp(v2 - v1)
    g1 = 1.0 / (1.0 + e21)
    g2 = e21 / (1.0 + e21)
    r = lax.broadcasted_iota(I32, (tt, rl), 1)
    gm = jnp.where(r == lr1, g1, 0.0) + jnp.where(r == lr2, g2, 0.0)
    gm_hi = gm.astype(BF16)
    gm_lo = (gm - gm_hi.astype(F32)).astype(BF16)

    copies(lambda cp: cp.wait())
    yh = sth[...]
    moe = _dot(gm_hi, yh) + _dot(gm_hi, stl[...]) + _dot(gm_lo, yh)

    @pl.when(tile < n_prompt_tiles)
    def _():
        yp_ref[...] = _ln(alpha * xp_ref[...] + moe, g_ref[...], b_ref[...])

    @pl.when(tile >= n_prompt_tiles)
    def _():
        ns = ysm_ref.shape[0]
        ysm_ref[...] = _ln(alpha * xs_ref[...] + moe[0:ns, :], g_ref[...], b_ref[...])


def _combine(alpha, n_experts, n_valid, npc, loff, goff, logits, x3p, x3s, loffr, g, b, yh, yl):
    npr, d = x3p.shape
    ns = x3s.shape[0]
    tt = TOKEN_TILE
    npt = npr // tt
    ntiles = logits.shape[0] // tt
    tril = jnp.asarray(np.tril(np.ones((tt, tt), np.float32), -1), BF16)
    grid_spec = pltpu.PrefetchScalarGridSpec(
        num_scalar_prefetch=3, grid=(ntiles,),
        in_specs=[pl.BlockSpec((tt, V7X_LANES), lambda i, *_: (i, 0)),
                  pl.BlockSpec((tt, d), lambda i, *_: (jnp.minimum(i, npt - 1), 0)),
                  pl.BlockSpec((ns, d), lambda i, *_: (0, 0)),
                  pl.BlockSpec((None, V7X_SUBLANES, V7X_LANES), lambda i, *_: (i, 0, 0)),
                  pl.BlockSpec(tril.shape, lambda i, *_: (0, 0)),
                  pl.BlockSpec(g.shape, lambda i, *_: (0, 0)),
                  pl.BlockSpec(b.shape, lambda i, *_: (0, 0)),
                  pl.BlockSpec(memory_space=pl.ANY),
                  pl.BlockSpec(memory_space=pl.ANY)],
        out_specs=[pl.BlockSpec((tt, d), lambda i, *_: (jnp.minimum(i, npt - 1), 0)),
                   pl.BlockSpec((ns, d), lambda i, *_: (0, 0))],
        scratch_shapes=[pltpu.VMEM((LOCAL_ROWS, d), BF16), pltpu.VMEM((LOCAL_ROWS, d), BF16),
                        pltpu.SemaphoreType.DMA((2,))])
    return pl.pallas_call(
        functools.partial(_combine_kernel, alpha, n_experts, n_valid, npt),
        grid_spec=grid_spec,
        out_shape=[jax.ShapeDtypeStruct((npr, d), F32), jax.ShapeDtypeStruct((ns, d), F32)],
        compiler_params=_cparams(("arbitrary",)),
        name="moe_combine",
    )(npc, loff, goff, logits, x3p, x3s, loffr, tril, g, b, yh, yl)


def _routing_plan(cnt, n_experts):
    ntiles = cnt.shape[0]
    pc = (cnt + SEG_PAD - 1) // SEG_PAD * SEG_PAD
    loff = jnp.cumsum(pc, axis=1) - pc
    per_expert = jnp.sum(pc, axis=0)
    gp = (per_expert + EXPERT_TILE - 1) // EXPERT_TILE * EXPERT_TILE
    gend = jnp.cumsum(gp)
    goff = (gend - gp)[None, :] + jnp.cumsum(pc, axis=0) - pc
    rmax = TOP_K * ntiles * TOKEN_TILE + ntiles * n_experts * (SEG_PAD - 1) + n_experts * (EXPERT_TILE - 1)
    nt_max = -(-rmax // EXPERT_TILE)
    tiles_used = gend[-1] // EXPERT_TILE
    ti = jnp.arange(nt_max, dtype=I32)
    tv = (ti < tiles_used).astype(I32)
    xr = jnp.minimum(ti, tiles_used - 1)
    te = jnp.minimum(jnp.searchsorted(gend // EXPERT_TILE, xr, side="right"), n_experts - 1).astype(I32)
    flat = lambda a: a.reshape(-1).astype(I32)
    return (flat(pc // SEG_PAD), flat(loff), flat(goff), loff.astype(F32), te, tv, xr.astype(I32),
            nt_max * EXPERT_TILE)


def kernel(x_prompt, x_sample, state_pool, state_conv_b, cache_k, cache_v, cache_logf, state_conv_d, page_table,
           w_in_even, pool_w, pool_scale, conv_b_w, conv_b_bias, conv_ln_g, conv_ln_b, w_out_even, ln_mix_even_g,
           ln_mix_even_b, ffn_w1, ffn_w3, ffn_w2, ln_ffn_even_g, ln_ffn_even_b, w_in_odd, forget_bias, conv_d_w,
           w_out_odd, ln_mix_odd_g, ln_mix_odd_b, router_w, moe_w1, moe_w3, moe_w2, ln_ffn_odd_g, ln_ffn_odd_b):
    assert w_in_even.shape[0] == 1 and w_in_odd.shape[0] == 1, "one even and one odd layer are supported"
    depth = w_in_even.shape[0] + w_in_odd.shape[0]
    alpha = float((2 * depth) ** 0.25)
    B, T, D = x_prompt.shape
    nb = x_sample.shape[0]
    assert x_sample.shape[1] == 1 and T % TOKEN_TILE == 0 and nb <= TOKEN_TILE
    n_heads = forget_bias.shape[-1]
    cwid = cache_k.shape[-1] * cache_k.shape[-2]
    head_dim = cache_k.shape[-1]
    dwid = conv_d_w.shape[-1]
    n_experts = router_w.shape[-1]
    past_len = page_table.shape[1] * cache_k.shape[2]
    hw = V7X_LANES
    bf = lambda w: w.astype(BF16)
    rowv = lambda w: w.reshape(1, -1)

    ew = (bf(w_in_even[0]), bf(pool_w[0]), pool_scale, conv_b_w[0], conv_b_bias, conv_ln_g, conv_ln_b,
          bf(w_out_even[0]), ln_mix_even_g, ln_mix_even_b)
    xp, a_hist, u_hist = _even_prompt(alpha, x_prompt, *ew)
    xs, pool_s_t, convb_s_t = _even_sample(
        alpha, past_len, x_sample.reshape(nb, D), jnp.swapaxes(state_pool[0], 0, 1),
        jnp.swapaxes(state_conv_b[0], 0, 1), *ew)
    nph = state_pool.shape[2]
    nch = state_conv_b.shape[2]
    pool_p = a_hist[None, :, POOL_HIST_ROWS - nph:, :]
    convb_p = u_hist[None, :, CONV_B_HIST_ROWS - nch:, :]
    pool_s = jnp.swapaxes(pool_s_t, 0, 1)[None]
    convb_s = jnp.swapaxes(convb_s_t, 0, 1)[None]

    fw = (bf(ffn_w1[0]), bf(ffn_w3[0]), bf(ffn_w2[0]), ln_ffn_even_g, ln_ffn_even_b)
    xp = _ffn(alpha, xp.reshape(B * T, D), *fw)
    xs = _ffn(alpha, xs, *fw)

    w_in = w_in_odd[0]
    wqkv = bf(w_in[:, :3 * cwid])
    wf = bf(jnp.pad(w_in[:, 3 * cwid:3 * cwid + n_heads], ((0, 0), (0, hw - n_heads))))
    whbc = bf(w_in[:, 3 * cwid + n_heads:])
    fb = jnp.pad(forget_bias, ((0, 0), (0, hw - n_heads)))
    k_p, v_p, lf_p, qa, ka, va, yd_p, cd_p = _odd_prompt(
        n_heads, head_dim, xp.reshape(B, T, D), wqkv, wf, whbc, fb, conv_d_w[0])
    o_p = _fox_prompt(head_dim, qa, ka, va)

    q_s, k_s, v_s, lf_s, yd_s, convd_s_t = _odd_sample(
        n_heads, head_dim, xs, jnp.swapaxes(state_conv_d[0], 0, 1), wqkv, wf, whbc, fb, conv_d_w[0])
    heads_t = lambda z: jnp.swapaxes(z.reshape(nb, n_heads, head_dim), 1, 2)
    qt = (heads_t(q_s) * head_dim ** -0.5).astype(BF16)
    lf_s = lf_s[:, :n_heads]
    o_s_t = _fox_sample(page_table, cache_k[0], cache_v[0], cache_logf[0], qt, heads_t(k_s), heads_t(v_s),
                        lf_s.reshape(nb, 1, n_heads))
    o_s = jnp.swapaxes(o_s_t, 1, 2).reshape(nb, cwid)

    n_prompt = B * T
    npt = n_prompt // TOKEN_TILE
    ntiles = npt + 1
    n_valid = n_prompt + nb
    ow = (bf(w_out_odd[0]), ln_mix_odd_g, ln_mix_odd_b,
          bf(jnp.pad(router_w[0], ((0, 0), (0, hw - n_experts)))))
    x3p, x3b, logits, cnt = _odd_out(alpha, n_experts, 0, ntiles, o_p.reshape(n_prompt, cwid),
                                     yd_p.reshape(n_prompt, dwid), xp, *ow)
    x3s, x3b, logits, cnt = _odd_out(alpha, n_experts, npt, ntiles, o_s, yd_s, xs, *ow,
                                     alias=(x3b, logits, cnt))

    cnt = cnt[:, 0, :n_experts].astype(I32)
    npc, loff, goff, loff_f, te, tv, xr, rmax = _routing_plan(cnt, n_experts)
    lt = jnp.transpose(logits[:, :n_experts])
    lofft = jnp.broadcast_to(loff_f[:, :, None], (ntiles, n_experts, hw))
    loffr = jnp.broadcast_to(jnp.pad(loff_f, ((0, 0), (0, hw - n_experts)))[:, None, :],
                             (ntiles, V7X_SUBLANES, hw))
    xs_sorted = _dispatch(n_experts, n_valid, npc, loff, goff, x3b, lt, lofft, jnp.zeros((rmax, D), BF16))
    yh, yl = _experts(te, tv, xr, xs_sorted, bf(moe_w1[0]), bf(moe_w3[0]), bf(moe_w2[0]))
    y_p, y_s = _combine(alpha, n_experts, n_valid, npc, loff, goff, logits, x3p, x3s, loffr,
                        ln_ffn_odd_g, ln_ffn_odd_b, yh, yl)

    nd = state_conv_d.shape[2]
    return (y_p.reshape(B, T, D), y_s.reshape(nb, 1, D),
            pool_p, pool_s, convb_p, convb_s,
            k_p.reshape(1, B, T, n_heads, head_dim), k_s.reshape(1, nb, 1, n_heads, head_dim),
            v_p.reshape(1, B, T, n_heads, head_dim), v_s.reshape(1, nb, 1, n_heads, head_dim),
            lf_p.reshape(1, B, T, n_heads), lf_s.reshape(1, nb, 1, n_heads),
            cd_p[None, :, CONV_D_HIST_ROWS - nd:, :], jnp.swapaxes(convd_s_t, 0, 1)[None])
```

```python
import functools

import numpy as np
import jax
import jax.numpy as jnp
from jax import lax
from jax.experimental import pallas as pl
from jax.experimental.pallas import tpu as pltpu

F32 = jnp.float32
BF16 = jnp.bfloat16
I32 = jnp.int32

LN_EPS = 1e-5
POOL_WINDOWS = (2, 4, 8, 16)
TOP_K = 2
NEG = -1e30

V7X_VMEM_BYTES = 64 * 1024 * 1024
V7X_LANES = 128
V7X_SUBLANES = 8
BF16_ROWS_PER_TILE = 2 * V7X_SUBLANES

VMEM_LIMIT = V7X_VMEM_BYTES - 8 * 1024 * 1024

TOKEN_TILE = 512
EXPERT_TILE = 512
EXPERT_FF_CHUNK = 512
FFN_CHUNK = 256
CONV_ROWS = 64
POOL_HIST_ROWS = 16
CONV_B_HIST_ROWS = 32
CONV_D_HIST_ROWS = 8
PAGES_PER_STEP = 16
SEG_PAD = BF16_ROWS_PER_TILE
SEG_BIG = 128
LOCAL_ROWS = TOP_K * TOKEN_TILE + 8 * SEG_PAD


def _cparams(sem):
    return pltpu.CompilerParams(dimension_semantics=sem, vmem_limit_bytes=VMEM_LIMIT)


def _dot(a, b):
    return jnp.dot(a, b, preferred_element_type=F32)


def _dot_nt(a, b):
    return lax.dot_general(a, b, (((1,), (1,)), ((), ())), preferred_element_type=F32)


def _dot_tn(a, b):
    return lax.dot_general(a, b, (((0,), (0,)), ((), ())), preferred_element_type=F32)


def _ln(z, g, b):
    mu = jnp.mean(z, axis=-1, keepdims=True)
    d = z - mu
    var = jnp.mean(d * d, axis=-1, keepdims=True)
    return d * lax.rsqrt(var + LN_EPS) * g + b


def _silu(x):
    return x * jax.nn.sigmoid(x)


def _log_sigmoid(z):
    return jnp.minimum(z, 0.0) - jnp.log1p(jnp.exp(-jnp.abs(z)))


def _split3(x):
    hi = x.astype(BF16)
    r = x - hi.astype(F32)
    mid = r.astype(BF16)
    lo = (r - mid.astype(F32)).astype(BF16)
    return hi, mid, lo


def _const_spec(shape):
    nd = len(shape)
    return pl.BlockSpec(shape, lambda *_: (0,) * nd)


def _resident_spec(shape):
    nd = len(shape)
    return pl.BlockSpec(shape, lambda *_: (0,) * nd, pipeline_mode=pl.Buffered(1))


def _pool_groups(a, hist_fn, pos, pw_ref):
    outs = []
    gw = a.shape[1] // len(POOL_WINDOWS)
    for g, w in enumerate(POOL_WINDOWS):
        c0 = g * gw
        cur = a[:, c0:c0 + gw]
        win = cur
        for k in range(1, w):
            win = win + hist_fn(k, c0, gw)
        cnt = jnp.minimum(pos + 1, w).astype(F32)
        pooled = win / cnt - cur
        outs.append(_dot(pooled.astype(BF16), pw_ref[g]))
    return jnp.concatenate(outs, axis=-1)


def _even_prompt_kernel(alpha, x_ref, win_ref, pw_ref, ps_ref, cw_ref, cb_ref, cg_ref, cbeta_ref,
                        wout_ref, g_ref, b_ref, y_ref, ah_ref, uh_ref, aext, uext, ybuf):
    t = pl.program_id(1)
    tT = x_ref.shape[0]
    aw = ps_ref.shape[-1]
    bw = cw_ref.shape[-1]
    taps = cw_ref.shape[0]
    AH, UH = POOL_HIST_ROWS, CONV_B_HIST_ROWS

    @pl.when(t == 0)
    def _():
        aext[0:AH, :] = jnp.zeros((AH, aw), F32)
        uext[0:UH, :] = jnp.zeros((UH, bw), F32)

    @pl.when(t > 0)
    def _():
        aext[0:AH, :] = aext[tT:tT + AH, :]
        uext[0:UH, :] = uext[tT:tT + UH, :]

    x = x_ref[...]
    proj = _dot(x.astype(BF16), win_ref[...])
    a = proj[:, :aw]
    u = proj[:, aw:aw + bw] * jax.nn.sigmoid(proj[:, aw + bw:])
    aext[AH:, :] = a
    uext[UH:, :] = u

    pos = t * tT + lax.broadcasted_iota(I32, (tT, 1), 0)
    ya = _pool_groups(a, lambda k, c0, gw: aext[AH - k:AH - k + tT, c0:c0 + gw], pos, pw_ref)
    ya = ya * ps_ref[...]

    base = UH - (taps - 1)
    for r0 in range(0, tT, CONV_ROWS):
        acc = uext[base + r0:base + r0 + CONV_ROWS, :] * cw_ref[0:1, :]
        for k in range(1, taps):
            acc = acc + uext[base + k + r0:base + k + r0 + CONV_ROWS, :] * cw_ref[k:k + 1, :]
        yb = _silu(_ln(acc + cb_ref[...], cg_ref[...], cbeta_ref[...]))
        ybuf[r0:r0 + CONV_ROWS, :] = yb.astype(BF16)

    mix = _dot(ya.astype(BF16), wout_ref[0:aw, :]) + _dot(ybuf[...], wout_ref[aw:, :])
    y_ref[...] = _ln(alpha * x + mix, g_ref[...], b_ref[...])
    ah_ref[...] = aext[tT:tT + AH, :]
    uh_ref[...] = uext[tT:tT + UH, :]


def _even_prompt(alpha, x, win, pw, ps, cw, cb, cg, cbeta, wout, g, b):
    B, T, D = x.shape
    tT = min(TOKEN_TILE, T)
    aw, bw = ps.shape[-1], cw.shape[-1]
    AH, UH = POOL_HIST_ROWS, CONV_B_HIST_ROWS
    consts = (win, pw, ps, cw, cb, cg, cbeta, wout, g, b)
    return pl.pallas_call(
        functools.partial(_even_prompt_kernel, alpha),
        grid=(B, T // tT),
        in_specs=[pl.BlockSpec((None, tT, D), lambda bi, ti: (bi, ti, 0))]
        + [_const_spec(c.shape) for c in consts],
        out_specs=[pl.BlockSpec((None, tT, D), lambda bi, ti: (bi, ti, 0)),
                   pl.BlockSpec((None, AH, aw), lambda bi, ti: (bi, 0, 0)),
                   pl.BlockSpec((None, UH, bw), lambda bi, ti: (bi, 0, 0))],
        out_shape=[jax.ShapeDtypeStruct((B, T, D), F32),
                   jax.ShapeDtypeStruct((B, AH, aw), F32),
                   jax.ShapeDtypeStruct((B, UH, bw), F32)],
        scratch_shapes=[pltpu.VMEM((AH + tT, aw), F32), pltpu.VMEM((UH + tT, bw), F32),
                        pltpu.VMEM((tT, bw), BF16)],
        compiler_params=_cparams(("arbitrary", "arbitrary")),
        name="even_mixer_prompt",
    )(x, *consts)


def _even_sample_kernel(alpha, first_pos, x_ref, sp_ref, sc_ref, win_ref, pw_ref, ps_ref, cw_ref, cb_ref,
                        cg_ref, cbeta_ref, wout_ref, g_ref, b_ref, y_ref, spo_ref, sco_ref):
    aw = ps_ref.shape[-1]
    bw = cw_ref.shape[-1]
    taps = cw_ref.shape[0]
    nph = sp_ref.shape[0]
    nch = sc_ref.shape[0]
    x = x_ref[...]
    proj = _dot(x.astype(BF16), win_ref[...])
    a = proj[:, :aw]
    u = proj[:, aw:aw + bw] * jax.nn.sigmoid(proj[:, aw + bw:])

    pos = jnp.full((x.shape[0], 1), first_pos, I32)
    ya = _pool_groups(a, lambda k, c0, gw: sp_ref[nph - k, :, c0:c0 + gw], pos, pw_ref)
    ya = ya * ps_ref[...]

    acc = u * cw_ref[taps - 1:taps, :]
    for k in range(taps - 1):
        acc = acc + sc_ref[k + nch - (taps - 1)] * cw_ref[k:k + 1, :]
    yb = _silu(_ln(acc + cb_ref[...], cg_ref[...], cbeta_ref[...]))

    mix = _dot(ya.astype(BF16), wout_ref[0:aw, :]) + _dot(yb.astype(BF16), wout_ref[aw:, :])
    y_ref[...] = _ln(alpha * x + mix, g_ref[...], b_ref[...])
    for j in range(nph - 1):
        spo_ref[j] = sp_ref[j + 1]
    spo_ref[nph - 1] = a
    for j in range(nch - 1):
        sco_ref[j] = sc_ref[j + 1]
    sco_ref[nch - 1] = u


def _even_sample(alpha, first_pos, x, sp_t, sc_t, win, pw, ps, cw, cb, cg, cbeta, wout, g, b):
    ins = (x, sp_t, sc_t, win, pw, ps, cw, cb, cg, cbeta, wout, g, b)
    return pl.pallas_call(
        functools.partial(_even_sample_kernel, alpha, first_pos),
        grid=(1,),
        in_specs=[_const_spec(c.shape) for c in ins],
        out_specs=[_const_spec(x.shape), _const_spec(sp_t.shape), _const_spec(sc_t.shape)],
        out_shape=[jax.ShapeDtypeStruct(x.shape, F32), jax.ShapeDtypeStruct(sp_t.shape, F32),
                   jax.ShapeDtypeStruct(sc_t.shape, F32)],
        compiler_params=_cparams(("arbitrary",)),
        name="even_mixer_sample",
    )(*ins)


def _ffn_kernel(alpha, x_ref, w1_ref, w3_ref, w2_ref, g_ref, b_ref, y_ref, h_ref):
    x = x_ref[...]
    xb = x.astype(BF16)
    ff = w1_ref.shape[1]
    for c in range(0, ff, FFN_CHUNK):
        h1 = _dot(xb, w1_ref[:, c:c + FFN_CHUNK])
        h3 = _dot(xb, w3_ref[:, c:c + FFN_CHUNK])
        h_ref[:, c:c + FFN_CHUNK] = (_silu(h1) * h3).astype(BF16)
    y = _dot(h_ref[...], w2_ref[...])
    y_ref[...] = _ln(alpha * x + y, g_ref[...], b_ref[...])


def _ffn(alpha, x, w1, w3, w2, g, b):
    n, d = x.shape
    tm = min(TOKEN_TILE, n)
    ff = w1.shape[1]
    assert ff % FFN_CHUNK == 0 and n % tm == 0
    return pl.pallas_call(
        functools.partial(_ffn_kernel, alpha),
        grid=(n // tm,),
        in_specs=[pl.BlockSpec((tm, d), lambda i: (i, 0)),
                  _resident_spec(w1.shape), _resident_spec(w3.shape), _resident_spec(w2.shape),
                  _const_spec(g.shape), _const_spec(b.shape)],
        out_specs=pl.BlockSpec((tm, d), lambda i: (i, 0)),
        out_shape=jax.ShapeDtypeStruct((n, d), F32),
        scratch_shapes=[pltpu.VMEM((tm, ff), BF16)],
        compiler_params=_cparams(("arbitrary",)),
        name="ffn_swiglu",
    )(x, w1, w3, w2, g, b)


def _head_select_mats(n_heads, head_dim):
    hw = V7X_LANES
    one_col = 3 * n_heads
    sq = np.zeros((hw, n_heads * hw), np.float32)
    sk = np.zeros((hw, n_heads * hw), np.float32)
    for h in range(n_heads):
        o = h * hw + head_dim
        for part in range(3):
            sq[part * n_heads + h, o + part] = 1.0
            sq[one_col, o + 3 + part] = 1.0
            sk[one_col, o + part] = 1.0
            sk[part * n_heads + h, o + 3 + part] = -1.0
    return jnp.asarray(sq, BF16), jnp.asarray(sk, BF16)


def _odd_prompt_kernel(n_heads, head_dim, x_ref, wqkv_ref, wf_ref, whbc_ref, fb_ref, cw_ref, tri_ref,
                       sq_ref, sk_ref, k_ref, v_ref, lf_ref, qa_ref, ka_ref, va_ref, yd_ref, cd_ref,
                       gext, fcarry):
    t = pl.program_id(1)
    tT = x_ref.shape[0]
    cwid = n_heads * head_dim
    dwid = cw_ref.shape[-1]
    taps = cw_ref.shape[0]
    GH = CONV_D_HIST_ROWS
    hw = V7X_LANES

    @pl.when(t == 0)
    def _():
        gext[0:GH, :] = jnp.zeros((GH, dwid), F32)
        fcarry[...] = jnp.zeros(fcarry.shape, F32)

    @pl.when(t > 0)
    def _():
        gext[0:GH, :] = gext[tT:tT + GH, :]

    xb = x_ref[...].astype(BF16)
    qkv = _dot(xb, wqkv_ref[...])
    q = qkv[:, :cwid]
    k = qkv[:, cwid:2 * cwid]
    v = qkv[:, 2 * cwid:]
    k_ref[...] = k
    v_ref[...] = v

    lane = lax.broadcasted_iota(I32, (tT, hw), 1)
    fl = _dot(xb, wf_ref[...])
    logf = jnp.where(lane < n_heads, _log_sigmoid(fl + fb_ref[...]), 0.0)
    lf_ref[...] = logf[:, :n_heads]

    tri = tri_ref[...]
    l_hi, l_mid, l_lo = _split3(logf)
    F = _dot(tri, l_hi) + _dot(tri, l_mid) + _dot(tri, l_lo) + fcarry[...]
    fcarry[...] = F[tT - 1:tT, :]

    f_hi, f_mid, f_lo = _split3(F)
    fparts = (f_hi.astype(F32) + pltpu.roll(f_mid.astype(F32), n_heads, axis=1)
              + pltpu.roll(f_lo.astype(F32), 2 * n_heads, axis=1)
              + jnp.where(lane == 3 * n_heads, 1.0, 0.0)).astype(BF16)
    xq = _dot(fparts, sq_ref[...])
    xk = _dot(fparts, sk_ref[...])
    vone = jnp.where(lane == head_dim, 1.0, 0.0)
    scale = head_dim ** -0.5
    for h in range(n_heads):
        p = (h * head_dim) // hw
        qp = q[:, p * hw:(p + 1) * hw] * scale
        kp = k[:, p * hw:(p + 1) * hw]
        vp = v[:, p * hw:(p + 1) * hw]
        if (h * head_dim) % hw:
            sh = hw - (h * head_dim) % hw
            qp = pltpu.roll(qp, sh, axis=1)
            kp = pltpu.roll(kp, sh, axis=1)
            vp = pltpu.roll(vp, sh, axis=1)
        qa_ref[h] = jnp.where(lane < head_dim, qp, xq[:, h * hw:(h + 1) * hw]).astype(BF16)
        ka_ref[h] = jnp.where(lane < head_dim, kp, xk[:, h * hw:(h + 1) * hw]).astype(BF16)
        va_ref[h] = jnp.where(lane < head_dim, vp, vone).astype(BF16)

    hbc = _dot(xb, whbc_ref[...])
    hh = hbc[:, :dwid]
    bg = hbc[:, dwid:2 * dwid]
    cg = hbc[:, 2 * dwid:]
    g = cg * hh
    gext[GH:, :] = g
    conv = g * cw_ref[taps - 1:taps, :]
    for kk in range(taps - 1):
        back = taps - 1 - kk
        conv = conv + gext[GH - back:GH - back + tT, :] * cw_ref[kk:kk + 1, :]
    yd_ref[...] = (bg * conv).astype(BF16)
    cd_ref[...] = gext[tT:tT + GH, :]


def _odd_prompt(n_heads, head_dim, x, wqkv, wf, whbc, fb, cw):
    B, T, D = x.shape
    tT = min(TOKEN_TILE, T)
    cwid = n_heads * head_dim
    dwid = cw.shape[-1]
    hw = V7X_LANES
    GH = CONV_D_HIST_ROWS
    tri = jnp.tri(tT, dtype=BF16)
    sq, sk = _head_select_mats(n_heads, head_dim)
    consts = (wqkv, wf, whbc, fb, cw, tri, sq, sk)
    tok = lambda w: pl.BlockSpec((None, tT, w), lambda bi, ti: (bi, ti, 0))
    head = pl.BlockSpec((None, n_heads, tT, hw), lambda bi, ti: (bi, 0, ti, 0))
    return pl.pallas_call(
        functools.partial(_odd_prompt_kernel, n_heads, head_dim),
        grid=(B, T // tT),
        in_specs=[tok(D)] + [_const_spec(c.shape) for c in consts],
        out_specs=[tok(cwid), tok(cwid), tok(n_heads), head, head, head, tok(dwid),
                   pl.BlockSpec((None, GH, dwid), lambda bi, ti: (bi, 0, 0))],
        out_shape=[jax.ShapeDtypeStruct((B, T, cwid), F32), jax.ShapeDtypeStruct((B, T, cwid), F32),
                   jax.ShapeDtypeStruct((B, T, n_heads), F32),
                   jax.ShapeDtypeStruct((B, n_heads, T, hw), BF16),
                   jax.ShapeDtypeStruct((B, n_heads, T, hw), BF16),
                   jax.ShapeDtypeStruct((B, n_heads, T, hw), BF16),
                   jax.ShapeDtypeStruct((B, T, dwid), BF16),
                   jax.ShapeDtypeStruct((B, GH, dwid), F32)],
        scratch_shapes=[pltpu.VMEM((GH + tT, dwid), F32), pltpu.VMEM((1, hw), F32)],
        compiler_params=_cparams(("arbitrary", "arbitrary")),
        name="odd_proj_prompt",
    )(x, *consts)


def _odd_sample_kernel(n_heads, head_dim, x_ref, sd_ref, wqkv_ref, wf_ref, whbc_ref, fb_ref, cw_ref,
                       q_ref, k_ref, v_ref, lf_ref, yd_ref, sdo_ref):
    cwid = n_heads * head_dim
    dwid = cw_ref.shape[-1]
    taps = cw_ref.shape[0]
    nh = sd_ref.shape[0]
    xb = x_ref[...].astype(BF16)
    qkv = _dot(xb, wqkv_ref[...])
    q_ref[...] = qkv[:, :cwid]
    k_ref[...] = qkv[:, cwid:2 * cwid]
    v_ref[...] = qkv[:, 2 * cwid:]
    fl = _dot(xb, wf_ref[...])
    lf_ref[...] = _log_sigmoid(fl + fb_ref[...])
    hbc = _dot(xb, whbc_ref[...])
    g = hbc[:, 2 * dwid:] * hbc[:, :dwid]
    conv = g * cw_ref[taps - 1:taps, :]
    for kk in range(taps - 1):
        conv = conv + sd_ref[kk + nh - (taps - 1)] * cw_ref[kk:kk + 1, :]
    yd_ref[...] = (hbc[:, dwid:2 * dwid] * conv).astype(BF16)
    for j in range(nh - 1):
        sdo_ref[j] = sd_ref[j + 1]
    sdo_ref[nh - 1] = g


def _odd_sample(n_heads, head_dim, x, sd_t, wqkv, wf, whbc, fb, cw):
    n = x.shape[0]
    cwid = n_heads * head_dim
    dwid = cw.shape[-1]
    ins = (x, sd_t, wqkv, wf, whbc, fb, cw)
    shapes = [((n, cwid), F32), ((n, cwid), F32), ((n, cwid), F32), ((n, V7X_LANES), F32),
              ((n, dwid), BF16), (sd_t.shape, F32)]
    return pl.pallas_call(
        functools.partial(_odd_sample_kernel, n_heads, head_dim),
        grid=(1,),
        in_specs=[_const_spec(c.shape) for c in ins],
        out_specs=[_const_spec(s) for s, _ in shapes],
        out_shape=[jax.ShapeDtypeStruct(s, d) for s, d in shapes],
        compiler_params=_cparams(("arbitrary",)),
        name="odd_proj_sample",
    )(*ins)


def _fox_prompt_kernel(head_dim, qa_ref, ka_ref, va_ref, o_ref):
    i = pl.program_id(2)
    tq = qa_ref.shape[1]
    hw = V7X_LANES
    lane = lax.broadcasted_iota(I32, (tq, hw), 1)
    row = lax.broadcasted_iota(I32, (tq, tq), 0)
    col = lax.broadcasted_iota(I32, (tq, tq), 1)
    outs = []
    for hh in range(qa_ref.shape[0]):
        q = qa_ref[hh]

        def step(j, carry, masked):
            m, acc = carry
            start = pl.multiple_of(j * tq, tq)
            kt = ka_ref[hh, pl.ds(start, tq), :]
            vt = va_ref[hh, pl.ds(start, tq), :]
            s = _dot_nt(q, kt)
            if masked:
                s = jnp.where(col <= row, s, NEG)
            m_new = jnp.maximum(m, jnp.max(s, axis=-1, keepdims=True))
            p = jnp.exp(s - m_new)
            acc = acc * jnp.exp(m - m_new) + _dot(p.astype(BF16), vt)
            return m_new, acc

        carry = (jnp.full((tq, 1), NEG, F32), jnp.zeros((tq, hw), F32))
        carry = lax.fori_loop(0, i, lambda j, c: step(j, c, False), carry)
        _, acc = step(i, carry, True)
        denom = jnp.sum(jnp.where(lane == head_dim, acc, 0.0), axis=-1, keepdims=True)
        outs.append(acc / denom)
    o = outs[0]
    for hh in range(1, len(outs)):
        o = jnp.where(lane < hh * head_dim, o, pltpu.roll(outs[hh], hh * head_dim, axis=1))
    o_ref[...] = o.astype(o_ref.dtype)


def _fox_prompt(head_dim, qa, ka, va):
    B, H, T, hw = qa.shape
    tq = min(TOKEN_TILE, T)
    hp = hw // head_dim
    return pl.pallas_call(
        functools.partial(_fox_prompt_kernel, head_dim),
        grid=(B, H // hp, T // tq),
        in_specs=[pl.BlockSpec((None, hp, tq, hw), lambda b, p, i: (b, p, i, 0)),
                  pl.BlockSpec((None, hp, T, hw), lambda b, p, i: (b, p, 0, 0)),
                  pl.BlockSpec((None, hp, T, hw), lambda b, p, i: (b, p, 0, 0))],
        out_specs=pl.BlockSpec((None, tq, hw), lambda b, p, i: (b, i, p)),
        out_shape=jax.ShapeDtypeStruct((B, T, H * head_dim), BF16),
        compiler_params=_cparams(("arbitrary", "arbitrary", "arbitrary")),
        name="fox_attention_prompt",
    )(qa, ka, va)


def _fox_sample_kernel(npp, pt_ref, *refs):
    k_refs = refs[0:npp]
    v_refs = refs[npp:2 * npp]
    lf_refs = refs[2 * npp:3 * npp]
    (qt_ref, qh_ref, knh_ref, vnt_ref, lfn_ref, u_ref, o_ref, m_s, l_s, acc_s, carry_s) = refs[3 * npp:]
    i = pl.program_id(1)
    n_steps = pl.num_programs(1)
    H, dh, page = k_refs[0].shape

    @pl.when(i == 0)
    def _():
        m_s[...] = jnp.full(m_s.shape, NEG, F32)
        l_s[...] = jnp.zeros(l_s.shape, F32)
        acc_s[...] = jnp.zeros(acc_s.shape, F32)
        carry_s[...] = lfn_ref[...]

    lf_all = jnp.concatenate([r[...] for r in lf_refs], axis=0)
    l_hi, l_mid, l_lo = _split3(lf_all)
    u = u_ref[...]
    g_all = _dot(l_hi, u) + _dot(l_mid, u) + _dot(l_lo, u)
    tot = jnp.sum(lf_all, axis=1, keepdims=True)
    later = carry_s[...]
    decay = [None] * npp
    for j in reversed(range(npp)):
        decay[j] = later
        later = later + tot[j * H:(j + 1) * H, :]
    carry_s[...] = later

    qt = qt_ref[...]
    qb = [jnp.broadcast_to(qt[:, h:h + 1], (dh, page)) for h in range(H)]
    s_pages = []
    for j in range(npp):
        rows = [jnp.sum(k_refs[j][h] * qb[h], axis=0, keepdims=True) for h in range(H)]
        s_pages.append(jnp.concatenate(rows, axis=0) + g_all[j * H:(j + 1) * H, :] + decay[j])

    mx = s_pages[0]
    for j in range(1, npp):
        mx = jnp.maximum(mx, s_pages[j])
    m = m_s[...]
    m_new = jnp.maximum(m, jnp.max(mx, axis=1, keepdims=True))
    c = jnp.exp(m - m_new)
    p_pages = [jnp.exp(s - m_new) for s in s_pages]
    psum = p_pages[0]
    for j in range(1, npp):
        psum = psum + p_pages[j]
    l_s[...] = l_s[...] * c + psum
    m_s[...] = m_new
    for h in range(H):
        acc = acc_s[h] * c[h:h + 1, :]
        for j in range(npp):
            acc = acc + p_pages[j][h:h + 1, :] * v_refs[j][h]
        acc_s[h] = acc

    @pl.when(i == n_steps - 1)
    def _():
        s_new = jnp.sum(qh_ref[...] * knh_ref[...], axis=1, keepdims=True)
        m_fin = jnp.maximum(m_new, s_new)
        cf = jnp.exp(m_new - m_fin)
        p_new = jnp.exp(s_new - m_fin)
        denom = jnp.sum(l_s[...], axis=1, keepdims=True) * cf + p_new
        for h in range(H):
            num = (jnp.sum(acc_s[h], axis=1, keepdims=True) * cf[h:h + 1, :]
                   + p_new[h:h + 1, :] * vnt_ref[:, h:h + 1])
            o_ref[:, h:h + 1] = num / denom[h:h + 1, :]


def _fox_sample(page_table, kview, vview, lfview, qt, qh, knh, vnt, lfn):
    nb, n_pages = page_table.shape
    _, H, dh, page = kview.shape
    npp = min(PAGES_PER_STEP, n_pages)
    assert n_pages % npp == 0
    n_steps = n_pages // npp
    u = jnp.asarray(np.tril(np.ones((page, page), np.float32), -1), BF16)

    def page_spec(shape, j):
        nd = len(shape)
        return pl.BlockSpec((None,) + tuple(shape),
                            lambda b, i, pt, j=j: (pt[b, n_pages - npp * (i + 1) + j],) + (0,) * nd)

    seq = lambda shape: pl.BlockSpec((None,) + tuple(shape), lambda b, i, pt: (b,) + (0,) * len(shape))
    in_specs = ([page_spec((H, dh, page), j) for j in range(npp)]
                + [page_spec((H, dh, page), j) for j in range(npp)]
                + [page_spec((H, page), j) for j in range(npp)]
                + [seq((dh, H)), seq((H, dh)), seq((H, dh)), seq((dh, H)), seq((H, 1)),
                   pl.BlockSpec(u.shape, lambda b, i, pt: (0, 0))])
    grid_spec = pltpu.PrefetchScalarGridSpec(
        num_scalar_prefetch=1, grid=(nb, n_steps), in_specs=in_specs,
        out_specs=seq((dh, H)),
        scratch_shapes=[pltpu.VMEM((H, 1), F32), pltpu.VMEM((H, page), F32), pltpu.VMEM((H, dh, page), F32),
                        pltpu.VMEM((H, 1), F32)])
    return pl.pallas_call(
        functools.partial(_fox_sample_kernel, npp),
        grid_spec=grid_spec,
        out_shape=jax.ShapeDtypeStruct((nb, dh, H), F32),
        compiler_params=_cparams(("arbitrary", "arbitrary")),
        name="fox_attention_sample",
    )(page_table, *([kview] * npp), *([vview] * npp), *([lfview] * npp), qt, qh, knh, vnt, lfn, u)


def _top2(logits, axis, n_experts):
    idx = lax.broadcasted_iota(I32, logits.shape, axis)
    big = logits.shape[axis]
    lg = jnp.where(idx < n_experts, logits, -jnp.inf)
    v1 = jnp.max(lg, axis=axis, keepdims=True)
    i1 = jnp.min(jnp.where(lg == v1, idx, big), axis=axis, keepdims=True)
    lg2 = jnp.where(idx == i1, -jnp.inf, lg)
    v2 = jnp.max(lg2, axis=axis, keepdims=True)
    i2 = jnp.min(jnp.where(lg2 == v2, idx, big), axis=axis, keepdims=True)
    return idx, i1, i2, v1, v2


def _odd_out_kernel(alpha, n_experts, n_prompt_tiles, op_ref, ydp_ref, xp_ref, os_ref, yds_ref, xs_ref,
                    wout_ref, g_ref, b_ref, rw_ref, x3p_ref, x3s_ref, x3b_ref, lg_ref, cnt_ref):
    i = pl.program_id(0)
    cwid = op_ref.shape[1]

    def rows(o_ref, yd_ref, x_ref, x3_ref):
        n = o_ref.shape[0]
        mix = _dot(o_ref[...].astype(BF16), wout_ref[0:cwid, :]) + _dot(yd_ref[...], wout_ref[cwid:, :])
        x3 = _ln(alpha * x_ref[...] + mix, g_ref[...], b_ref[...])
        x3b = x3.astype(BF16)
        logits = _dot(x3b, rw_ref[...])
        x3_ref[...] = x3
        if n < x3b_ref.shape[0]:
            x3b_ref[...] = jnp.zeros(x3b_ref.shape, BF16)
            lg_ref[...] = jnp.zeros(lg_ref.shape, F32)
        x3b_ref[0:n, :] = x3b
        lg_ref[0:n, :] = logits
        idx, i1, i2, _, _ = _top2(logits, 1, n_experts)
        mask = jnp.where((idx == i1) | (idx == i2), 1.0, 0.0)
        cnt_ref[...] = jnp.sum(mask, axis=0, keepdims=True)

    @pl.when(i < n_prompt_tiles)
    def _():
        rows(op_ref, ydp_ref, xp_ref, x3p_ref)

    @pl.when(i >= n_prompt_tiles)
    def _():
        rows(os_ref, yds_ref, xs_ref, x3s_ref)


def _odd_out(alpha, n_experts, o_p, yd_p, x_p, o_s, yd_s, x_s, wout, g, b, rw):
    n, d = x_p.shape
    ns = x_s.shape[0]
    tm = TOKEN_TILE
    npt = n // tm
    ntiles = npt + 1
    hw = V7X_LANES
    prow = lambda w: pl.BlockSpec((tm, w), lambda i: (jnp.minimum(i, npt - 1), 0))
    return pl.pallas_call(
        functools.partial(_odd_out_kernel, alpha, n_experts, npt),
        grid=(ntiles,),
        in_specs=[prow(o_p.shape[1]), prow(yd_p.shape[1]), prow(d),
                  _const_spec(o_s.shape), _const_spec(yd_s.shape), _const_spec(x_s.shape)]
        + [_const_spec(c.shape) for c in (wout, g, b, rw)],
        out_specs=[prow(d), _const_spec((ns, d)),
                   pl.BlockSpec((tm, d), lambda i: (i, 0)),
                   pl.BlockSpec((tm, hw), lambda i: (i, 0)),
                   pl.BlockSpec((None, 1, hw), lambda i: (i, 0, 0))],
        out_shape=[jax.ShapeDtypeStruct((n, d), F32), jax.ShapeDtypeStruct((ns, d), F32),
                   jax.ShapeDtypeStruct((ntiles * tm, d), BF16),
                   jax.ShapeDtypeStruct((ntiles * tm, hw), F32),
                   jax.ShapeDtypeStruct((ntiles, 1, hw), F32)],
        compiler_params=_cparams(("arbitrary",)),
        name="odd_out_router",
    )(o_p, yd_p, x_p, o_s, yd_s, x_s, wout, g, b, rw)


def _segment_copies(tile, n_experts, npc_ref, loff_ref, goff_ref, make_copy):
    ratio = SEG_BIG // SEG_PAD
    for e in range(n_experts):
        n = npc_ref[tile * n_experts + e]
        lo = loff_ref[tile * n_experts + e]
        go = goff_ref[tile * n_experts + e]
        nbig = n // ratio

        def big(j, _, lo=lo, go=go):
            make_copy(pl.multiple_of(lo + j * SEG_BIG, SEG_PAD), pl.multiple_of(go + j * SEG_BIG, SEG_PAD), SEG_BIG)
            return 0

        def small(j, _, lo=lo, go=go, nbig=nbig):
            off = nbig * SEG_BIG + j * SEG_PAD
            make_copy(pl.multiple_of(lo + off, SEG_PAD), pl.multiple_of(go + off, SEG_PAD), SEG_PAD)
            return 0

        lax.fori_loop(0, nbig, big, 0)
        lax.fori_loop(0, n - nbig * ratio, small, 0)


def _dispatch_kernel(n_experts, n_valid, npc_ref, loff_ref, goff_ref, x_ref, lt_ref, lofft_ref, triu_ref,
                     xs_in_ref, xs_ref, stage, sem):
    del xs_in_ref
    tile = pl.program_id(0)
    tt = x_ref.shape[0]
    rl = stage.shape[0]
    sub, i1, i2, _, _ = _top2(lt_ref[...], 0, n_experts)
    valid = (tile * tt + lax.broadcasted_iota(I32, (1, tt), 1)) < n_valid
    sel1 = (sub == i1) & valid
    sel2 = (sub == i2) & valid
    mask = jnp.where(sel1 | sel2, 1.0, 0.0)
    rank = _dot(mask.astype(BF16), triu_ref[...])
    loc = rank + jnp.concatenate([lofft_ref[...]] * (tt // V7X_LANES), axis=1)
    lr1 = jnp.sum(jnp.where(sel1, loc, 0.0), axis=0, keepdims=True).astype(I32)
    lr2 = jnp.sum(jnp.where(sel2, loc, 0.0), axis=0, keepdims=True).astype(I32)
    lr1 = jnp.where(valid, lr1, -1)
    lr2 = jnp.where(valid, lr2, -1)
    r = lax.broadcasted_iota(I32, (rl, tt), 0)
    onehot = jnp.where((r == lr1) | (r == lr2), 1.0, 0.0).astype(BF16)
    stage[...] = _dot(onehot, x_ref[...]).astype(BF16)

    def copy(lo, go, nrows):
        return pltpu.make_async_copy(stage.at[pl.ds(lo, nrows), :], xs_ref.at[pl.ds(go, nrows), :], sem)

    _segment_copies(tile, n_experts, npc_ref, loff_ref, goff_ref, lambda lo, go, n: copy(lo, go, n).start())
    _segment_copies(tile, n_experts, npc_ref, loff_ref, goff_ref, lambda lo, go, n: copy(lo, go, n).wait())


def _dispatch(n_experts, n_valid, npc, loff, goff, x3b, lt, lofft, xs_zero):
    ntot, d = x3b.shape
    tt = TOKEN_TILE
    triu = jnp.asarray(np.triu(np.ones((tt, tt), np.float32), 1), BF16)
    grid_spec = pltpu.PrefetchScalarGridSpec(
        num_scalar_prefetch=3, grid=(ntot // tt,),
        in_specs=[pl.BlockSpec((tt, d), lambda i, *_: (i, 0)),
                  pl.BlockSpec((n_experts, tt), lambda i, *_: (0, i)),
                  pl.BlockSpec((None, n_experts, V7X_LANES), lambda i, *_: (i, 0, 0)),
                  pl.BlockSpec(triu.shape, lambda i, *_: (0, 0)),
                  pl.BlockSpec(memory_space=pl.ANY)],
        out_specs=pl.BlockSpec(memory_space=pl.ANY),
        scratch_shapes=[pltpu.VMEM((LOCAL_ROWS, d), BF16), pltpu.SemaphoreType.DMA(())])
    return pl.pallas_call(
        functools.partial(_dispatch_kernel, n_experts, n_valid),
        grid_spec=grid_spec,
        out_shape=jax.ShapeDtypeStruct(xs_zero.shape, BF16),
        input_output_aliases={7: 0},
        compiler_params=_cparams(("arbitrary",)),
        name="moe_dispatch",
    )(npc, loff, goff, x3b, lt, lofft, triu, xs_zero)


def _experts_kernel(te_ref, tv_ref, xr_ref, x_ref, w1_ref, w3_ref, w2_ref, yh_ref, yl_ref, acc):
    del te_ref, xr_ref
    i = pl.program_id(0)
    c = pl.program_id(1)
    last = pl.num_programs(1) - 1
    valid = tv_ref[i] > 0

    @pl.when(valid)
    def _():
        x = x_ref[...]
        h = (_silu(_dot(x, w1_ref[...])) * _dot(x, w3_ref[...])).astype(BF16)
        part = _dot(h, w2_ref[...])

        @pl.when(c == 0)
        def _():
            acc[...] = part

        @pl.when(c > 0)
        def _():
            acc[...] += part

        @pl.when(c == last)
        def _():
            y = acc[...]
            hi = y.astype(BF16)
            yh_ref[...] = hi
            yl_ref[...] = (y - hi.astype(F32)).astype(BF16)

    @pl.when(jnp.logical_not(valid) & (c == last))
    def _():
        yh_ref[...] = jnp.zeros(yh_ref.shape, BF16)
        yl_ref[...] = jnp.zeros(yl_ref.shape, BF16)


def _experts(te, tv, xr, xs, w1, w3, w2):
    rmax, d = xs.shape
    tm = EXPERT_TILE
    ff = w1.shape[-1]
    fc = EXPERT_FF_CHUNK
    nch = ff // fc
    assert ff % fc == 0 and rmax % tm == 0

    def chunk(i, c, tv):
        return jnp.where(tv[i] > 0, c, nch - 1)

    grid_spec = pltpu.PrefetchScalarGridSpec(
        num_scalar_prefetch=3, grid=(rmax // tm, nch),
        in_specs=[pl.BlockSpec((tm, d), lambda i, c, te, tv, xr: (xr[i], 0)),
                  pl.BlockSpec((None, d, fc), lambda i, c, te, tv, xr: (te[i], 0, chunk(i, c, tv))),
                  pl.BlockSpec((None, d, fc), lambda i, c, te, tv, xr: (te[i], 0, chunk(i, c, tv))),
                  pl.BlockSpec((None, fc, d), lambda i, c, te, tv, xr: (te[i], chunk(i, c, tv), 0))],
        out_specs=[pl.BlockSpec((tm, d), lambda i, c, te, tv, xr: (i, 0)),
                   pl.BlockSpec((tm, d), lambda i, c, te, tv, xr: (i, 0))],
        scratch_shapes=[pltpu.VMEM((tm, d), F32)])
    return pl.pallas_call(
        _experts_kernel,
        grid_spec=grid_spec,
        out_shape=[jax.ShapeDtypeStruct((rmax, d), BF16), jax.ShapeDtypeStruct((rmax, d), BF16)],
        compiler_params=_cparams(("arbitrary", "arbitrary")),
        name="moe_experts",
    )(te, tv, xr, xs, w1, w3, w2)


def _combine_kernel(alpha, n_experts, n_valid, n_prompt_tiles, npc_ref, loff_ref, goff_ref, lg_ref, xp_ref,
                    xs_ref, loffr_ref, tril_ref, g_ref, b_ref, yh_hbm, yl_hbm, yp_ref, ysm_ref,
                    sth, stl, sem):
    tile = pl.program_id(0)
    tt = lg_ref.shape[0]
    rl = sth.shape[0]
    d = sth.shape[1]

    def copies(fn):
        _segment_copies(tile, n_experts, npc_ref, loff_ref, goff_ref,
                        lambda lo, go, n: fn(pltpu.make_async_copy(yh_hbm.at[pl.ds(go, n), :],
                                                                  sth.at[pl.ds(lo, n), :], sem.at[0])))
        _segment_copies(tile, n_experts, npc_ref, loff_ref, goff_ref,
                        lambda lo, go, n: fn(pltpu.make_async_copy(yl_hbm.at[pl.ds(go, n), :],
                                                                  stl.at[pl.ds(lo, n), :], sem.at[1])))

    copies(lambda cp: cp.start())

    last = tile * n_experts + n_experts - 1
    used = loff_ref[last] // SEG_PAD + npc_ref[last]

    def clear(j, _):
        o = pl.multiple_of(j * SEG_PAD, SEG_PAD)
        sth[pl.ds(o, SEG_PAD), :] = jnp.zeros((SEG_PAD, d), BF16)
        stl[pl.ds(o, SEG_PAD), :] = jnp.zeros((SEG_PAD, d), BF16)
        return 0

    lax.fori_loop(used, rl // SEG_PAD, clear, 0)

    lane, i1, i2, v1, v2 = _top2(lg_ref[...], 1, n_experts)
    valid = (tile * tt + lax.broadcasted_iota(I32, (tt, 1), 0)) < n_valid
    sel1 = (lane == i1) & valid
    sel2 = (lane == i2) & valid
    mask = jnp.where(sel1 | sel2, 1.0, 0.0)
    rank = _dot(tril_ref[...], mask.astype(BF16))
    loc = rank + jnp.concatenate([loffr_ref[...]] * (tt // V7X_SUBLANES), axis=0)
    lr1 = jnp.sum(jnp.where(sel1, loc, 0.0), axis=1, keepdims=True).astype(I32)
    lr2 = jnp.sum(jnp.where(sel2, loc, 0.0), axis=1, keepdims=True).astype(I32)
    lr1 = jnp.where(valid, lr1, -1)
    lr2 = jnp.where(valid, lr2, -1)
    e21 = jnp.exp(v2 - v1)
    g1 = 1.0 / (1.0 + e21)
    g2 = e21 / (1.0 + e21)
    r = lax.broadcasted_iota(I32, (tt, rl), 1)
    gm = jnp.where(r == lr1, g1, 0.0) + jnp.where(r == lr2, g2, 0.0)
    gm_hi = gm.astype(BF16)
    gm_lo = (gm - gm_hi.astype(F32)).astype(BF16)

    copies(lambda cp: cp.wait())
    yh = sth[...]
    moe = _dot(gm_hi, yh) + _dot(gm_hi, stl[...]) + _dot(gm_lo, yh)

    @pl.when(tile < n_prompt_tiles)
    def _():
        yp_ref[...] = _ln(alpha * xp_ref[...] + moe, g_ref[...], b_ref[...])

    @pl.when(tile >= n_prompt_tiles)
    def _():
        ns = ysm_ref.shape[0]
        ysm_ref[...] = _ln(alpha * xs_ref[...] + moe[0:ns, :], g_ref[...], b_ref[...])


def _combine(alpha, n_experts, n_valid, npc, loff, goff, logits, x3p, x3s, loffr, g, b, yh, yl):
    npr, d = x3p.shape
    ns = x3s.shape[0]
    tt = TOKEN_TILE
    npt = npr // tt
    ntiles = logits.shape[0] // tt
    tril = jnp.asarray(np.tril(np.ones((tt, tt), np.float32), -1), BF16)
    grid_spec = pltpu.PrefetchScalarGridSpec(
        num_scalar_prefetch=3, grid=(ntiles,),
        in_specs=[pl.BlockSpec((tt, V7X_LANES), lambda i, *_: (i, 0)),
                  pl.BlockSpec((tt, d), lambda i, *_: (jnp.minimum(i, npt - 1), 0)),
                  pl.BlockSpec((ns, d), lambda i, *_: (0, 0)),
                  pl.BlockSpec((None, V7X_SUBLANES, V7X_LANES), lambda i, *_: (i, 0, 0)),
                  pl.BlockSpec(tril.shape, lambda i, *_: (0, 0)),
                  pl.BlockSpec(g.shape, lambda i, *_: (0, 0)),
                  pl.BlockSpec(b.shape, lambda i, *_: (0, 0)),
                  pl.BlockSpec(memory_space=pl.ANY),
                  pl.BlockSpec(memory_space=pl.ANY)],
        out_specs=[pl.BlockSpec((tt, d), lambda i, *_: (jnp.minimum(i, npt - 1), 0)),
                   pl.BlockSpec((ns, d), lambda i, *_: (0, 0))],
        scratch_shapes=[pltpu.VMEM((LOCAL_ROWS, d), BF16), pltpu.VMEM((LOCAL_ROWS, d), BF16),
                        pltpu.SemaphoreType.DMA((2,))])
    return pl.pallas_call(
        functools.partial(_combine_kernel, alpha, n_experts, n_valid, npt),
        grid_spec=grid_spec,
        out_shape=[jax.ShapeDtypeStruct((npr, d), F32), jax.ShapeDtypeStruct((ns, d), F32)],
        compiler_params=_cparams(("arbitrary",)),
        name="moe_combine",
    )(npc, loff, goff, logits, x3p, x3s, loffr, tril, g, b, yh, yl)


def _routing_plan(cnt, n_experts):
    ntiles = cnt.shape[0]
    pc = (cnt + SEG_PAD - 1) // SEG_PAD * SEG_PAD
    loff = jnp.cumsum(pc, axis=1) - pc
    per_expert = jnp.sum(pc, axis=0)
    gp = (per_expert + EXPERT_TILE - 1) // EXPERT_TILE * EXPERT_TILE
    gend = jnp.cumsum(gp)
    goff = (gend - gp)[None, :] + jnp.cumsum(pc, axis=0) - pc
    rmax = TOP_K * ntiles * TOKEN_TILE + ntiles * n_experts * (SEG_PAD - 1) + n_experts * (EXPERT_TILE - 1)
    nt_max = -(-rmax // EXPERT_TILE)
    tiles_used = gend[-1] // EXPERT_TILE
    ti = jnp.arange(nt_max, dtype=I32)
    tv = (ti < tiles_used).astype(I32)
    xr = jnp.minimum(ti, tiles_used - 1)
    te = jnp.sum((xr[:, None] >= (gend // EXPERT_TILE)[None, :]).astype(I32), axis=1)
    te = jnp.minimum(te, n_experts - 1)
    flat = lambda a: a.reshape(-1).astype(I32)
    return (flat(pc // SEG_PAD), flat(loff), flat(goff), loff.astype(F32), te, tv, xr.astype(I32),
            nt_max * EXPERT_TILE)


def kernel(x_prompt, x_sample, state_pool, state_conv_b, cache_k, cache_v, cache_logf, state_conv_d, page_table,
           w_in_even, pool_w, pool_scale, conv_b_w, conv_b_bias, conv_ln_g, conv_ln_b, w_out_even, ln_mix_even_g,
           ln_mix_even_b, ffn_w1, ffn_w3, ffn_w2, ln_ffn_even_g, ln_ffn_even_b, w_in_odd, forget_bias, conv_d_w,
           w_out_odd, ln_mix_odd_g, ln_mix_odd_b, router_w, moe_w1, moe_w3, moe_w2, ln_ffn_odd_g, ln_ffn_odd_b):
    assert w_in_even.shape[0] == 1 and w_in_odd.shape[0] == 1, "one even and one odd layer are supported"
    depth = w_in_even.shape[0] + w_in_odd.shape[0]
    alpha = float((2 * depth) ** 0.25)
    B, T, D = x_prompt.shape
    nb = x_sample.shape[0]
    assert x_sample.shape[1] == 1 and T % TOKEN_TILE == 0 and nb <= TOKEN_TILE
    n_heads = forget_bias.shape[-1]
    cwid = cache_k.shape[-1] * cache_k.shape[-2]
    head_dim = cache_k.shape[-1]
    dwid = conv_d_w.shape[-1]
    n_experts = router_w.shape[-1]
    past_len = page_table.shape[1] * cache_k.shape[2]
    hw = V7X_LANES
    bf = lambda w: w.astype(BF16)
    rowv = lambda w: w.reshape(1, -1)

    ew = (bf(w_in_even[0]), bf(pool_w[0]), pool_scale, conv_b_w[0], conv_b_bias, conv_ln_g, conv_ln_b,
          bf(w_out_even[0]), ln_mix_even_g, ln_mix_even_b)
    xp, a_hist, u_hist = _even_prompt(alpha, x_prompt, *ew)
    xs, pool_s_t, convb_s_t = _even_sample(
        alpha, past_len, x_sample.reshape(nb, D), jnp.swapaxes(state_pool[0], 0, 1),
        jnp.swapaxes(state_conv_b[0], 0, 1), *ew)
    nph = state_pool.shape[2]
    nch = state_conv_b.shape[2]
    pool_p = a_hist[None, :, POOL_HIST_ROWS - nph:, :]
    convb_p = u_hist[None, :, CONV_B_HIST_ROWS - nch:, :]
    pool_s = jnp.swapaxes(pool_s_t, 0, 1)[None]
    convb_s = jnp.swapaxes(convb_s_t, 0, 1)[None]

    fw = (bf(ffn_w1[0]), bf(ffn_w3[0]), bf(ffn_w2[0]), ln_ffn_even_g, ln_ffn_even_b)
    xp = _ffn(alpha, xp.reshape(B * T, D), *fw)
    xs = _ffn(alpha, xs, *fw)

    w_in = w_in_odd[0]
    wqkv = bf(w_in[:, :3 * cwid])
    wf = bf(jnp.pad(w_in[:, 3 * cwid:3 * cwid + n_heads], ((0, 0), (0, hw - n_heads))))
    whbc = bf(w_in[:, 3 * cwid + n_heads:])
    fb = jnp.pad(forget_bias, ((0, 0), (0, hw - n_heads)))
    k_p, v_p, lf_p, qa, ka, va, yd_p, cd_p = _odd_prompt(
        n_heads, head_dim, xp.reshape(B, T, D), wqkv, wf, whbc, fb, conv_d_w[0])
    o_p = _fox_prompt(head_dim, qa, ka, va)

    q_s, k_s, v_s, lf_s, yd_s, convd_s_t = _odd_sample(
        n_heads, head_dim, xs, jnp.swapaxes(state_conv_d[0], 0, 1), wqkv, wf, whbc, fb, conv_d_w[0])
    heads = lambda z: z.reshape(nb, n_heads, head_dim)
    heads_t = lambda z: jnp.swapaxes(heads(z), 1, 2)
    q_sc = q_s * head_dim ** -0.5
    lf_s = lf_s[:, :n_heads]
    o_s_t = _fox_sample(page_table, jnp.transpose(cache_k[0], (0, 2, 3, 1)), jnp.transpose(cache_v[0], (0, 2, 3, 1)),
                        jnp.transpose(cache_logf[0], (0, 2, 1)), heads_t(q_sc), heads(q_sc), heads(k_s),
                        heads_t(v_s), lf_s.reshape(nb, n_heads, 1))
    o_s = jnp.swapaxes(o_s_t, 1, 2).reshape(nb, cwid)

    n_prompt = B * T
    npt = n_prompt // TOKEN_TILE
    ntiles = npt + 1
    n_valid = n_prompt + nb
    ow = (bf(w_out_odd[0]), ln_mix_odd_g, ln_mix_odd_b,
          bf(jnp.pad(router_w[0], ((0, 0), (0, hw - n_experts)))))
    x3p, x3s, x3b, logits, cnt = _odd_out(alpha, n_experts, o_p.reshape(n_prompt, cwid),
                                          yd_p.reshape(n_prompt, dwid), xp, o_s, yd_s, xs, *ow)

    cnt = cnt[:, 0, :n_experts].astype(I32)
    npc, loff, goff, loff_f, te, tv, xr, rmax = _routing_plan(cnt, n_experts)
    lt = jnp.transpose(logits[:, :n_experts])
    lofft = jnp.broadcast_to(loff_f[:, :, None], (ntiles, n_experts, hw))
    loffr = jnp.broadcast_to(jnp.pad(loff_f, ((0, 0), (0, hw - n_experts)))[:, None, :],
                             (ntiles, V7X_SUBLANES, hw))
    xs_sorted = _dispatch(n_experts, n_valid, npc, loff, goff, x3b, lt, lofft, jnp.zeros((rmax, D), BF16))
    yh, yl = _experts(te, tv, xr, xs_sorted, bf(moe_w1[0]), bf(moe_w3[0]), bf(moe_w2[0]))
    y_p, y_s = _combine(alpha, n_experts, n_valid, npc, loff, goff, logits, x3p, x3s, loffr,
                        ln_ffn_odd_g, ln_ffn_odd_b, yh, yl)

    nd = state_conv_d.shape[2]
    return (y_p.reshape(B, T, D), y_s.reshape(nb, 1, D),
            pool_p, pool_s, convb_p, convb_s,
            k_p.reshape(1, B, T, n_heads, head_dim), k_s.reshape(1, nb, 1, n_heads, head_dim),
            v_p.reshape(1, B, T, n_heads, head_dim), v_s.reshape(1, nb, 1, n_heads, head_dim),
            lf_p.reshape(1, B, T, n_heads), lf_s.reshape(1, nb, 1, n_heads),
            cd_p[None, :, CONV_D_HIST_ROWS - nd:, :], jnp.swapaxes(convd_s_t, 0, 1)[None])
```

```python
import functools

import numpy as np
import jax
import jax.numpy as jnp
from jax import lax
from jax.experimental import pallas as pl
from jax.experimental.pallas import tpu as pltpu

F32 = jnp.float32
BF16 = jnp.bfloat16
I32 = jnp.int32

LN_EPS = 1e-5
POOL_WINDOWS = (2, 4, 8, 16)
TOP_K = 2
NEG = -1e30
LOG2E = 1.4426950408889634

V7X_VMEM_BYTES = 64 * 1024 * 1024
V7X_LANES = 128
V7X_SUBLANES = 8
BF16_ROWS_PER_TILE = 2 * V7X_SUBLANES

VMEM_LIMIT = V7X_VMEM_BYTES - 8 * 1024 * 1024

TOKEN_TILE = 512
EXPERT_TILE = 512
EXPERT_FF_CHUNK = 512
FFN_CHUNK = 256
CONV_ROWS = 64
POOL_HIST_ROWS = 16
CONV_B_HIST_ROWS = 32
CONV_D_HIST_ROWS = 8
PAGES_PER_STEP = 16
SEG_PAD = BF16_ROWS_PER_TILE
SEG_BIG = 128
LOCAL_ROWS = TOP_K * TOKEN_TILE + 8 * SEG_PAD


def _cparams(sem):
    return pltpu.CompilerParams(dimension_semantics=sem, vmem_limit_bytes=VMEM_LIMIT)


def _dot(a, b):
    return jnp.dot(a, b, preferred_element_type=F32)


def _dot_nt(a, b):
    return lax.dot_general(a, b, (((1,), (1,)), ((), ())), preferred_element_type=F32)


def _dot_tn(a, b):
    return lax.dot_general(a, b, (((0,), (0,)), ((), ())), preferred_element_type=F32)


def _ln(z, g, b):
    mu = jnp.mean(z, axis=-1, keepdims=True)
    d = z - mu
    var = jnp.mean(d * d, axis=-1, keepdims=True)
    return d * lax.rsqrt(var + LN_EPS) * g + b


def _silu(x):
    return x * jax.nn.sigmoid(x)


def _log_sigmoid(z):
    return jnp.minimum(z, 0.0) - jnp.log1p(jnp.exp(-jnp.abs(z)))


def _split3(x):
    hi = x.astype(BF16)
    r = x - hi.astype(F32)
    mid = r.astype(BF16)
    lo = (r - mid.astype(F32)).astype(BF16)
    return hi, mid, lo


def _const_spec(shape):
    nd = len(shape)
    return pl.BlockSpec(shape, lambda *_: (0,) * nd)


def _resident_spec(shape):
    nd = len(shape)
    return pl.BlockSpec(shape, lambda *_: (0,) * nd, pipeline_mode=pl.Buffered(1))


def _pool_groups(a, hist_fn, pos, pw_ref):
    outs = []
    gw = a.shape[1] // len(POOL_WINDOWS)
    for g, w in enumerate(POOL_WINDOWS):
        c0 = g * gw
        cur = a[:, c0:c0 + gw]
        win = cur
        for k in range(1, w):
            win = win + hist_fn(k, c0, gw)
        cnt = jnp.minimum(pos + 1, w).astype(F32)
        pooled = win / cnt - cur
        outs.append(_dot(pooled.astype(BF16), pw_ref[g]))
    return jnp.concatenate(outs, axis=-1)


def _even_prompt_kernel(alpha, x_ref, win_ref, pw_ref, ps_ref, cw_ref, cb_ref, cg_ref, cbeta_ref,
                        wout_ref, g_ref, b_ref, y_ref, ah_ref, uh_ref, aext, uext, ybuf):
    t = pl.program_id(1)
    tT = x_ref.shape[0]
    aw = ps_ref.shape[-1]
    bw = cw_ref.shape[-1]
    taps = cw_ref.shape[0]
    AH, UH = POOL_HIST_ROWS, CONV_B_HIST_ROWS

    @pl.when(t == 0)
    def _():
        aext[0:AH, :] = jnp.zeros((AH, aw), F32)
        uext[0:UH, :] = jnp.zeros((UH, bw), F32)

    @pl.when(t > 0)
    def _():
        aext[0:AH, :] = aext[tT:tT + AH, :]
        uext[0:UH, :] = uext[tT:tT + UH, :]

    x = x_ref[...]
    proj = _dot(x.astype(BF16), win_ref[...])
    a = proj[:, :aw]
    u = proj[:, aw:aw + bw] * jax.nn.sigmoid(proj[:, aw + bw:])
    aext[AH:, :] = a
    uext[UH:, :] = u

    pos = t * tT + lax.broadcasted_iota(I32, (tT, 1), 0)
    ya = _pool_groups(a, lambda k, c0, gw: aext[AH - k:AH - k + tT, c0:c0 + gw], pos, pw_ref)
    ya = ya * ps_ref[...]

    base = UH - (taps - 1)
    for r0 in range(0, tT, CONV_ROWS):
        acc = uext[base + r0:base + r0 + CONV_ROWS, :] * cw_ref[0:1, :]
        for k in range(1, taps):
            acc = acc + uext[base + k + r0:base + k + r0 + CONV_ROWS, :] * cw_ref[k:k + 1, :]
        yb = _silu(_ln(acc + cb_ref[...], cg_ref[...], cbeta_ref[...]))
        ybuf[r0:r0 + CONV_ROWS, :] = yb.astype(BF16)

    mix = _dot(ya.astype(BF16), wout_ref[0:aw, :]) + _dot(ybuf[...], wout_ref[aw:, :])
    y_ref[...] = _ln(alpha * x + mix, g_ref[...], b_ref[...])
    ah_ref[...] = aext[tT:tT + AH, :]
    uh_ref[...] = uext[tT:tT + UH, :]


def _even_prompt(alpha, x, win, pw, ps, cw, cb, cg, cbeta, wout, g, b):
    B, T, D = x.shape
    tT = min(TOKEN_TILE, T)
    aw, bw = ps.shape[-1], cw.shape[-1]
    AH, UH = POOL_HIST_ROWS, CONV_B_HIST_ROWS
    consts = (win, pw, ps, cw, cb, cg, cbeta, wout, g, b)
    return pl.pallas_call(
        functools.partial(_even_prompt_kernel, alpha),
        grid=(B, T // tT),
        in_specs=[pl.BlockSpec((None, tT, D), lambda bi, ti: (bi, ti, 0))]
        + [_const_spec(c.shape) for c in consts],
        out_specs=[pl.BlockSpec((None, tT, D), lambda bi, ti: (bi, ti, 0)),
                   pl.BlockSpec((None, AH, aw), lambda bi, ti: (bi, 0, 0)),
                   pl.BlockSpec((None, UH, bw), lambda bi, ti: (bi, 0, 0))],
        out_shape=[jax.ShapeDtypeStruct((B, T, D), F32),
                   jax.ShapeDtypeStruct((B, AH, aw), F32),
                   jax.ShapeDtypeStruct((B, UH, bw), F32)],
        scratch_shapes=[pltpu.VMEM((AH + tT, aw), F32), pltpu.VMEM((UH + tT, bw), F32),
                        pltpu.VMEM((tT, bw), BF16)],
        compiler_params=_cparams(("arbitrary", "arbitrary")),
        name="even_mixer_prompt",
    )(x, *consts)


def _even_sample_kernel(alpha, first_pos, x_ref, sp_ref, sc_ref, win_ref, pw_ref, ps_ref, cw_ref, cb_ref,
                        cg_ref, cbeta_ref, wout_ref, g_ref, b_ref, y_ref, spo_ref, sco_ref):
    aw = ps_ref.shape[-1]
    bw = cw_ref.shape[-1]
    taps = cw_ref.shape[0]
    nph = sp_ref.shape[0]
    nch = sc_ref.shape[0]
    x = x_ref[...]
    proj = _dot(x.astype(BF16), win_ref[...])
    a = proj[:, :aw]
    u = proj[:, aw:aw + bw] * jax.nn.sigmoid(proj[:, aw + bw:])

    pos = jnp.full((x.shape[0], 1), first_pos, I32)
    ya = _pool_groups(a, lambda k, c0, gw: sp_ref[nph - k, :, c0:c0 + gw], pos, pw_ref)
    ya = ya * ps_ref[...]

    acc = u * cw_ref[taps - 1:taps, :]
    for k in range(taps - 1):
        acc = acc + sc_ref[k + nch - (taps - 1)] * cw_ref[k:k + 1, :]
    yb = _silu(_ln(acc + cb_ref[...], cg_ref[...], cbeta_ref[...]))

    mix = _dot(ya.astype(BF16), wout_ref[0:aw, :]) + _dot(yb.astype(BF16), wout_ref[aw:, :])
    y_ref[...] = _ln(alpha * x + mix, g_ref[...], b_ref[...])
    for j in range(nph - 1):
        spo_ref[j] = sp_ref[j + 1]
    spo_ref[nph - 1] = a
    for j in range(nch - 1):
        sco_ref[j] = sc_ref[j + 1]
    sco_ref[nch - 1] = u


def _even_sample(alpha, first_pos, x, sp_t, sc_t, win, pw, ps, cw, cb, cg, cbeta, wout, g, b):
    ins = (x, sp_t, sc_t, win, pw, ps, cw, cb, cg, cbeta, wout, g, b)
    return pl.pallas_call(
        functools.partial(_even_sample_kernel, alpha, first_pos),
        grid=(1,),
        in_specs=[_const_spec(c.shape) for c in ins],
        out_specs=[_const_spec(x.shape), _const_spec(sp_t.shape), _const_spec(sc_t.shape)],
        out_shape=[jax.ShapeDtypeStruct(x.shape, F32), jax.ShapeDtypeStruct(sp_t.shape, F32),
                   jax.ShapeDtypeStruct(sc_t.shape, F32)],
        compiler_params=_cparams(("arbitrary",)),
        name="even_mixer_sample",
    )(*ins)


def _ffn_kernel(alpha, x_ref, w1_ref, w3_ref, w2_ref, g_ref, b_ref, y_ref, h_ref):
    x = x_ref[...]
    xb = x.astype(BF16)
    ff = w1_ref.shape[1]
    for c in range(0, ff, FFN_CHUNK):
        h1 = _dot(xb, w1_ref[:, c:c + FFN_CHUNK])
        h3 = _dot(xb, w3_ref[:, c:c + FFN_CHUNK])
        h_ref[:, c:c + FFN_CHUNK] = (_silu(h1) * h3).astype(BF16)
    y = _dot(h_ref[...], w2_ref[...])
    y_ref[...] = _ln(alpha * x + y, g_ref[...], b_ref[...])


def _ffn(alpha, x, w1, w3, w2, g, b):
    n, d = x.shape
    tm = min(TOKEN_TILE, n)
    ff = w1.shape[1]
    assert ff % FFN_CHUNK == 0 and n % tm == 0
    return pl.pallas_call(
        functools.partial(_ffn_kernel, alpha),
        grid=(n // tm,),
        in_specs=[pl.BlockSpec((tm, d), lambda i: (i, 0)),
                  _resident_spec(w1.shape), _resident_spec(w3.shape), _resident_spec(w2.shape),
                  _const_spec(g.shape), _const_spec(b.shape)],
        out_specs=pl.BlockSpec((tm, d), lambda i: (i, 0)),
        out_shape=jax.ShapeDtypeStruct((n, d), F32),
        scratch_shapes=[pltpu.VMEM((tm, ff), BF16)],
        compiler_params=_cparams(("arbitrary",)),
        name="ffn_swiglu",
    )(x, w1, w3, w2, g, b)


def _head_select_mats(n_heads, head_dim):
    hw = V7X_LANES
    one_col = 3 * n_heads
    sq = np.zeros((hw, n_heads * hw), np.float32)
    sk = np.zeros((hw, n_heads * hw), np.float32)
    for h in range(n_heads):
        o = h * hw + head_dim
        for part in range(3):
            sq[part * n_heads + h, o + part] = 1.0
            sq[one_col, o + 3 + part] = 1.0
            sk[one_col, o + part] = 1.0
            sk[part * n_heads + h, o + 3 + part] = -1.0
    return jnp.asarray(sq, BF16), jnp.asarray(sk, BF16)


def _odd_prompt_kernel(n_heads, head_dim, x_ref, wqkv_ref, wf_ref, whbc_ref, fb_ref, cw_ref, tri_ref,
                       sq_ref, sk_ref, k_ref, v_ref, lf_ref, qa_ref, ka_ref, va_ref, yd_ref, cd_ref,
                       gext, fcarry):
    t = pl.program_id(1)
    tT = x_ref.shape[0]
    cwid = n_heads * head_dim
    dwid = cw_ref.shape[-1]
    taps = cw_ref.shape[0]
    GH = CONV_D_HIST_ROWS
    hw = V7X_LANES

    @pl.when(t == 0)
    def _():
        gext[0:GH, :] = jnp.zeros((GH, dwid), F32)
        fcarry[...] = jnp.zeros(fcarry.shape, F32)

    @pl.when(t > 0)
    def _():
        gext[0:GH, :] = gext[tT:tT + GH, :]

    xb = x_ref[...].astype(BF16)
    qkv = _dot(xb, wqkv_ref[...])
    q = qkv[:, :cwid]
    k = qkv[:, cwid:2 * cwid]
    v = qkv[:, 2 * cwid:]
    k_ref[...] = k
    v_ref[...] = v

    lane = lax.broadcasted_iota(I32, (tT, hw), 1)
    fl = _dot(xb, wf_ref[...])
    logf = jnp.where(lane < n_heads, _log_sigmoid(fl + fb_ref[...]), 0.0)
    lf_ref[...] = logf[:, :n_heads]

    tri = tri_ref[...]
    l_hi, l_mid, l_lo = _split3(logf)
    F = _dot(tri, l_hi) + _dot(tri, l_mid) + _dot(tri, l_lo) + fcarry[...]
    fcarry[...] = F[tT - 1:tT, :]

    f_hi, f_mid, f_lo = _split3(F * LOG2E)
    fparts = (f_hi.astype(F32) + pltpu.roll(f_mid.astype(F32), n_heads, axis=1)
              + pltpu.roll(f_lo.astype(F32), 2 * n_heads, axis=1)
              + jnp.where(lane == 3 * n_heads, 1.0, 0.0)).astype(BF16)
    xq = _dot(fparts, sq_ref[...])
    xk = _dot(fparts, sk_ref[...])
    vone = jnp.where(lane == head_dim, 1.0, 0.0)
    scale = head_dim ** -0.5 * LOG2E
    for h in range(n_heads):
        p = (h * head_dim) // hw
        qp = q[:, p * hw:(p + 1) * hw] * scale
        kp = k[:, p * hw:(p + 1) * hw]
        vp = v[:, p * hw:(p + 1) * hw]
        if (h * head_dim) % hw:
            sh = hw - (h * head_dim) % hw
            qp = pltpu.roll(qp, sh, axis=1)
            kp = pltpu.roll(kp, sh, axis=1)
            vp = pltpu.roll(vp, sh, axis=1)
        qa_ref[h] = jnp.where(lane < head_dim, qp, xq[:, h * hw:(h + 1) * hw]).astype(BF16)
        ka_ref[h] = jnp.where(lane < head_dim, kp, xk[:, h * hw:(h + 1) * hw]).astype(BF16)
        va_ref[h] = jnp.where(lane < head_dim, vp, vone).astype(BF16)

    hbc = _dot(xb, whbc_ref[...])
    hh = hbc[:, :dwid]
    bg = hbc[:, dwid:2 * dwid]
    cg = hbc[:, 2 * dwid:]
    g = cg * hh
    gext[GH:, :] = g
    conv = g * cw_ref[taps - 1:taps, :]
    for kk in range(taps - 1):
        back = taps - 1 - kk
        conv = conv + gext[GH - back:GH - back + tT, :] * cw_ref[kk:kk + 1, :]
    yd_ref[...] = (bg * conv).astype(BF16)
    cd_ref[...] = gext[tT:tT + GH, :]


def _odd_prompt(n_heads, head_dim, x, wqkv, wf, whbc, fb, cw):
    B, T, D = x.shape
    tT = min(TOKEN_TILE, T)
    cwid = n_heads * head_dim
    dwid = cw.shape[-1]
    hw = V7X_LANES
    GH = CONV_D_HIST_ROWS
    tri = jnp.tri(tT, dtype=BF16)
    sq, sk = _head_select_mats(n_heads, head_dim)
    consts = (wqkv, wf, whbc, fb, cw, tri, sq, sk)
    tok = lambda w: pl.BlockSpec((None, tT, w), lambda bi, ti: (bi, ti, 0))
    head = pl.BlockSpec((None, n_heads, tT, hw), lambda bi, ti: (bi, 0, ti, 0))
    return pl.pallas_call(
        functools.partial(_odd_prompt_kernel, n_heads, head_dim),
        grid=(B, T // tT),
        in_specs=[tok(D)] + [_const_spec(c.shape) for c in consts],
        out_specs=[tok(cwid), tok(cwid), tok(n_heads), head, head, head, tok(dwid),
                   pl.BlockSpec((None, GH, dwid), lambda bi, ti: (bi, 0, 0))],
        out_shape=[jax.ShapeDtypeStruct((B, T, cwid), F32), jax.ShapeDtypeStruct((B, T, cwid), F32),
                   jax.ShapeDtypeStruct((B, T, n_heads), F32),
                   jax.ShapeDtypeStruct((B, n_heads, T, hw), BF16),
                   jax.ShapeDtypeStruct((B, n_heads, T, hw), BF16),
                   jax.ShapeDtypeStruct((B, n_heads, T, hw), BF16),
                   jax.ShapeDtypeStruct((B, T, dwid), BF16),
                   jax.ShapeDtypeStruct((B, GH, dwid), F32)],
        scratch_shapes=[pltpu.VMEM((GH + tT, dwid), F32), pltpu.VMEM((1, hw), F32)],
        compiler_params=_cparams(("arbitrary", "arbitrary")),
        name="odd_proj_prompt",
    )(x, *consts)


def _odd_sample_kernel(n_heads, head_dim, x_ref, sd_ref, wqkv_ref, wf_ref, whbc_ref, fb_ref, cw_ref,
                       q_ref, k_ref, v_ref, lf_ref, yd_ref, sdo_ref):
    cwid = n_heads * head_dim
    dwid = cw_ref.shape[-1]
    taps = cw_ref.shape[0]
    nh = sd_ref.shape[0]
    xb = x_ref[...].astype(BF16)
    qkv = _dot(xb, wqkv_ref[...])
    q_ref[...] = qkv[:, :cwid]
    k_ref[...] = qkv[:, cwid:2 * cwid]
    v_ref[...] = qkv[:, 2 * cwid:]
    fl = _dot(xb, wf_ref[...])
    lf_ref[...] = _log_sigmoid(fl + fb_ref[...])
    hbc = _dot(xb, whbc_ref[...])
    g = hbc[:, 2 * dwid:] * hbc[:, :dwid]
    conv = g * cw_ref[taps - 1:taps, :]
    for kk in range(taps - 1):
        conv = conv + sd_ref[kk + nh - (taps - 1)] * cw_ref[kk:kk + 1, :]
    yd_ref[...] = (hbc[:, dwid:2 * dwid] * conv).astype(BF16)
    for j in range(nh - 1):
        sdo_ref[j] = sd_ref[j + 1]
    sdo_ref[nh - 1] = g


def _odd_sample(n_heads, head_dim, x, sd_t, wqkv, wf, whbc, fb, cw):
    n = x.shape[0]
    cwid = n_heads * head_dim
    dwid = cw.shape[-1]
    ins = (x, sd_t, wqkv, wf, whbc, fb, cw)
    shapes = [((n, cwid), F32), ((n, cwid), F32), ((n, cwid), F32), ((n, V7X_LANES), F32),
              ((n, dwid), BF16), (sd_t.shape, F32)]
    return pl.pallas_call(
        functools.partial(_odd_sample_kernel, n_heads, head_dim),
        grid=(1,),
        in_specs=[_const_spec(c.shape) for c in ins],
        out_specs=[_const_spec(s) for s, _ in shapes],
        out_shape=[jax.ShapeDtypeStruct(s, d) for s, d in shapes],
        compiler_params=_cparams(("arbitrary",)),
        name="odd_proj_sample",
    )(*ins)


def _fox_prompt_kernel(head_dim, qa_ref, ka_ref, va_ref, o_ref):
    i = pl.program_id(2)
    tq = qa_ref.shape[1]
    hw = V7X_LANES
    lane = lax.broadcasted_iota(I32, (tq, hw), 1)
    row = lax.broadcasted_iota(I32, (tq, tq), 0)
    col = lax.broadcasted_iota(I32, (tq, tq), 1)
    nh = qa_ref.shape[0]
    qs = [qa_ref[hh] for hh in range(nh)]

    def step(j, carry, masked):
        start = pl.multiple_of(j * tq, tq)
        new = []
        for hh in range(nh):
            m, acc = carry[hh]
            kt = ka_ref[hh, pl.ds(start, tq), :]
            vt = va_ref[hh, pl.ds(start, tq), :]
            s = _dot_nt(qs[hh], kt)
            if masked:
                s = jnp.where(col <= row, s, NEG)
            m_new = jnp.maximum(m, jnp.max(s, axis=-1, keepdims=True))
            p = jnp.exp2(s - m_new)
            acc = acc * jnp.exp2(m - m_new) + _dot(p.astype(BF16), vt)
            new.append((m_new, acc))
        return tuple(new)

    carry = tuple((jnp.full((tq, 1), NEG, F32), jnp.zeros((tq, hw), F32)) for _ in range(nh))
    carry = lax.fori_loop(0, i, lambda j, c: step(j, c, False), carry)
    carry = step(i, carry, True)
    outs = []
    for hh in range(nh):
        acc = carry[hh][1]
        denom = jnp.sum(jnp.where(lane == head_dim, acc, 0.0), axis=-1, keepdims=True)
        outs.append(acc / denom)
    o = outs[0]
    for hh in range(1, len(outs)):
        o = jnp.where(lane < hh * head_dim, o, pltpu.roll(outs[hh], hh * head_dim, axis=1))
    o_ref[...] = o.astype(o_ref.dtype)


def _fox_prompt(head_dim, qa, ka, va):
    B, H, T, hw = qa.shape
    tq = min(TOKEN_TILE, T)
    hp = hw // head_dim
    return pl.pallas_call(
        functools.partial(_fox_prompt_kernel, head_dim),
        grid=(B, H // hp, T // tq),
        in_specs=[pl.BlockSpec((None, hp, tq, hw), lambda b, p, i: (b, p, i, 0)),
                  pl.BlockSpec((None, hp, T, hw), lambda b, p, i: (b, p, 0, 0)),
                  pl.BlockSpec((None, hp, T, hw), lambda b, p, i: (b, p, 0, 0))],
        out_specs=pl.BlockSpec((None, tq, hw), lambda b, p, i: (b, i, p)),
        out_shape=jax.ShapeDtypeStruct((B, T, H * head_dim), BF16),
        compiler_params=_cparams(("arbitrary", "arbitrary", "arbitrary")),
        name="fox_attention_prompt",
    )(qa, ka, va)


def _fox_sample_kernel(npp, pt_ref, *refs):
    k_refs = refs[0:npp]
    v_refs = refs[npp:2 * npp]
    lf_refs = refs[2 * npp:3 * npp]
    (qt_ref, qh_ref, knh_ref, vnt_ref, lfn_ref, u_ref, o_ref, m_s, l_s, acc_s, carry_s) = refs[3 * npp:]
    i = pl.program_id(1)
    n_steps = pl.num_programs(1)
    H, dh, page = k_refs[0].shape

    @pl.when(i == 0)
    def _():
        m_s[...] = jnp.full(m_s.shape, NEG, F32)
        l_s[...] = jnp.zeros(l_s.shape, F32)
        acc_s[...] = jnp.zeros(acc_s.shape, F32)
        carry_s[...] = lfn_ref[...]

    lf_all = jnp.concatenate([r[...] for r in lf_refs], axis=0)
    l_hi, l_mid, l_lo = _split3(lf_all)
    u = u_ref[...]
    g_all = _dot(l_hi, u) + _dot(l_mid, u) + _dot(l_lo, u)
    tot = jnp.sum(lf_all, axis=1, keepdims=True)
    later = carry_s[...]
    decay = [None] * npp
    for j in reversed(range(npp)):
        decay[j] = later
        later = later + tot[j * H:(j + 1) * H, :]
    carry_s[...] = later

    qt = qt_ref[...]
    qb = [jnp.broadcast_to(qt[:, h:h + 1], (dh, page)) for h in range(H)]
    s_pages = []
    for j in range(npp):
        rows = [jnp.sum(k_refs[j][h] * qb[h], axis=0, keepdims=True) for h in range(H)]
        s_pages.append(jnp.concatenate(rows, axis=0) + g_all[j * H:(j + 1) * H, :] + decay[j])

    mx = s_pages[0]
    for j in range(1, npp):
        mx = jnp.maximum(mx, s_pages[j])
    m = m_s[...]
    m_new = jnp.maximum(m, jnp.max(mx, axis=1, keepdims=True))
    c = jnp.exp(m - m_new)
    p_pages = [jnp.exp(s - m_new) for s in s_pages]
    psum = p_pages[0]
    for j in range(1, npp):
        psum = psum + p_pages[j]
    l_s[...] = l_s[...] * c + psum
    m_s[...] = m_new
    for h in range(H):
        acc = acc_s[h] * c[h:h + 1, :]
        for j in range(npp):
            acc = acc + p_pages[j][h:h + 1, :] * v_refs[j][h]
        acc_s[h] = acc

    @pl.when(i == n_steps - 1)
    def _():
        s_new = jnp.sum(qh_ref[...] * knh_ref[...], axis=1, keepdims=True)
        m_fin = jnp.maximum(m_new, s_new)
        cf = jnp.exp(m_new - m_fin)
        p_new = jnp.exp(s_new - m_fin)
        denom = jnp.sum(l_s[...], axis=1, keepdims=True) * cf + p_new
        for h in range(H):
            num = (jnp.sum(acc_s[h], axis=1, keepdims=True) * cf[h:h + 1, :]
                   + p_new[h:h + 1, :] * vnt_ref[:, h:h + 1])
            o_ref[:, h:h + 1] = num / denom[h:h + 1, :]


def _fox_sample(page_table, kview, vview, lfview, qt, qh, knh, vnt, lfn):
    nb, n_pages = page_table.shape
    _, H, dh, page = kview.shape
    npp = min(PAGES_PER_STEP, n_pages)
    assert n_pages % npp == 0
    n_steps = n_pages // npp
    u = jnp.asarray(np.tril(np.ones((page, page), np.float32), -1), BF16)

    def page_spec(shape, j):
        nd = len(shape)
        return pl.BlockSpec((None,) + tuple(shape),
                            lambda b, i, pt, j=j: (pt[b, n_pages - npp * (i + 1) + j],) + (0,) * nd)

    seq = lambda shape: pl.BlockSpec((None,) + tuple(shape), lambda b, i, pt: (b,) + (0,) * len(shape))
    in_specs = ([page_spec((H, dh, page), j) for j in range(npp)]
                + [page_spec((H, dh, page), j) for j in range(npp)]
                + [page_spec((H, page), j) for j in range(npp)]
                + [seq((dh, H)), seq((H, dh)), seq((H, dh)), seq((dh, H)), seq((H, 1)),
                   pl.BlockSpec(u.shape, lambda b, i, pt: (0, 0))])
    grid_spec = pltpu.PrefetchScalarGridSpec(
        num_scalar_prefetch=1, grid=(nb, n_steps), in_specs=in_specs,
        out_specs=seq((dh, H)),
        scratch_shapes=[pltpu.VMEM((H, 1), F32), pltpu.VMEM((H, page), F32), pltpu.VMEM((H, dh, page), F32),
                        pltpu.VMEM((H, 1), F32)])
    return pl.pallas_call(
        functools.partial(_fox_sample_kernel, npp),
        grid_spec=grid_spec,
        out_shape=jax.ShapeDtypeStruct((nb, dh, H), F32),
        compiler_params=_cparams(("arbitrary", "arbitrary")),
        name="fox_attention_sample",
    )(page_table, *([kview] * npp), *([vview] * npp), *([lfview] * npp), qt, qh, knh, vnt, lfn, u)


def _top2(logits, axis, n_experts):
    idx = lax.broadcasted_iota(I32, logits.shape, axis)
    big = logits.shape[axis]
    lg = jnp.where(idx < n_experts, logits, -jnp.inf)
    v1 = jnp.max(lg, axis=axis, keepdims=True)
    i1 = jnp.min(jnp.where(lg == v1, idx, big), axis=axis, keepdims=True)
    lg2 = jnp.where(idx == i1, -jnp.inf, lg)
    v2 = jnp.max(lg2, axis=axis, keepdims=True)
    i2 = jnp.min(jnp.where(lg2 == v2, idx, big), axis=axis, keepdims=True)
    return idx, i1, i2, v1, v2


def _odd_out_kernel(alpha, n_experts, n_prompt_tiles, op_ref, ydp_ref, xp_ref, os_ref, yds_ref, xs_ref,
                    wout_ref, g_ref, b_ref, rw_ref, x3p_ref, x3s_ref, x3b_ref, lg_ref, cnt_ref):
    i = pl.program_id(0)
    cwid = op_ref.shape[1]

    def rows(o_ref, yd_ref, x_ref, x3_ref):
        n = o_ref.shape[0]
        mix = _dot(o_ref[...].astype(BF16), wout_ref[0:cwid, :]) + _dot(yd_ref[...], wout_ref[cwid:, :])
        x3 = _ln(alpha * x_ref[...] + mix, g_ref[...], b_ref[...])
        x3b = x3.astype(BF16)
        logits = _dot(x3b, rw_ref[...])
        x3_ref[...] = x3
        if n < x3b_ref.shape[0]:
            x3b_ref[...] = jnp.zeros(x3b_ref.shape, BF16)
            lg_ref[...] = jnp.zeros(lg_ref.shape, F32)
        x3b_ref[0:n, :] = x3b
        lg_ref[0:n, :] = logits
        idx, i1, i2, _, _ = _top2(logits, 1, n_experts)
        mask = jnp.where((idx == i1) | (idx == i2), 1.0, 0.0)
        cnt_ref[...] = jnp.sum(mask, axis=0, keepdims=True)

    @pl.when(i < n_prompt_tiles)
    def _():
        rows(op_ref, ydp_ref, xp_ref, x3p_ref)

    @pl.when(i >= n_prompt_tiles)
    def _():
        rows(os_ref, yds_ref, xs_ref, x3s_ref)


def _odd_out(alpha, n_experts, o_p, yd_p, x_p, o_s, yd_s, x_s, wout, g, b, rw):
    n, d = x_p.shape
    ns = x_s.shape[0]
    tm = TOKEN_TILE
    npt = n // tm
    ntiles = npt + 1
    hw = V7X_LANES
    prow = lambda w: pl.BlockSpec((tm, w), lambda i: (jnp.minimum(i, npt - 1), 0))
    return pl.pallas_call(
        functools.partial(_odd_out_kernel, alpha, n_experts, npt),
        grid=(ntiles,),
        in_specs=[prow(o_p.shape[1]), prow(yd_p.shape[1]), prow(d),
                  _const_spec(o_s.shape), _const_spec(yd_s.shape), _const_spec(x_s.shape)]
        + [_const_spec(c.shape) for c in (wout, g, b, rw)],
        out_specs=[prow(d), _const_spec((ns, d)),
                   pl.BlockSpec((tm, d), lambda i: (i, 0)),
                   pl.BlockSpec((tm, hw), lambda i: (i, 0)),
                   pl.BlockSpec((None, 1, hw), lambda i: (i, 0, 0))],
        out_shape=[jax.ShapeDtypeStruct((n, d), F32), jax.ShapeDtypeStruct((ns, d), F32),
                   jax.ShapeDtypeStruct((ntiles * tm, d), BF16),
                   jax.ShapeDtypeStruct((ntiles * tm, hw), F32),
                   jax.ShapeDtypeStruct((ntiles, 1, hw), F32)],
        compiler_params=_cparams(("arbitrary",)),
        name="odd_out_router",
    )(o_p, yd_p, x_p, o_s, yd_s, x_s, wout, g, b, rw)


def _segment_copies(tile, n_experts, npc_ref, loff_ref, goff_ref, make_copy):
    ratio = SEG_BIG // SEG_PAD
    for e in range(n_experts):
        n = npc_ref[tile * n_experts + e]
        lo = loff_ref[tile * n_experts + e]
        go = goff_ref[tile * n_experts + e]
        nbig = n // ratio

        def big(j, _, lo=lo, go=go):
            make_copy(pl.multiple_of(lo + j * SEG_BIG, SEG_PAD), pl.multiple_of(go + j * SEG_BIG, SEG_PAD), SEG_BIG)
            return 0

        def small(j, _, lo=lo, go=go, nbig=nbig):
            off = nbig * SEG_BIG + j * SEG_PAD
            make_copy(pl.multiple_of(lo + off, SEG_PAD), pl.multiple_of(go + off, SEG_PAD), SEG_PAD)
            return 0

        lax.fori_loop(0, nbig, big, 0)
        lax.fori_loop(0, n - nbig * ratio, small, 0)


def _dispatch_kernel(n_experts, n_valid, npc_ref, loff_ref, goff_ref, x_ref, lt_ref, lofft_ref, triu_ref,
                     xs_in_ref, xs_ref, stage, sem):
    del xs_in_ref
    tile = pl.program_id(0)
    tt = x_ref.shape[0]
    rl = stage.shape[0]
    sub, i1, i2, _, _ = _top2(lt_ref[...], 0, n_experts)
    valid = (tile * tt + lax.broadcasted_iota(I32, (1, tt), 1)) < n_valid
    sel1 = (sub == i1) & valid
    sel2 = (sub == i2) & valid
    mask = jnp.where(sel1 | sel2, 1.0, 0.0)
    rank = _dot(mask.astype(BF16), triu_ref[...])
    loc = rank + jnp.concatenate([lofft_ref[...]] * (tt // V7X_LANES), axis=1)
    lr1 = jnp.sum(jnp.where(sel1, loc, 0.0), axis=0, keepdims=True).astype(I32)
    lr2 = jnp.sum(jnp.where(sel2, loc, 0.0), axis=0, keepdims=True).astype(I32)
    lr1 = jnp.where(valid, lr1, -1)
    lr2 = jnp.where(valid, lr2, -1)
    r = lax.broadcasted_iota(I32, (rl, tt), 0)
    onehot = jnp.where((r == lr1) | (r == lr2), 1.0, 0.0).astype(BF16)
    stage[...] = _dot(onehot, x_ref[...]).astype(BF16)

    def copy(lo, go, nrows):
        return pltpu.make_async_copy(stage.at[pl.ds(lo, nrows), :], xs_ref.at[pl.ds(go, nrows), :], sem)

    _segment_copies(tile, n_experts, npc_ref, loff_ref, goff_ref, lambda lo, go, n: copy(lo, go, n).start())
    _segment_copies(tile, n_experts, npc_ref, loff_ref, goff_ref, lambda lo, go, n: copy(lo, go, n).wait())


def _dispatch(n_experts, n_valid, npc, loff, goff, x3b, lt, lofft, xs_zero):
    ntot, d = x3b.shape
    tt = TOKEN_TILE
    triu = jnp.asarray(np.triu(np.ones((tt, tt), np.float32), 1), BF16)
    grid_spec = pltpu.PrefetchScalarGridSpec(
        num_scalar_prefetch=3, grid=(ntot // tt,),
        in_specs=[pl.BlockSpec((tt, d), lambda i, *_: (i, 0)),
                  pl.BlockSpec((n_experts, tt), lambda i, *_: (0, i)),
                  pl.BlockSpec((None, n_experts, V7X_LANES), lambda i, *_: (i, 0, 0)),
                  pl.BlockSpec(triu.shape, lambda i, *_: (0, 0)),
                  pl.BlockSpec(memory_space=pl.ANY)],
        out_specs=pl.BlockSpec(memory_space=pl.ANY),
        scratch_shapes=[pltpu.VMEM((LOCAL_ROWS, d), BF16), pltpu.SemaphoreType.DMA(())])
    return pl.pallas_call(
        functools.partial(_dispatch_kernel, n_experts, n_valid),
        grid_spec=grid_spec,
        out_shape=jax.ShapeDtypeStruct(xs_zero.shape, BF16),
        input_output_aliases={7: 0},
        compiler_params=_cparams(("arbitrary",)),
        name="moe_dispatch",
    )(npc, loff, goff, x3b, lt, lofft, triu, xs_zero)


def _experts_kernel(te_ref, tv_ref, xr_ref, x_ref, w1_ref, w3_ref, w2_ref, yh_ref, yl_ref, h_ref):
    del te_ref, xr_ref
    i = pl.program_id(0)
    c = pl.program_id(1)
    last = pl.num_programs(1) - 1
    fc = w1_ref.shape[1]
    valid = tv_ref[i] > 0

    @pl.when(valid)
    def _():
        x = x_ref[...]
        h = (_silu(_dot(x, w1_ref[...])) * _dot(x, w3_ref[...])).astype(BF16)
        h_ref[:, pl.ds(pl.multiple_of(c * fc, fc), fc)] = h

        @pl.when(c == last)
        def _():
            y = _dot(h_ref[...], w2_ref[...])
            hi = y.astype(BF16)
            yh_ref[...] = hi
            yl_ref[...] = (y - hi.astype(F32)).astype(BF16)

    @pl.when(jnp.logical_not(valid) & (c == last))
    def _():
        yh_ref[...] = jnp.zeros(yh_ref.shape, BF16)
        yl_ref[...] = jnp.zeros(yl_ref.shape, BF16)


def _experts(te, tv, xr, xs, w1, w3, w2):
    rmax, d = xs.shape
    tm = EXPERT_TILE
    ff = w1.shape[-1]
    fc = EXPERT_FF_CHUNK
    nch = ff // fc
    assert ff % fc == 0 and rmax % tm == 0

    def chunk(i, c, tv):
        return jnp.where(tv[i] > 0, c, nch - 1)

    grid_spec = pltpu.PrefetchScalarGridSpec(
        num_scalar_prefetch=3, grid=(rmax // tm, nch),
        in_specs=[pl.BlockSpec((tm, d), lambda i, c, te, tv, xr: (xr[i], 0)),
                  pl.BlockSpec((None, d, fc), lambda i, c, te, tv, xr: (te[i], 0, chunk(i, c, tv))),
                  pl.BlockSpec((None, d, fc), lambda i, c, te, tv, xr: (te[i], 0, chunk(i, c, tv))),
                  pl.BlockSpec((None, ff, d), lambda i, c, te, tv, xr: (te[i], 0, 0))],
        out_specs=[pl.BlockSpec((tm, d), lambda i, c, te, tv, xr: (i, 0)),
                   pl.BlockSpec((tm, d), lambda i, c, te, tv, xr: (i, 0))],
        scratch_shapes=[pltpu.VMEM((tm, ff), BF16)])
    return pl.pallas_call(
        _experts_kernel,
        grid_spec=grid_spec,
        out_shape=[jax.ShapeDtypeStruct((rmax, d), BF16), jax.ShapeDtypeStruct((rmax, d), BF16)],
        compiler_params=_cparams(("arbitrary", "arbitrary")),
        name="moe_experts",
    )(te, tv, xr, xs, w1, w3, w2)


def _combine_kernel(alpha, n_experts, n_valid, n_prompt_tiles, npc_ref, loff_ref, goff_ref, lg_ref, xp_ref,
                    xs_ref, loffr_ref, tril_ref, g_ref, b_ref, yh_hbm, yl_hbm, yp_ref, ysm_ref,
                    sth, stl, sem):
    tile = pl.program_id(0)
    tt = lg_ref.shape[0]
    rl = sth.shape[0]
    d = sth.shape[1]

    def copies(fn):
        _segment_copies(tile, n_experts, npc_ref, loff_ref, goff_ref,
                        lambda lo, go, n: fn(pltpu.make_async_copy(yh_hbm.at[pl.ds(go, n), :],
                                                                  sth.at[pl.ds(lo, n), :], sem.at[0])))
        _segment_copies(tile, n_experts, npc_ref, loff_ref, goff_ref,
                        lambda lo, go, n: fn(pltpu.make_async_copy(yl_hbm.at[pl.ds(go, n), :],
                                                                  stl.at[pl.ds(lo, n), :], sem.at[1])))

    copies(lambda cp: cp.start())

    last = tile * n_experts + n_experts - 1
    used = loff_ref[last] // SEG_PAD + npc_ref[last]

    def clear(j, _):
        o = pl.multiple_of(j * SEG_PAD, SEG_PAD)
        sth[pl.ds(o, SEG_PAD), :] = jnp.zeros((SEG_PAD, d), BF16)
        stl[pl.ds(o, SEG_PAD), :] = jnp.zeros((SEG_PAD, d), BF16)
        return 0

    lax.fori_loop(used, rl // SEG_PAD, clear, 0)

    lane, i1, i2, v1, v2 = _top2(lg_ref[...], 1, n_experts)
    valid = (tile * tt + lax.broadcasted_iota(I32, (tt, 1), 0)) < n_valid
    sel1 = (lane == i1) & valid
    sel2 = (lane == i2) & valid
    mask = jnp.where(sel1 | sel2, 1.0, 0.0)
    rank = _dot(tril_ref[...], mask.astype(BF16))
    loc = rank + jnp.concatenate([loffr_ref[...]] * (tt // V7X_SUBLANES), axis=0)
    lr1 = jnp.sum(jnp.where(sel1, loc, 0.0), axis=1, keepdims=True).astype(I32)
    lr2 = jnp.sum(jnp.where(sel2, loc, 0.0), axis=1, keepdims=True).astype(I32)
    lr1 = jnp.where(valid, lr1, -1)
    lr2 = jnp.where(valid, lr2, -1)
    e21 = jnp.exp(v2 - v1)
    g1 = 1.0 / (1.0 + e21)
    g2 = e21 / (1.0 + e21)
    r = lax.broadcasted_iota(I32, (tt, rl), 1)
    gm = jnp.where(r == lr1, g1, 0.0) + jnp.where(r == lr2, g2, 0.0)
    gm_hi = gm.astype(BF16)
    gm_lo = (gm - gm_hi.astype(F32)).astype(BF16)

    copies(lambda cp: cp.wait())
    yh = sth[...]
    moe = _dot(gm_hi, yh) + _dot(gm_hi, stl[...]) + _dot(gm_lo, yh)

    @pl.when(tile < n_prompt_tiles)
    def _():
        yp_ref[...] = _ln(alpha * xp_ref[...] + moe, g_ref[...], b_ref[...])

    @pl.when(tile >= n_prompt_tiles)
    def _():
        ns = ysm_ref.shape[0]
        ysm_ref[...] = _ln(alpha * xs_ref[...] + moe[0:ns, :], g_ref[...], b_ref[...])


def _combine(alpha, n_experts, n_valid, npc, loff, goff, logits, x3p, x3s, loffr, g, b, yh, yl):
    npr, d = x3p.shape
    ns = x3s.shape[0]
    tt = TOKEN_TILE
    npt = npr // tt
    ntiles = logits.shape[0] // tt
    tril = jnp.asarray(np.tril(np.ones((tt, tt), np.float32), -1), BF16)
    grid_spec = pltpu.PrefetchScalarGridSpec(
        num_scalar_prefetch=3, grid=(ntiles,),
        in_specs=[pl.BlockSpec((tt, V7X_LANES), lambda i, *_: (i, 0)),
                  pl.BlockSpec((tt, d), lambda i, *_: (jnp.minimum(i, npt - 1), 0)),
                  pl.BlockSpec((ns, d), lambda i, *_: (0, 0)),
                  pl.BlockSpec((None, V7X_SUBLANES, V7X_LANES), lambda i, *_: (i, 0, 0)),
                  pl.BlockSpec(tril.shape, lambda i, *_: (0, 0)),
                  pl.BlockSpec(g.shape, lambda i, *_: (0, 0)),
                  pl.BlockSpec(b.shape, lambda i, *_: (0, 0)),
                  pl.BlockSpec(memory_space=pl.ANY),
                  pl.BlockSpec(memory_space=pl.ANY)],
        out_specs=[pl.BlockSpec((tt, d), lambda i, *_: (jnp.minimum(i, npt - 1), 0)),
                   pl.BlockSpec((ns, d), lambda i, *_: (0, 0))],
        scratch_shapes=[pltpu.VMEM((LOCAL_ROWS, d), BF16), pltpu.VMEM((LOCAL_ROWS, d), BF16),
                        pltpu.SemaphoreType.DMA((2,))])
    return pl.pallas_call(
        functools.partial(_combine_kernel, alpha, n_experts, n_valid, npt),
        grid_spec=grid_spec,
        out_shape=[jax.ShapeDtypeStruct((npr, d), F32), jax.ShapeDtypeStruct((ns, d), F32)],
        compiler_params=_cparams(("arbitrary",)),
        name="moe_combine",
    )(npc, loff, goff, logits, x3p, x3s, loffr, tril, g, b, yh, yl)


def _routing_plan(cnt, n_experts):
    ntiles = cnt.shape[0]
    pc = (cnt + SEG_PAD - 1) // SEG_PAD * SEG_PAD
    loff = jnp.cumsum(pc, axis=1) - pc
    per_expert = jnp.sum(pc, axis=0)
    gp = (per_expert + EXPERT_TILE - 1) // EXPERT_TILE * EXPERT_TILE
    gend = jnp.cumsum(gp)
    goff = (gend - gp)[None, :] + jnp.cumsum(pc, axis=0) - pc
    rmax = TOP_K * ntiles * TOKEN_TILE + ntiles * n_experts * (SEG_PAD - 1) + n_experts * (EXPERT_TILE - 1)
    nt_max = -(-rmax // EXPERT_TILE)
    tiles_used = gend[-1] // EXPERT_TILE
    ti = jnp.arange(nt_max, dtype=I32)
    tv = (ti < tiles_used).astype(I32)
    xr = jnp.maximum(jnp.minimum(ti, tiles_used - 1), 0)
    te = jnp.sum((xr[:, None] >= (gend // EXPERT_TILE)[None, :]).astype(I32), axis=1)
    te = jnp.minimum(te, n_experts - 1)
    flat = lambda a: a.reshape(-1).astype(I32)
    return (flat(pc // SEG_PAD), flat(loff), flat(goff), loff.astype(F32), te, tv, xr.astype(I32),
            nt_max * EXPERT_TILE)


def kernel(x_prompt, x_sample, state_pool, state_conv_b, cache_k, cache_v, cache_logf, state_conv_d, page_table,
           w_in_even, pool_w, pool_scale, conv_b_w, conv_b_bias, conv_ln_g, conv_ln_b, w_out_even, ln_mix_even_g,
           ln_mix_even_b, ffn_w1, ffn_w3, ffn_w2, ln_ffn_even_g, ln_ffn_even_b, w_in_odd, forget_bias, conv_d_w,
           w_out_odd, ln_mix_odd_g, ln_mix_odd_b, router_w, moe_w1, moe_w3, moe_w2, ln_ffn_odd_g, ln_ffn_odd_b):
    assert w_in_even.shape[0] == 1 and w_in_odd.shape[0] == 1, "one even and one odd layer are supported"
    depth = w_in_even.shape[0] + w_in_odd.shape[0]
    alpha = float((2 * depth) ** 0.25)
    B, T, D = x_prompt.shape
    nb = x_sample.shape[0]
    assert x_sample.shape[1] == 1 and T % TOKEN_TILE == 0 and nb <= TOKEN_TILE
    n_heads = forget_bias.shape[-1]
    cwid = cache_k.shape[-1] * cache_k.shape[-2]
    head_dim = cache_k.shape[-1]
    dwid = conv_d_w.shape[-1]
    n_experts = router_w.shape[-1]
    past_len = page_table.shape[1] * cache_k.shape[2]
    hw = V7X_LANES
    bf = lambda w: w.astype(BF16)
    rowv = lambda w: w.reshape(1, -1)

    ew = (bf(w_in_even[0]), bf(pool_w[0]), pool_scale, conv_b_w[0], conv_b_bias, conv_ln_g, conv_ln_b,
          bf(w_out_even[0]), ln_mix_even_g, ln_mix_even_b)
    xp, a_hist, u_hist = _even_prompt(alpha, x_prompt, *ew)
    xs, pool_s_t, convb_s_t = _even_sample(
        alpha, past_len, x_sample.reshape(nb, D), jnp.swapaxes(state_pool[0], 0, 1),
        jnp.swapaxes(state_conv_b[0], 0, 1), *ew)
    nph = state_pool.shape[2]
    nch = state_conv_b.shape[2]
    pool_p = a_hist[None, :, POOL_HIST_ROWS - nph:, :]
    convb_p = u_hist[None, :, CONV_B_HIST_ROWS - nch:, :]
    pool_s = jnp.swapaxes(pool_s_t, 0, 1)[None]
    convb_s = jnp.swapaxes(convb_s_t, 0, 1)[None]

    fw = (bf(ffn_w1[0]), bf(ffn_w3[0]), bf(ffn_w2[0]), ln_ffn_even_g, ln_ffn_even_b)
    xp = _ffn(alpha, xp.reshape(B * T, D), *fw)
    xs = _ffn(alpha, xs, *fw)

    w_in = w_in_odd[0]
    wqkv = bf(w_in[:, :3 * cwid])
    wf = bf(jnp.pad(w_in[:, 3 * cwid:3 * cwid + n_heads], ((0, 0), (0, hw - n_heads))))
    whbc = bf(w_in[:, 3 * cwid + n_heads:])
    fb = jnp.pad(forget_bias, ((0, 0), (0, hw - n_heads)))
    k_p, v_p, lf_p, qa, ka, va, yd_p, cd_p = _odd_prompt(
        n_heads, head_dim, xp.reshape(B, T, D), wqkv, wf, whbc, fb, conv_d_w[0])
    o_p = _fox_prompt(head_dim, qa, ka, va)

    q_s, k_s, v_s, lf_s, yd_s, convd_s_t = _odd_sample(
        n_heads, head_dim, xs, jnp.swapaxes(state_conv_d[0], 0, 1), wqkv, wf, whbc, fb, conv_d_w[0])
    heads = lambda z: z.reshape(nb, n_heads, head_dim)
    heads_t = lambda z: jnp.swapaxes(heads(z), 1, 2)
    q_sc = q_s * head_dim ** -0.5
    lf_s = lf_s[:, :n_heads]
    o_s_t = _fox_sample(page_table, jnp.transpose(cache_k[0], (0, 2, 3, 1)), jnp.transpose(cache_v[0], (0, 2, 3, 1)),
                        jnp.transpose(cache_logf[0], (0, 2, 1)), heads_t(q_sc), heads(q_sc), heads(k_s),
                        heads_t(v_s), lf_s.reshape(nb, n_heads, 1))
    o_s = jnp.swapaxes(o_s_t, 1, 2).reshape(nb, cwid)

    n_prompt = B * T
    npt = n_prompt // TOKEN_TILE
    ntiles = npt + 1
    n_valid = n_prompt + nb
    ow = (bf(w_out_odd[0]), ln_mix_odd_g, ln_mix_odd_b,
          bf(jnp.pad(router_w[0], ((0, 0), (0, hw - n_experts)))))
    x3p, x3s, x3b, logits, cnt = _odd_out(alpha, n_experts, o_p.reshape(n_prompt, cwid),
                                          yd_p.reshape(n_prompt, dwid), xp, o_s, yd_s, xs, *ow)

    cnt = cnt[:, 0, :n_experts].astype(I32)
    npc, loff, goff, loff_f, te, tv, xr, rmax = _routing_plan(cnt, n_experts)
    lt = jnp.transpose(logits[:, :n_experts])
    lofft = jnp.broadcast_to(loff_f[:, :, None], (ntiles, n_experts, hw))
    loffr = jnp.broadcast_to(jnp.pad(loff_f, ((0, 0), (0, hw - n_experts)))[:, None, :],
                             (ntiles, V7X_SUBLANES, hw))
    xs_sorted = _dispatch(n_experts, n_valid, npc, loff, goff, x3b, lt, lofft, jnp.zeros((rmax, D), BF16))
    yh, yl = _experts(te, tv, xr, xs_sorted, bf(moe_w1[0]), bf(moe_w3[0]), bf(moe_w2[0]))
    y_p, y_s = _combine(alpha, n_experts, n_valid, npc, loff, goff, logits, x3p, x3s, loffr,
                        ln_ffn_odd_g, ln_ffn_odd_b, yh, yl)

    nd = state_conv_d.shape[2]
    return (y_p.reshape(B, T, D), y_s.reshape(nb, 1, D),
            pool_p, pool_s, convb_p, convb_s,
            k_p.reshape(1, B, T, n_heads, head_dim), k_s.reshape(1, nb, 1, n_heads, head_dim),
            v_p.reshape(1, B, T, n_heads, head_dim), v_s.reshape(1, nb, 1, n_heads, head_dim),
            lf_p.reshape(1, B, T, n_heads), lf_s.reshape(1, nb, 1, n_heads),
            cd_p[None, :, CONV_D_HIST_ROWS - nd:, :], jnp.swapaxes(convd_s_t, 0, 1)[None])
```

```python
import functools

import numpy as np
import jax
import jax.numpy as jnp
from jax import lax
from jax.experimental import pallas as pl
from jax.experimental.pallas import tpu as pltpu

F32 = jnp.float32
BF16 = jnp.bfloat16
I32 = jnp.int32

LN_EPS = 1e-5
POOL_WINDOWS = (2, 4, 8, 16)
TOP_K = 2
NEG = -1e30
LOG2E = 1.4426950408889634

V7X_VMEM_BYTES = 64 * 1024 * 1024
V7X_LANES = 128
V7X_SUBLANES = 8
BF16_ROWS_PER_TILE = 2 * V7X_SUBLANES

VMEM_LIMIT = V7X_VMEM_BYTES - 8 * 1024 * 1024

TOKEN_TILE = 512
EXPERT_TILE = 512
EXPERT_FF_CHUNK = 1792
FFN_CHUNK = 256
CONV_ROWS = 64
POOL_HIST_ROWS = 16
CONV_B_HIST_ROWS = 32
CONV_D_HIST_ROWS = 8
PAGES_PER_STEP = 16
SEG_PAD = BF16_ROWS_PER_TILE
SEG_BIG = 128
LOCAL_ROWS = TOP_K * TOKEN_TILE + 8 * SEG_PAD


def _cparams(sem):
    return pltpu.CompilerParams(dimension_semantics=sem, vmem_limit_bytes=VMEM_LIMIT)


def _dot(a, b):
    return jnp.dot(a, b, preferred_element_type=F32)


def _dot_nt(a, b):
    return lax.dot_general(a, b, (((1,), (1,)), ((), ())), preferred_element_type=F32)


def _dot_tn(a, b):
    return lax.dot_general(a, b, (((0,), (0,)), ((), ())), preferred_element_type=F32)


def _ln(z, g, b):
    mu = jnp.mean(z, axis=-1, keepdims=True)
    d = z - mu
    var = jnp.mean(d * d, axis=-1, keepdims=True)
    return d * lax.rsqrt(var + LN_EPS) * g + b


def _silu(x):
    return x * jax.nn.sigmoid(x)


def _log_sigmoid(z):
    return jnp.minimum(z, 0.0) - jnp.log1p(jnp.exp(-jnp.abs(z)))


def _split3(x):
    hi = x.astype(BF16)
    r = x - hi.astype(F32)
    mid = r.astype(BF16)
    lo = (r - mid.astype(F32)).astype(BF16)
    return hi, mid, lo


def _const_spec(shape):
    nd = len(shape)
    return pl.BlockSpec(shape, lambda *_: (0,) * nd)


def _resident_spec(shape):
    nd = len(shape)
    return pl.BlockSpec(shape, lambda *_: (0,) * nd, pipeline_mode=pl.Buffered(1))


def _pool_groups(a, hist_fn, pos, pw_ref):
    outs = []
    gw = a.shape[1] // len(POOL_WINDOWS)
    for g, w in enumerate(POOL_WINDOWS):
        c0 = g * gw
        cur = a[:, c0:c0 + gw]
        win = cur
        for k in range(1, w):
            win = win + hist_fn(k, c0, gw)
        cnt = jnp.minimum(pos + 1, w).astype(F32)
        pooled = win / cnt - cur
        outs.append(_dot(pooled.astype(BF16), pw_ref[g]))
    return jnp.concatenate(outs, axis=-1)


def _even_prompt_kernel(alpha, x_ref, win_ref, pw_ref, ps_ref, cw_ref, cb_ref, cg_ref, cbeta_ref,
                        wout_ref, g_ref, b_ref, y_ref, ah_ref, uh_ref, aext, uext, ush, ybuf):
    t = pl.program_id(1)
    tT = x_ref.shape[0]
    aw = ps_ref.shape[-1]
    bw = cw_ref.shape[-1]
    taps = cw_ref.shape[0]
    AH, UH = POOL_HIST_ROWS, CONV_B_HIST_ROWS

    @pl.when(t == 0)
    def _():
        aext[0:AH, :] = jnp.zeros((AH, aw), F32)
        uext[0:UH, :] = jnp.zeros((UH, bw), F32)

    @pl.when(t > 0)
    def _():
        aext[0:AH, :] = aext[tT:tT + AH, :]
        uext[0:UH, :] = uext[tT:tT + UH, :]

    x = x_ref[...]
    proj = _dot(x.astype(BF16), win_ref[...])
    a = proj[:, :aw]
    u = proj[:, aw:aw + bw] * jax.nn.sigmoid(proj[:, aw + bw:])
    aext[AH:, :] = a
    uext[UH:, :] = u

    pos = t * tT + lax.broadcasted_iota(I32, (tT, 1), 0)
    ya = _pool_groups(a, lambda k, c0, gw: aext[AH - k:AH - k + tT, c0:c0 + gw], pos, pw_ref)
    ya = ya * ps_ref[...]

    base = UH - (taps - 1)
    ns = V7X_SUBLANES
    span = UH + tT - ns
    for s in range(1, ns):
        ush[s - 1, 0:span, :] = uext[s:s + span, :]

    def tap(k, r0):
        off = base + k
        a, s = off - off % ns + r0, off % ns
        rows = uext[a:a + CONV_ROWS, :] if s == 0 else ush[s - 1, a:a + CONV_ROWS, :]
        return rows * cw_ref[k:k + 1, :]

    for r0 in range(0, tT, CONV_ROWS):
        acc = tap(0, r0)
        for k in range(1, taps):
            acc = acc + tap(k, r0)
        yb = _silu(_ln(acc + cb_ref[...], cg_ref[...], cbeta_ref[...]))
        ybuf[r0:r0 + CONV_ROWS, :] = yb.astype(BF16)

    mix = _dot(ya.astype(BF16), wout_ref[0:aw, :]) + _dot(ybuf[...], wout_ref[aw:, :])
    y_ref[...] = _ln(alpha * x + mix, g_ref[...], b_ref[...])
    ah_ref[...] = aext[tT:tT + AH, :]
    uh_ref[...] = uext[tT:tT + UH, :]


def _even_prompt(alpha, x, win, pw, ps, cw, cb, cg, cbeta, wout, g, b):
    B, T, D = x.shape
    tT = min(TOKEN_TILE, T)
    aw, bw = ps.shape[-1], cw.shape[-1]
    AH, UH = POOL_HIST_ROWS, CONV_B_HIST_ROWS
    consts = (win, pw, ps, cw, cb, cg, cbeta, wout, g, b)
    return pl.pallas_call(
        functools.partial(_even_prompt_kernel, alpha),
        grid=(B, T // tT),
        in_specs=[pl.BlockSpec((None, tT, D), lambda bi, ti: (bi, ti, 0))]
        + [_const_spec(c.shape) for c in consts],
        out_specs=[pl.BlockSpec((None, tT, D), lambda bi, ti: (bi, ti, 0)),
                   pl.BlockSpec((None, AH, aw), lambda bi, ti: (bi, 0, 0)),
                   pl.BlockSpec((None, UH, bw), lambda bi, ti: (bi, 0, 0))],
        out_shape=[jax.ShapeDtypeStruct((B, T, D), F32),
                   jax.ShapeDtypeStruct((B, AH, aw), F32),
                   jax.ShapeDtypeStruct((B, UH, bw), F32)],
        scratch_shapes=[pltpu.VMEM((AH + tT, aw), F32), pltpu.VMEM((UH + tT, bw), F32),
                        pltpu.VMEM((V7X_SUBLANES - 1, UH + tT, bw), F32), pltpu.VMEM((tT, bw), BF16)],
        compiler_params=_cparams(("arbitrary", "arbitrary")),
        name="even_mixer_prompt",
    )(x, *consts)


def _even_sample_kernel(alpha, first_pos, x_ref, sp_ref, sc_ref, win_ref, pw_ref, ps_ref, cw_ref, cb_ref,
                        cg_ref, cbeta_ref, wout_ref, g_ref, b_ref, y_ref, spo_ref, sco_ref):
    aw = ps_ref.shape[-1]
    bw = cw_ref.shape[-1]
    taps = cw_ref.shape[0]
    nph = sp_ref.shape[0]
    nch = sc_ref.shape[0]
    x = x_ref[...]
    proj = _dot(x.astype(BF16), win_ref[...])
    a = proj[:, :aw]
    u = proj[:, aw:aw + bw] * jax.nn.sigmoid(proj[:, aw + bw:])

    pos = jnp.full((x.shape[0], 1), first_pos, I32)
    ya = _pool_groups(a, lambda k, c0, gw: sp_ref[nph - k, :, c0:c0 + gw], pos, pw_ref)
    ya = ya * ps_ref[...]

    acc = u * cw_ref[taps - 1:taps, :]
    for k in range(taps - 1):
        acc = acc + sc_ref[k + nch - (taps - 1)] * cw_ref[k:k + 1, :]
    yb = _silu(_ln(acc + cb_ref[...], cg_ref[...], cbeta_ref[...]))

    mix = _dot(ya.astype(BF16), wout_ref[0:aw, :]) + _dot(yb.astype(BF16), wout_ref[aw:, :])
    y_ref[...] = _ln(alpha * x + mix, g_ref[...], b_ref[...])
    for j in range(nph - 1):
        spo_ref[j] = sp_ref[j + 1]
    spo_ref[nph - 1] = a
    for j in range(nch - 1):
        sco_ref[j] = sc_ref[j + 1]
    sco_ref[nch - 1] = u


def _even_sample(alpha, first_pos, x, sp_t, sc_t, win, pw, ps, cw, cb, cg, cbeta, wout, g, b):
    ins = (x, sp_t, sc_t, win, pw, ps, cw, cb, cg, cbeta, wout, g, b)
    return pl.pallas_call(
        functools.partial(_even_sample_kernel, alpha, first_pos),
        grid=(1,),
        in_specs=[_const_spec(c.shape) for c in ins],
        out_specs=[_const_spec(x.shape), _const_spec(sp_t.shape), _const_spec(sc_t.shape)],
        out_shape=[jax.ShapeDtypeStruct(x.shape, F32), jax.ShapeDtypeStruct(sp_t.shape, F32),
                   jax.ShapeDtypeStruct(sc_t.shape, F32)],
        compiler_params=_cparams(("arbitrary",)),
        name="even_mixer_sample",
    )(*ins)


def _ffn_kernel(alpha, x_ref, w1_ref, w3_ref, w2_ref, g_ref, b_ref, y_ref, h_ref):
    x = x_ref[...]
    xb = x.astype(BF16)
    ff = w1_ref.shape[1]
    for c in range(0, ff, FFN_CHUNK):
        h1 = _dot(xb, w1_ref[:, c:c + FFN_CHUNK])
        h3 = _dot(xb, w3_ref[:, c:c + FFN_CHUNK])
        h_ref[:, c:c + FFN_CHUNK] = (_silu(h1) * h3).astype(BF16)
    y = _dot(h_ref[...], w2_ref[...])
    y_ref[...] = _ln(alpha * x + y, g_ref[...], b_ref[...])


def _ffn(alpha, x, w1, w3, w2, g, b):
    n, d = x.shape
    tm = min(TOKEN_TILE, n)
    ff = w1.shape[1]
    assert ff % FFN_CHUNK == 0 and n % tm == 0
    return pl.pallas_call(
        functools.partial(_ffn_kernel, alpha),
        grid=(n // tm,),
        in_specs=[pl.BlockSpec((tm, d), lambda i: (i, 0)),
                  _resident_spec(w1.shape), _resident_spec(w3.shape), _resident_spec(w2.shape),
                  _const_spec(g.shape), _const_spec(b.shape)],
        out_specs=pl.BlockSpec((tm, d), lambda i: (i, 0)),
        out_shape=jax.ShapeDtypeStruct((n, d), F32),
        scratch_shapes=[pltpu.VMEM((tm, ff), BF16)],
        compiler_params=_cparams(("arbitrary",)),
        name="ffn_swiglu",
    )(x, w1, w3, w2, g, b)


def _head_select_mats(n_heads, head_dim):
    hw = V7X_LANES
    one_col = 3 * n_heads
    sq = np.zeros((hw, n_heads * hw), np.float32)
    sk = np.zeros((hw, n_heads * hw), np.float32)
    for h in range(n_heads):
        o = h * hw + head_dim
        for part in range(3):
            sq[part * n_heads + h, o + part] = 1.0
            sq[one_col, o + 3 + part] = 1.0
            sk[one_col, o + part] = 1.0
            sk[part * n_heads + h, o + 3 + part] = -1.0
    return jnp.asarray(sq, BF16), jnp.asarray(sk, BF16)


def _odd_prompt_kernel(n_heads, head_dim, x_ref, wqkv_ref, wf_ref, whbc_ref, fb_ref, cw_ref, tri_ref,
                       sq_ref, sk_ref, k_ref, v_ref, lf_ref, qa_ref, ka_ref, va_ref, yd_ref, cd_ref,
                       gext, fcarry):
    t = pl.program_id(1)
    tT = x_ref.shape[0]
    cwid = n_heads * head_dim
    dwid = cw_ref.shape[-1]
    taps = cw_ref.shape[0]
    GH = CONV_D_HIST_ROWS
    hw = V7X_LANES

    @pl.when(t == 0)
    def _():
        gext[0:GH, :] = jnp.zeros((GH, dwid), F32)
        fcarry[...] = jnp.zeros(fcarry.shape, F32)

    @pl.when(t > 0)
    def _():
        gext[0:GH, :] = gext[tT:tT + GH, :]

    xb = x_ref[...].astype(BF16)
    qkv = _dot(xb, wqkv_ref[...])
    q = qkv[:, :cwid]
    k = qkv[:, cwid:2 * cwid]
    v = qkv[:, 2 * cwid:]
    k_ref[...] = k
    v_ref[...] = v

    lane = lax.broadcasted_iota(I32, (tT, hw), 1)
    fl = _dot(xb, wf_ref[...])
    logf = jnp.where(lane < n_heads, _log_sigmoid(fl + fb_ref[...]), 0.0)
    lf_ref[...] = logf[:, :n_heads]

    tri = tri_ref[...]
    l_hi, l_mid, l_lo = _split3(logf)
    F = _dot(tri, l_hi) + _dot(tri, l_mid) + _dot(tri, l_lo) + fcarry[...]
    fcarry[...] = F[tT - 1:tT, :]

    f_hi, f_mid, f_lo = _split3(F * LOG2E)
    fparts = (f_hi.astype(F32) + pltpu.roll(f_mid.astype(F32), n_heads, axis=1)
              + pltpu.roll(f_lo.astype(F32), 2 * n_heads, axis=1)
              + jnp.where(lane == 3 * n_heads, 1.0, 0.0)).astype(BF16)
    xq = _dot(fparts, sq_ref[...])
    xk = _dot(fparts, sk_ref[...])
    vone = jnp.where(lane == head_dim, 1.0, 0.0)
    scale = head_dim ** -0.5 * LOG2E
    for h in range(n_heads):
        p = (h * head_dim) // hw
        qp = q[:, p * hw:(p + 1) * hw] * scale
        kp = k[:, p * hw:(p + 1) * hw]
        vp = v[:, p * hw:(p + 1) * hw]
        if (h * head_dim) % hw:
            sh = hw - (h * head_dim) % hw
            qp = pltpu.roll(qp, sh, axis=1)
            kp = pltpu.roll(kp, sh, axis=1)
            vp = pltpu.roll(vp, sh, axis=1)
        qa_ref[h] = jnp.where(lane < head_dim, qp, xq[:, h * hw:(h + 1) * hw]).astype(BF16)
        ka_ref[h] = jnp.where(lane < head_dim, kp, xk[:, h * hw:(h + 1) * hw]).astype(BF16)
        va_ref[h] = jnp.where(lane < head_dim, vp, vone).astype(BF16)

    hbc = _dot(xb, whbc_ref[...])
    hh = hbc[:, :dwid]
    bg = hbc[:, dwid:2 * dwid]
    cg = hbc[:, 2 * dwid:]
    g = cg * hh
    gext[GH:, :] = g
    conv = g * cw_ref[taps - 1:taps, :]
    for kk in range(taps - 1):
        back = taps - 1 - kk
        conv = conv + gext[GH - back:GH - back + tT, :] * cw_ref[kk:kk + 1, :]
    yd_ref[...] = (bg * conv).astype(BF16)
    cd_ref[...] = gext[tT:tT + GH, :]


def _odd_prompt(n_heads, head_dim, x, wqkv, wf, whbc, fb, cw):
    B, T, D = x.shape
    tT = min(TOKEN_TILE, T)
    cwid = n_heads * head_dim
    dwid = cw.shape[-1]
    hw = V7X_LANES
    GH = CONV_D_HIST_ROWS
    tri = jnp.tri(tT, dtype=BF16)
    sq, sk = _head_select_mats(n_heads, head_dim)
    consts = (wqkv, wf, whbc, fb, cw, tri, sq, sk)
    tok = lambda w: pl.BlockSpec((None, tT, w), lambda bi, ti: (bi, ti, 0))
    head = pl.BlockSpec((None, n_heads, tT, hw), lambda bi, ti: (bi, 0, ti, 0))
    return pl.pallas_call(
        functools.partial(_odd_prompt_kernel, n_heads, head_dim),
        grid=(B, T // tT),
        in_specs=[tok(D)] + [_const_spec(c.shape) for c in consts],
        out_specs=[tok(cwid), tok(cwid), tok(n_heads), head, head, head, tok(dwid),
                   pl.BlockSpec((None, GH, dwid), lambda bi, ti: (bi, 0, 0))],
        out_shape=[jax.ShapeDtypeStruct((B, T, cwid), F32), jax.ShapeDtypeStruct((B, T, cwid), F32),
                   jax.ShapeDtypeStruct((B, T, n_heads), F32),
                   jax.ShapeDtypeStruct((B, n_heads, T, hw), BF16),
                   jax.ShapeDtypeStruct((B, n_heads, T, hw), BF16),
                   jax.ShapeDtypeStruct((B, n_heads, T, hw), BF16),
                   jax.ShapeDtypeStruct((B, T, dwid), BF16),
                   jax.ShapeDtypeStruct((B, GH, dwid), F32)],
        scratch_shapes=[pltpu.VMEM((GH + tT, dwid), F32), pltpu.VMEM((1, hw), F32)],
        compiler_params=_cparams(("arbitrary", "arbitrary")),
        name="odd_proj_prompt",
    )(x, *consts)


def _odd_sample_kernel(n_heads, head_dim, x_ref, sd_ref, wqkv_ref, wf_ref, whbc_ref, fb_ref, cw_ref,
                       q_ref, k_ref, v_ref, lf_ref, yd_ref, sdo_ref):
    cwid = n_heads * head_dim
    dwid = cw_ref.shape[-1]
    taps = cw_ref.shape[0]
    nh = sd_ref.shape[0]
    xb = x_ref[...].astype(BF16)
    qkv = _dot(xb, wqkv_ref[...])
    q_ref[...] = qkv[:, :cwid]
    k_ref[...] = qkv[:, cwid:2 * cwid]
    v_ref[...] = qkv[:, 2 * cwid:]
    fl = _dot(xb, wf_ref[...])
    lf_ref[...] = _log_sigmoid(fl + fb_ref[...])
    hbc = _dot(xb, whbc_ref[...])
    g = hbc[:, 2 * dwid:] * hbc[:, :dwid]
    conv = g * cw_ref[taps - 1:taps, :]
    for kk in range(taps - 1):
        conv = conv + sd_ref[kk + nh - (taps - 1)] * cw_ref[kk:kk + 1, :]
    yd_ref[...] = (hbc[:, dwid:2 * dwid] * conv).astype(BF16)
    for j in range(nh - 1):
        sdo_ref[j] = sd_ref[j + 1]
    sdo_ref[nh - 1] = g


def _odd_sample(n_heads, head_dim, x, sd_t, wqkv, wf, whbc, fb, cw):
    n = x.shape[0]
    cwid = n_heads * head_dim
    dwid = cw.shape[-1]
    ins = (x, sd_t, wqkv, wf, whbc, fb, cw)
    shapes = [((n, cwid), F32), ((n, cwid), F32), ((n, cwid), F32), ((n, V7X_LANES), F32),
              ((n, dwid), BF16), (sd_t.shape, F32)]
    return pl.pallas_call(
        functools.partial(_odd_sample_kernel, n_heads, head_dim),
        grid=(1,),
        in_specs=[_const_spec(c.shape) for c in ins],
        out_specs=[_const_spec(s) for s, _ in shapes],
        out_shape=[jax.ShapeDtypeStruct(s, d) for s, d in shapes],
        compiler_params=_cparams(("arbitrary",)),
        name="odd_proj_sample",
    )(*ins)


def _fox_prompt_kernel(head_dim, qa_ref, ka_ref, va_ref, o_ref):
    i = pl.program_id(2)
    tq = qa_ref.shape[1]
    hw = V7X_LANES
    lane = lax.broadcasted_iota(I32, (tq, hw), 1)
    row = lax.broadcasted_iota(I32, (tq, tq), 0)
    col = lax.broadcasted_iota(I32, (tq, tq), 1)
    nh = qa_ref.shape[0]
    qs = [qa_ref[hh] for hh in range(nh)]

    def step(j, carry, masked):
        start = pl.multiple_of(j * tq, tq)
        new = []
        for hh in range(nh):
            m, acc = carry[hh]
            kt = ka_ref[hh, pl.ds(start, tq), :]
            vt = va_ref[hh, pl.ds(start, tq), :]
            s = _dot_nt(qs[hh], kt)
            if masked:
                s = jnp.where(col <= row, s, NEG)
            m_new = jnp.maximum(m, jnp.max(s, axis=-1, keepdims=True))
            p = jnp.exp2(s - m_new)
            acc = acc * jnp.exp2(m - m_new) + _dot(p.astype(BF16), vt)
            new.append((m_new, acc))
        return tuple(new)

    carry = tuple((jnp.full((tq, 1), NEG, F32), jnp.zeros((tq, hw), F32)) for _ in range(nh))
    carry = lax.fori_loop(0, i, lambda j, c: step(j, c, False), carry)
    carry = step(i, carry, True)
    outs = []
    for hh in range(nh):
        acc = carry[hh][1]
        denom = jnp.sum(jnp.where(lane == head_dim, acc, 0.0), axis=-1, keepdims=True)
        outs.append(acc / denom)
    o = outs[0]
    for hh in range(1, len(outs)):
        o = jnp.where(lane < hh * head_dim, o, pltpu.roll(outs[hh], hh * head_dim, axis=1))
    o_ref[...] = o.astype(o_ref.dtype)


def _fox_prompt(head_dim, qa, ka, va):
    B, H, T, hw = qa.shape
    tq = min(TOKEN_TILE, T)
    hp = hw // head_dim
    return pl.pallas_call(
        functools.partial(_fox_prompt_kernel, head_dim),
        grid=(B, H // hp, T // tq),
        in_specs=[pl.BlockSpec((None, hp, tq, hw), lambda b, p, i: (b, p, i, 0)),
                  pl.BlockSpec((None, hp, T, hw), lambda b, p, i: (b, p, 0, 0)),
                  pl.BlockSpec((None, hp, T, hw), lambda b, p, i: (b, p, 0, 0))],
        out_specs=pl.BlockSpec((None, tq, hw), lambda b, p, i: (b, i, p)),
        out_shape=jax.ShapeDtypeStruct((B, T, H * head_dim), BF16),
        compiler_params=_cparams(("arbitrary", "arbitrary", "arbitrary")),
        name="fox_attention_prompt",
    )(qa, ka, va)


def _fox_sample_kernel(npp, pt_ref, *refs):
    k_refs = refs[0:npp]
    v_refs = refs[npp:2 * npp]
    lf_refs = refs[2 * npp:3 * npp]
    (qt_ref, qh_ref, knh_ref, vnt_ref, lfn_ref, u_ref, o_ref, m_s, l_s, acc_s, carry_s) = refs[3 * npp:]
    i = pl.program_id(1)
    n_steps = pl.num_programs(1)
    H, dh, page = k_refs[0].shape

    @pl.when(i == 0)
    def _():
        m_s[...] = jnp.full(m_s.shape, NEG, F32)
        l_s[...] = jnp.zeros(l_s.shape, F32)
        acc_s[...] = jnp.zeros(acc_s.shape, F32)
        carry_s[...] = lfn_ref[...]

    lf_all = jnp.concatenate([r[...] for r in lf_refs], axis=0)
    l_hi, l_mid, l_lo = _split3(lf_all)
    u = u_ref[...]
    g_all = _dot(l_hi, u) + _dot(l_mid, u) + _dot(l_lo, u)
    tot = jnp.sum(lf_all, axis=1, keepdims=True)
    later = carry_s[...]
    decay = [None] * npp
    for j in reversed(range(npp)):
        decay[j] = later
        later = later + tot[j * H:(j + 1) * H, :]
    carry_s[...] = later

    qt = qt_ref[...]
    qb = [jnp.broadcast_to(qt[:, h:h + 1], (dh, page)) for h in range(H)]
    s_pages = []
    for j in range(npp):
        rows = [jnp.sum(k_refs[j][h] * qb[h], axis=0, keepdims=True) for h in range(H)]
        s_pages.append(jnp.concatenate(rows, axis=0) + g_all[j * H:(j + 1) * H, :] + decay[j])

    mx = s_pages[0]
    for j in range(1, npp):
        mx = jnp.maximum(mx, s_pages[j])
    m = m_s[...]
    m_new = jnp.maximum(m, jnp.max(mx, axis=1, keepdims=True))
    c = jnp.exp(m - m_new)
    p_pages = [jnp.exp(s - m_new) for s in s_pages]
    psum = p_pages[0]
    for j in range(1, npp):
        psum = psum + p_pages[j]
    l_s[...] = l_s[...] * c + psum
    m_s[...] = m_new
    for h in range(H):
        acc = acc_s[h] * c[h:h + 1, :]
        for j in range(npp):
            acc = acc + p_pages[j][h:h + 1, :] * v_refs[j][h]
        acc_s[h] = acc

    @pl.when(i == n_steps - 1)
    def _():
        s_new = jnp.sum(qh_ref[...] * knh_ref[...], axis=1, keepdims=True)
        m_fin = jnp.maximum(m_new, s_new)
        cf = jnp.exp(m_new - m_fin)
        p_new = jnp.exp(s_new - m_fin)
        denom = jnp.sum(l_s[...], axis=1, keepdims=True) * cf + p_new
        for h in range(H):
            num = (jnp.sum(acc_s[h], axis=1, keepdims=True) * cf[h:h + 1, :]
                   + p_new[h:h + 1, :] * vnt_ref[:, h:h + 1])
            o_ref[:, h:h + 1] = num / denom[h:h + 1, :]


def _fox_sample(page_table, kview, vview, lfview, qt, qh, knh, vnt, lfn):
    nb, n_pages = page_table.shape
    _, H, dh, page = kview.shape
    npp = min(PAGES_PER_STEP, n_pages)
    assert n_pages % npp == 0
    n_steps = n_pages // npp
    u = jnp.asarray(np.tril(np.ones((page, page), np.float32), -1), BF16)

    def page_spec(shape, j):
        nd = len(shape)
        return pl.BlockSpec((None,) + tuple(shape),
                            lambda b, i, pt, j=j: (pt[b, n_pages - npp * (i + 1) + j],) + (0,) * nd)

    seq = lambda shape: pl.BlockSpec((None,) + tuple(shape), lambda b, i, pt: (b,) + (0,) * len(shape))
    in_specs = ([page_spec((H, dh, page), j) for j in range(npp)]
                + [page_spec((H, dh, page), j) for j in range(npp)]
                + [page_spec((H, page), j) for j in range(npp)]
                + [seq((dh, H)), seq((H, dh)), seq((H, dh)), seq((dh, H)), seq((H, 1)),
                   pl.BlockSpec(u.shape, lambda b, i, pt: (0, 0))])
    grid_spec = pltpu.PrefetchScalarGridSpec(
        num_scalar_prefetch=1, grid=(nb, n_steps), in_specs=in_specs,
        out_specs=seq((dh, H)),
        scratch_shapes=[pltpu.VMEM((H, 1), F32), pltpu.VMEM((H, page), F32), pltpu.VMEM((H, dh, page), F32),
                        pltpu.VMEM((H, 1), F32)])
    return pl.pallas_call(
        functools.partial(_fox_sample_kernel, npp),
        grid_spec=grid_spec,
        out_shape=jax.ShapeDtypeStruct((nb, dh, H), F32),
        compiler_params=_cparams(("arbitrary", "arbitrary")),
        name="fox_attention_sample",
    )(page_table, *([kview] * npp), *([vview] * npp), *([lfview] * npp), qt, qh, knh, vnt, lfn, u)


def _top2(logits, axis, n_experts):
    idx = lax.broadcasted_iota(I32, logits.shape, axis)
    big = logits.shape[axis]
    lg = jnp.where(idx < n_experts, logits, -jnp.inf)
    v1 = jnp.max(lg, axis=axis, keepdims=True)
    i1 = jnp.min(jnp.where(lg == v1, idx, big), axis=axis, keepdims=True)
    lg2 = jnp.where(idx == i1, -jnp.inf, lg)
    v2 = jnp.max(lg2, axis=axis, keepdims=True)
    i2 = jnp.min(jnp.where(lg2 == v2, idx, big), axis=axis, keepdims=True)
    return idx, i1, i2, v1, v2


def _odd_out_kernel(alpha, n_experts, n_prompt_tiles, op_ref, ydp_ref, xp_ref, os_ref, yds_ref, xs_ref,
                    wout_ref, g_ref, b_ref, rw_ref, x3p_ref, x3s_ref, x3b_ref, lg_ref, cnt_ref):
    i = pl.program_id(0)
    cwid = op_ref.shape[1]

    def rows(o_ref, yd_ref, x_ref, x3_ref):
        n = o_ref.shape[0]
        mix = _dot(o_ref[...].astype(BF16), wout_ref[0:cwid, :]) + _dot(yd_ref[...], wout_ref[cwid:, :])
        x3 = _ln(alpha * x_ref[...] + mix, g_ref[...], b_ref[...])
        x3b = x3.astype(BF16)
        logits = _dot(x3b, rw_ref[...])
        x3_ref[...] = x3
        if n < x3b_ref.shape[0]:
            x3b_ref[...] = jnp.zeros(x3b_ref.shape, BF16)
            lg_ref[...] = jnp.zeros(lg_ref.shape, F32)
        x3b_ref[0:n, :] = x3b
        lg_ref[0:n, :] = logits
        idx, i1, i2, _, _ = _top2(logits, 1, n_experts)
        mask = jnp.where((idx == i1) | (idx == i2), 1.0, 0.0)
        cnt_ref[...] = jnp.sum(mask, axis=0, keepdims=True)

    @pl.when(i < n_prompt_tiles)
    def _():
        rows(op_ref, ydp_ref, xp_ref, x3p_ref)

    @pl.when(i >= n_prompt_tiles)
    def _():
        rows(os_ref, yds_ref, xs_ref, x3s_ref)


def _odd_out(alpha, n_experts, o_p, yd_p, x_p, o_s, yd_s, x_s, wout, g, b, rw):
    n, d = x_p.shape
    ns = x_s.shape[0]
    tm = TOKEN_TILE
    npt = n // tm
    ntiles = npt + 1
    hw = V7X_LANES
    prow = lambda w: pl.BlockSpec((tm, w), lambda i: (jnp.minimum(i, npt - 1), 0))
    return pl.pallas_call(
        functools.partial(_odd_out_kernel, alpha, n_experts, npt),
        grid=(ntiles,),
        in_specs=[prow(o_p.shape[1]), prow(yd_p.shape[1]), prow(d),
                  _const_spec(o_s.shape), _const_spec(yd_s.shape), _const_spec(x_s.shape)]
        + [_const_spec(c.shape) for c in (wout, g, b, rw)],
        out_specs=[prow(d), _const_spec((ns, d)),
                   pl.BlockSpec((tm, d), lambda i: (i, 0)),
                   pl.BlockSpec((tm, hw), lambda i: (i, 0)),
                   pl.BlockSpec((None, 1, hw), lambda i: (i, 0, 0))],
        out_shape=[jax.ShapeDtypeStruct((n, d), F32), jax.ShapeDtypeStruct((ns, d), F32),
                   jax.ShapeDtypeStruct((ntiles * tm, d), BF16),
                   jax.ShapeDtypeStruct((ntiles * tm, hw), F32),
                   jax.ShapeDtypeStruct((ntiles, 1, hw), F32)],
        compiler_params=_cparams(("arbitrary",)),
        name="odd_out_router",
    )(o_p, yd_p, x_p, o_s, yd_s, x_s, wout, g, b, rw)


def _segment_copies(tile, n_experts, npc_ref, loff_ref, goff_ref, make_copy):
    ratio = SEG_BIG // SEG_PAD
    for e in range(n_experts):
        n = npc_ref[tile * n_experts + e]
        lo = loff_ref[tile * n_experts + e]
        go = goff_ref[tile * n_experts + e]
        nbig = n // ratio

        def big(j, _, lo=lo, go=go):
            make_copy(pl.multiple_of(lo + j * SEG_BIG, SEG_PAD), pl.multiple_of(go + j * SEG_BIG, SEG_PAD), SEG_BIG)
            return 0

        def small(j, _, lo=lo, go=go, nbig=nbig):
            off = nbig * SEG_BIG + j * SEG_PAD
            make_copy(pl.multiple_of(lo + off, SEG_PAD), pl.multiple_of(go + off, SEG_PAD), SEG_PAD)
            return 0

        lax.fori_loop(0, nbig, big, 0)
        lax.fori_loop(0, n - nbig * ratio, small, 0)


def _dispatch_kernel(n_experts, n_valid, npc_ref, loff_ref, goff_ref, x_ref, lt_ref, lofft_ref, triu_ref,
                     xs_in_ref, xs_ref, stage, sem):
    del xs_in_ref
    tile = pl.program_id(0)
    tt = x_ref.shape[0]
    rl = stage.shape[0]
    sub, i1, i2, _, _ = _top2(lt_ref[...], 0, n_experts)
    valid = (tile * tt + lax.broadcasted_iota(I32, (1, tt), 1)) < n_valid
    sel1 = (sub == i1) & valid
    sel2 = (sub == i2) & valid
    mask = jnp.where(sel1 | sel2, 1.0, 0.0)
    rank = _dot(mask.astype(BF16), triu_ref[...])
    loc = rank + jnp.concatenate([lofft_ref[...]] * (tt // V7X_LANES), axis=1)
    lr1 = jnp.sum(jnp.where(sel1, loc, 0.0), axis=0, keepdims=True).astype(I32)
    lr2 = jnp.sum(jnp.where(sel2, loc, 0.0), axis=0, keepdims=True).astype(I32)
    lr1 = jnp.where(valid, lr1, -1)
    lr2 = jnp.where(valid, lr2, -1)
    r = lax.broadcasted_iota(I32, (rl, tt), 0)
    onehot = jnp.where((r == lr1) | (r == lr2), 1.0, 0.0).astype(BF16)
    stage[...] = _dot(onehot, x_ref[...]).astype(BF16)

    def copy(lo, go, nrows):
        return pltpu.make_async_copy(stage.at[pl.ds(lo, nrows), :], xs_ref.at[pl.ds(go, nrows), :], sem)

    _segment_copies(tile, n_experts, npc_ref, loff_ref, goff_ref, lambda lo, go, n: copy(lo, go, n).start())
    _segment_copies(tile, n_experts, npc_ref, loff_ref, goff_ref, lambda lo, go, n: copy(lo, go, n).wait())


def _dispatch(n_experts, n_valid, npc, loff, goff, x3b, lt, lofft, xs_zero):
    ntot, d = x3b.shape
    tt = TOKEN_TILE
    triu = jnp.asarray(np.triu(np.ones((tt, tt), np.float32), 1), BF16)
    grid_spec = pltpu.PrefetchScalarGridSpec(
        num_scalar_prefetch=3, grid=(ntot // tt,),
        in_specs=[pl.BlockSpec((tt, d), lambda i, *_: (i, 0)),
                  pl.BlockSpec((n_experts, tt), lambda i, *_: (0, i)),
                  pl.BlockSpec((None, n_experts, V7X_LANES), lambda i, *_: (i, 0, 0)),
                  pl.BlockSpec(triu.shape, lambda i, *_: (0, 0)),
                  pl.BlockSpec(memory_space=pl.ANY)],
        out_specs=pl.BlockSpec(memory_space=pl.ANY),
        scratch_shapes=[pltpu.VMEM((LOCAL_ROWS, d), BF16), pltpu.SemaphoreType.DMA(())])
    return pl.pallas_call(
        functools.partial(_dispatch_kernel, n_experts, n_valid),
        grid_spec=grid_spec,
        out_shape=jax.ShapeDtypeStruct(xs_zero.shape, BF16),
        input_output_aliases={7: 0},
        compiler_params=_cparams(("arbitrary",)),
        name="moe_dispatch",
    )(npc, loff, goff, x3b, lt, lofft, triu, xs_zero)


def _experts_kernel(te_ref, tv_ref, xr_ref, x_ref, w1_ref, w3_ref, w2_ref, yh_ref, yl_ref, h_ref):
    del te_ref, xr_ref
    i = pl.program_id(0)
    c = pl.program_id(1)
    last = pl.num_programs(1) - 1
    fc = w1_ref.shape[1]
    valid = tv_ref[i] > 0

    @pl.when(valid)
    def _():
        x = x_ref[...]
        h = (_silu(_dot(x, w1_ref[...])) * _dot(x, w3_ref[...])).astype(BF16)
        h_ref[:, pl.ds(pl.multiple_of(c * fc, fc), fc)] = h

        @pl.when(c == last)
        def _():
            y = _dot(h_ref[...], w2_ref[...])
            hi = y.astype(BF16)
            yh_ref[...] = hi
            yl_ref[...] = (y - hi.astype(F32)).astype(BF16)

    @pl.when(jnp.logical_not(valid) & (c == last))
    def _():
        yh_ref[...] = jnp.zeros(yh_ref.shape, BF16)
        yl_ref[...] = jnp.zeros(yl_ref.shape, BF16)


def _experts(te, tv, xr, xs, w1, w3, w2):
    rmax, d = xs.shape
    tm = EXPERT_TILE
    ff = w1.shape[-1]
    fc = min(EXPERT_FF_CHUNK, ff)
    nch = ff // fc
    assert ff % fc == 0 and rmax % tm == 0

    def chunk(i, c, tv):
        return jnp.where(tv[i] > 0, c, nch - 1)

    grid_spec = pltpu.PrefetchScalarGridSpec(
        num_scalar_prefetch=3, grid=(rmax // tm, nch),
        in_specs=[pl.BlockSpec((tm, d), lambda i, c, te, tv, xr: (xr[i], 0)),
                  pl.BlockSpec((None, d, fc), lambda i, c, te, tv, xr: (te[i], 0, chunk(i, c, tv))),
                  pl.BlockSpec((None, d, fc), lambda i, c, te, tv, xr: (te[i], 0, chunk(i, c, tv))),
                  pl.BlockSpec((None, ff, d), lambda i, c, te, tv, xr: (te[i], 0, 0))],
        out_specs=[pl.BlockSpec((tm, d), lambda i, c, te, tv, xr: (i, 0)),
                   pl.BlockSpec((tm, d), lambda i, c, te, tv, xr: (i, 0))],
        scratch_shapes=[pltpu.VMEM((tm, ff), BF16)])
    return pl.pallas_call(
        _experts_kernel,
        grid_spec=grid_spec,
        out_shape=[jax.ShapeDtypeStruct((rmax, d), BF16), jax.ShapeDtypeStruct((rmax, d), BF16)],
        compiler_params=_cparams(("arbitrary", "arbitrary")),
        name="moe_experts",
    )(te, tv, xr, xs, w1, w3, w2)


def _combine_kernel(alpha, n_experts, n_valid, n_prompt_tiles, npc_ref, loff_ref, goff_ref, lg_ref, xp_ref,
                    xs_ref, loffr_ref, tril_ref, g_ref, b_ref, yh_hbm, yl_hbm, yp_ref, ysm_ref,
                    sth, stl, sem):
    tile = pl.program_id(0)
    tt = lg_ref.shape[0]
    rl = sth.shape[0]
    d = sth.shape[1]

    def copies(fn):
        _segment_copies(tile, n_experts, npc_ref, loff_ref, goff_ref,
                        lambda lo, go, n: fn(pltpu.make_async_copy(yh_hbm.at[pl.ds(go, n), :],
                                                                  sth.at[pl.ds(lo, n), :], sem.at[0])))
        _segment_copies(tile, n_experts, npc_ref, loff_ref, goff_ref,
                        lambda lo, go, n: fn(pltpu.make_async_copy(yl_hbm.at[pl.ds(go, n), :],
                                                                  stl.at[pl.ds(lo, n), :], sem.at[1])))

    copies(lambda cp: cp.start())

    last = tile * n_experts + n_experts - 1
    used = loff_ref[last] // SEG_PAD + npc_ref[last]

    def clear(j, _):
        o = pl.multiple_of(j * SEG_PAD, SEG_PAD)
        sth[pl.ds(o, SEG_PAD), :] = jnp.zeros((SEG_PAD, d), BF16)
        stl[pl.ds(o, SEG_PAD), :] = jnp.zeros((SEG_PAD, d), BF16)
        return 0

    lax.fori_loop(used, rl // SEG_PAD, clear, 0)

    lane, i1, i2, v1, v2 = _top2(lg_ref[...], 1, n_experts)
    valid = (tile * tt + lax.broadcasted_iota(I32, (tt, 1), 0)) < n_valid
    sel1 = (lane == i1) & valid
    sel2 = (lane == i2) & valid
    mask = jnp.where(sel1 | sel2, 1.0, 0.0)
    rank = _dot(tril_ref[...], mask.astype(BF16))
    loc = rank + jnp.concatenate([loffr_ref[...]] * (tt // V7X_SUBLANES), axis=0)
    lr1 = jnp.sum(jnp.where(sel1, loc, 0.0), axis=1, keepdims=True).astype(I32)
    lr2 = jnp.sum(jnp.where(sel2, loc, 0.0), axis=1, keepdims=True).astype(I32)
    lr1 = jnp.where(valid, lr1, -1)
    lr2 = jnp.where(valid, lr2, -1)
    e21 = jnp.exp(v2 - v1)
    g1 = 1.0 / (1.0 + e21)
    g2 = e21 / (1.0 + e21)
    r = lax.broadcasted_iota(I32, (tt, rl), 1)
    gm = jnp.where(r == lr1, g1, 0.0) + jnp.where(r == lr2, g2, 0.0)
    gm_hi = gm.astype(BF16)
    gm_lo = (gm - gm_hi.astype(F32)).astype(BF16)

    copies(lambda cp: cp.wait())
    yh = sth[...]
    moe = _dot(gm_hi, yh) + _dot(gm_hi, stl[...]) + _dot(gm_lo, yh)

    @pl.when(tile < n_prompt_tiles)
    def _():
        yp_ref[...] = _ln(alpha * xp_ref[...] + moe, g_ref[...], b_ref[...])

    @pl.when(tile >= n_prompt_tiles)
    def _():
        ns = ysm_ref.shape[0]
        ysm_ref[...] = _ln(alpha * xs_ref[...] + moe[0:ns, :], g_ref[...], b_ref[...])


def _combine(alpha, n_experts, n_valid, npc, loff, goff, logits, x3p, x3s, loffr, g, b, yh, yl):
    npr, d = x3p.shape
    ns = x3s.shape[0]
    tt = TOKEN_TILE
    npt = npr // tt
    ntiles = logits.shape[0] // tt
    tril = jnp.asarray(np.tril(np.ones((tt, tt), np.float32), -1), BF16)
    grid_spec = pltpu.PrefetchScalarGridSpec(
        num_scalar_prefetch=3, grid=(ntiles,),
        in_specs=[pl.BlockSpec((tt, V7X_LANES), lambda i, *_: (i, 0)),
                  pl.BlockSpec((tt, d), lambda i, *_: (jnp.minimum(i, npt - 1), 0)),
                  pl.BlockSpec((ns, d), lambda i, *_: (0, 0)),
                  pl.BlockSpec((None, V7X_SUBLANES, V7X_LANES), lambda i, *_: (i, 0, 0)),
                  pl.BlockSpec(tril.shape, lambda i, *_: (0, 0)),
                  pl.BlockSpec(g.shape, lambda i, *_: (0, 0)),
                  pl.BlockSpec(b.shape, lambda i, *_: (0, 0)),
                  pl.BlockSpec(memory_space=pl.ANY),
                  pl.BlockSpec(memory_space=pl.ANY)],
        out_specs=[pl.BlockSpec((tt, d), lambda i, *_: (jnp.minimum(i, npt - 1), 0)),
                   pl.BlockSpec((ns, d), lambda i, *_: (0, 0))],
        scratch_shapes=[pltpu.VMEM((LOCAL_ROWS, d), BF16), pltpu.VMEM((LOCAL_ROWS, d), BF16),
                        pltpu.SemaphoreType.DMA((2,))])
    return pl.pallas_call(
        functools.partial(_combine_kernel, alpha, n_experts, n_valid, npt),
        grid_spec=grid_spec,
        out_shape=[jax.ShapeDtypeStruct((npr, d), F32), jax.ShapeDtypeStruct((ns, d), F32)],
        compiler_params=_cparams(("arbitrary",)),
        name="moe_combine",
    )(npc, loff, goff, logits, x3p, x3s, loffr, tril, g, b, yh, yl)


def _routing_plan(cnt, n_experts):
    ntiles = cnt.shape[0]
    pc = (cnt + SEG_PAD - 1) // SEG_PAD * SEG_PAD
    loff = jnp.cumsum(pc, axis=1) - pc
    per_expert = jnp.sum(pc, axis=0)
    gp = (per_expert + EXPERT_TILE - 1) // EXPERT_TILE * EXPERT_TILE
    gend = jnp.cumsum(gp)
    goff = (gend - gp)[None, :] + jnp.cumsum(pc, axis=0) - pc
    rmax = TOP_K * ntiles * TOKEN_TILE + ntiles * n_experts * (SEG_PAD - 1) + n_experts * (EXPERT_TILE - 1)
    nt_max = -(-rmax // EXPERT_TILE)
    tiles_used = gend[-1] // EXPERT_TILE
    ti = jnp.arange(nt_max, dtype=I32)
    tv = (ti < tiles_used).astype(I32)
    xr = jnp.maximum(jnp.minimum(ti, tiles_used - 1), 0)
    te = jnp.sum((xr[:, None] >= (gend // EXPERT_TILE)[None, :]).astype(I32), axis=1)
    te = jnp.minimum(te, n_experts - 1)
    flat = lambda a: a.reshape(-1).astype(I32)
    return (flat(pc // SEG_PAD), flat(loff), flat(goff), loff.astype(F32), te, tv, xr.astype(I32),
            nt_max * EXPERT_TILE)


def kernel(x_prompt, x_sample, state_pool, state_conv_b, cache_k, cache_v, cache_logf, state_conv_d, page_table,
           w_in_even, pool_w, pool_scale, conv_b_w, conv_b_bias, conv_ln_g, conv_ln_b, w_out_even, ln_mix_even_g,
           ln_mix_even_b, ffn_w1, ffn_w3, ffn_w2, ln_ffn_even_g, ln_ffn_even_b, w_in_odd, forget_bias, conv_d_w,
           w_out_odd, ln_mix_odd_g, ln_mix_odd_b, router_w, moe_w1, moe_w3, moe_w2, ln_ffn_odd_g, ln_ffn_odd_b):
    assert w_in_even.shape[0] == 1 and w_in_odd.shape[0] == 1, "one even and one odd layer are supported"
    depth = w_in_even.shape[0] + w_in_odd.shape[0]
    alpha = float((2 * depth) ** 0.25)
    B, T, D = x_prompt.shape
    nb = x_sample.shape[0]
    assert x_sample.shape[1] == 1 and T % TOKEN_TILE == 0 and nb <= TOKEN_TILE
    n_heads = forget_bias.shape[-1]
    cwid = cache_k.shape[-1] * cache_k.shape[-2]
    head_dim = cache_k.shape[-1]
    dwid = conv_d_w.shape[-1]
    n_experts = router_w.shape[-1]
    past_len = page_table.shape[1] * cache_k.shape[2]
    hw = V7X_LANES
    bf = lambda w: w.astype(BF16)
    rowv = lambda w: w.reshape(1, -1)

    ew = (bf(w_in_even[0]), bf(pool_w[0]), pool_scale, conv_b_w[0], conv_b_bias, conv_ln_g, conv_ln_b,
          bf(w_out_even[0]), ln_mix_even_g, ln_mix_even_b)
    xp, a_hist, u_hist = _even_prompt(alpha, x_prompt, *ew)
    xs, pool_s_t, convb_s_t = _even_sample(
        alpha, past_len, x_sample.reshape(nb, D), jnp.swapaxes(state_pool[0], 0, 1),
        jnp.swapaxes(state_conv_b[0], 0, 1), *ew)
    nph = state_pool.shape[2]
    nch = state_conv_b.shape[2]
    pool_p = a_hist[None, :, POOL_HIST_ROWS - nph:, :]
    convb_p = u_hist[None, :, CONV_B_HIST_ROWS - nch:, :]
    pool_s = jnp.swapaxes(pool_s_t, 0, 1)[None]
    convb_s = jnp.swapaxes(convb_s_t, 0, 1)[None]

    fw = (bf(ffn_w1[0]), bf(ffn_w3[0]), bf(ffn_w2[0]), ln_ffn_even_g, ln_ffn_even_b)
    xp = _ffn(alpha, xp.reshape(B * T, D), *fw)
    xs = _ffn(alpha, xs, *fw)

    w_in = w_in_odd[0]
    wqkv = bf(w_in[:, :3 * cwid])
    wf = bf(jnp.pad(w_in[:, 3 * cwid:3 * cwid + n_heads], ((0, 0), (0, hw - n_heads))))
    whbc = bf(w_in[:, 3 * cwid + n_heads:])
    fb = jnp.pad(forget_bias, ((0, 0), (0, hw - n_heads)))
    k_p, v_p, lf_p, qa, ka, va, yd_p, cd_p = _odd_prompt(
        n_heads, head_dim, xp.reshape(B, T, D), wqkv, wf, whbc, fb, conv_d_w[0])
    o_p = _fox_prompt(head_dim, qa, ka, va)

    q_s, k_s, v_s, lf_s, yd_s, convd_s_t = _odd_sample(
        n_heads, head_dim, xs, jnp.swapaxes(state_conv_d[0], 0, 1), wqkv, wf, whbc, fb, conv_d_w[0])
    heads = lambda z: z.reshape(nb, n_heads, head_dim)
    heads_t = lambda z: jnp.swapaxes(heads(z), 1, 2)
    q_sc = q_s * head_dim ** -0.5
    lf_s = lf_s[:, :n_heads]
    o_s_t = _fox_sample(page_table, jnp.transpose(cache_k[0], (0, 2, 3, 1)), jnp.transpose(cache_v[0], (0, 2, 3, 1)),
                        jnp.transpose(cache_logf[0], (0, 2, 1)), heads_t(q_sc), heads(q_sc), heads(k_s),
                        heads_t(v_s), lf_s.reshape(nb, n_heads, 1))
    o_s = jnp.swapaxes(o_s_t, 1, 2).reshape(nb, cwid)

    n_prompt = B * T
    npt = n_prompt // TOKEN_TILE
    ntiles = npt + 1
    n_valid = n_prompt + nb
    ow = (bf(w_out_odd[0]), ln_mix_odd_g, ln_mix_odd_b,
          bf(jnp.pad(router_w[0], ((0, 0), (0, hw - n_experts)))))
    x3p, x3s, x3b, logits, cnt = _odd_out(alpha, n_experts, o_p.reshape(n_prompt, cwid),
                                          yd_p.reshape(n_prompt, dwid), xp, o_s, yd_s, xs, *ow)

    cnt = cnt[:, 0, :n_experts].astype(I32)
    npc, loff, goff, loff_f, te, tv, xr, rmax = _routing_plan(cnt, n_experts)
    lt = jnp.transpose(logits[:, :n_experts])
    lofft = jnp.broadcast_to(loff_f[:, :, None], (ntiles, n_experts, hw))
    loffr = jnp.broadcast_to(jnp.pad(loff_f, ((0, 0), (0, hw - n_experts)))[:, None, :],
                             (ntiles, V7X_SUBLANES, hw))
    xs_sorted = _dispatch(n_experts, n_valid, npc, loff, goff, x3b, lt, lofft, jnp.zeros((rmax, D), BF16))
    yh, yl = _experts(te, tv, xr, xs_sorted, bf(moe_w1[0]), bf(moe_w3[0]), bf(moe_w2[0]))
    y_p, y_s = _combine(alpha, n_experts, n_valid, npc, loff, goff, logits, x3p, x3s, loffr,
                        ln_ffn_odd_g, ln_ffn_odd_b, yh, yl)

    nd = state_conv_d.shape[2]
    return (y_p.reshape(B, T, D), y_s.reshape(nb, 1, D),
            pool_p, pool_s, convb_p, convb_s,
            k_p.reshape(1, B, T, n_heads, head_dim), k_s.reshape(1, nb, 1, n_heads, head_dim),
            v_p.reshape(1, B, T, n_heads, head_dim), v_s.reshape(1, nb, 1, n_heads, head_dim),
            lf_p.reshape(1, B, T, n_heads), lf_s.reshape(1, nb, 1, n_heads),
            cd_p[None, :, CONV_D_HIST_ROWS - nd:, :], jnp.swapaxes(convd_s_t, 0, 1)[None])
```

```python
import functools

import numpy as np
import jax
import jax.numpy as jnp
from jax import lax
from jax.experimental import pallas as pl
from jax.experimental.pallas import tpu as pltpu

F32 = jnp.float32
BF16 = jnp.bfloat16
I32 = jnp.int32

LN_EPS = 1e-5
POOL_WINDOWS = (2, 4, 8, 16)
TOP_K = 2
NEG = -1e30
LOG2E = 1.4426950408889634

V7X_VMEM_BYTES = 64 * 1024 * 1024
V7X_LANES = 128
V7X_SUBLANES = 8
BF16_ROWS_PER_TILE = 2 * V7X_SUBLANES

VMEM_LIMIT = V7X_VMEM_BYTES - 8 * 1024 * 1024

TOKEN_TILE = 512
EXPERT_TILE = 512
EXPERT_FF_CHUNK = 1792
FFN_CHUNK = 256
CONV_ROWS = 64
POOL_HIST_ROWS = 16
CONV_B_HIST_ROWS = 32
CONV_D_HIST_ROWS = 8
PAGES_PER_STEP = 16
SEG_PAD = BF16_ROWS_PER_TILE
SEG_BIG = 128
LOCAL_ROWS = TOP_K * TOKEN_TILE + 8 * SEG_PAD


def _cparams(sem):
    return pltpu.CompilerParams(dimension_semantics=sem, vmem_limit_bytes=VMEM_LIMIT)


def _dot(a, b):
    return jnp.dot(a, b, preferred_element_type=F32)


def _dot_nt(a, b):
    return lax.dot_general(a, b, (((1,), (1,)), ((), ())), preferred_element_type=F32)


def _dot_tn(a, b):
    return lax.dot_general(a, b, (((0,), (0,)), ((), ())), preferred_element_type=F32)


def _ln(z, g, b):
    mu = jnp.mean(z, axis=-1, keepdims=True)
    d = z - mu
    var = jnp.mean(d * d, axis=-1, keepdims=True)
    return d * lax.rsqrt(var + LN_EPS) * g + b


def _silu(x):
    return x * jax.nn.sigmoid(x)


def _log_sigmoid(z):
    return jnp.minimum(z, 0.0) - jnp.log1p(jnp.exp(-jnp.abs(z)))


def _split3(x):
    hi = x.astype(BF16)
    r = x - hi.astype(F32)
    mid = r.astype(BF16)
    lo = (r - mid.astype(F32)).astype(BF16)
    return hi, mid, lo


def _const_spec(shape):
    nd = len(shape)
    return pl.BlockSpec(shape, lambda *_: (0,) * nd)


def _resident_spec(shape):
    nd = len(shape)
    return pl.BlockSpec(shape, lambda *_: (0,) * nd, pipeline_mode=pl.Buffered(1))


def _pool_groups(a, hist_fn, pos, pw_ref):
    outs = []
    gw = a.shape[1] // len(POOL_WINDOWS)
    for g, w in enumerate(POOL_WINDOWS):
        c0 = g * gw
        cur = a[:, c0:c0 + gw]
        win = cur
        for k in range(1, w):
            win = win + hist_fn(k, c0, gw)
        cnt = jnp.minimum(pos + 1, w).astype(F32)
        pooled = win / cnt - cur
        outs.append(_dot(pooled.astype(BF16), pw_ref[g]))
    return jnp.concatenate(outs, axis=-1)


def _even_prompt_kernel(alpha, x_ref, win_ref, pw_ref, ps_ref, cw_ref, cb_ref, cg_ref, cbeta_ref,
                        wout_ref, g_ref, b_ref, y_ref, ah_ref, uh_ref, aext, uext, ush, ybuf):
    t = pl.program_id(1)
    tT = x_ref.shape[0]
    aw = ps_ref.shape[-1]
    bw = cw_ref.shape[-1]
    taps = cw_ref.shape[0]
    AH, UH = POOL_HIST_ROWS, CONV_B_HIST_ROWS

    @pl.when(t == 0)
    def _():
        aext[0:AH, :] = jnp.zeros((AH, aw), F32)
        uext[0:UH, :] = jnp.zeros((UH, bw), F32)

    @pl.when(t > 0)
    def _():
        aext[0:AH, :] = aext[tT:tT + AH, :]
        uext[0:UH, :] = uext[tT:tT + UH, :]

    x = x_ref[...]
    proj = _dot(x.astype(BF16), win_ref[...])
    a = proj[:, :aw]
    u = proj[:, aw:aw + bw] * jax.nn.sigmoid(proj[:, aw + bw:])
    aext[AH:, :] = a
    uext[UH:, :] = u

    pos = t * tT + lax.broadcasted_iota(I32, (tT, 1), 0)
    ya = _pool_groups(a, lambda k, c0, gw: aext[AH - k:AH - k + tT, c0:c0 + gw], pos, pw_ref)
    ya = ya * ps_ref[...]

    base = UH - (taps - 1)
    ns = V7X_SUBLANES
    span = UH + tT - ns
    for s in range(1, ns):
        ush[s - 1, 0:span, :] = uext[s:s + span, :]

    def tap(k, r0):
        off = base + k
        a, s = off - off % ns + r0, off % ns
        rows = uext[a:a + CONV_ROWS, :] if s == 0 else ush[s - 1, a:a + CONV_ROWS, :]
        return rows * cw_ref[k:k + 1, :]

    for r0 in range(0, tT, CONV_ROWS):
        acc = tap(0, r0)
        for k in range(1, taps):
            acc = acc + tap(k, r0)
        yb = _silu(_ln(acc + cb_ref[...], cg_ref[...], cbeta_ref[...]))
        ybuf[r0:r0 + CONV_ROWS, :] = yb.astype(BF16)

    mix = _dot(ya.astype(BF16), wout_ref[0:aw, :]) + _dot(ybuf[...], wout_ref[aw:, :])
    y_ref[...] = _ln(alpha * x + mix, g_ref[...], b_ref[...])
    ah_ref[...] = aext[tT:tT + AH, :]
    uh_ref[...] = uext[tT:tT + UH, :]


def _even_prompt(alpha, x, win, pw, ps, cw, cb, cg, cbeta, wout, g, b):
    B, T, D = x.shape
    tT = min(TOKEN_TILE, T)
    aw, bw = ps.shape[-1], cw.shape[-1]
    AH, UH = POOL_HIST_ROWS, CONV_B_HIST_ROWS
    consts = (win, pw, ps, cw, cb, cg, cbeta, wout, g, b)
    return pl.pallas_call(
        functools.partial(_even_prompt_kernel, alpha),
        grid=(B, T // tT),
        in_specs=[pl.BlockSpec((None, tT, D), lambda bi, ti: (bi, ti, 0))]
        + [_const_spec(c.shape) for c in consts],
        out_specs=[pl.BlockSpec((None, tT, D), lambda bi, ti: (bi, ti, 0)),
                   pl.BlockSpec((None, AH, aw), lambda bi, ti: (bi, 0, 0)),
                   pl.BlockSpec((None, UH, bw), lambda bi, ti: (bi, 0, 0))],
        out_shape=[jax.ShapeDtypeStruct((B, T, D), F32),
                   jax.ShapeDtypeStruct((B, AH, aw), F32),
                   jax.ShapeDtypeStruct((B, UH, bw), F32)],
        scratch_shapes=[pltpu.VMEM((AH + tT, aw), F32), pltpu.VMEM((UH + tT, bw), F32),
                        pltpu.VMEM((V7X_SUBLANES - 1, UH + tT, bw), F32), pltpu.VMEM((tT, bw), BF16)],
        compiler_params=_cparams(("arbitrary", "arbitrary")),
        name="even_mixer_prompt",
    )(x, *consts)


def _even_sample_kernel(alpha, first_pos, x_ref, sp_ref, sc_ref, win_ref, pw_ref, ps_ref, cw_ref, cb_ref,
                        cg_ref, cbeta_ref, wout_ref, g_ref, b_ref, y_ref, spo_ref, sco_ref):
    aw = ps_ref.shape[-1]
    bw = cw_ref.shape[-1]
    taps = cw_ref.shape[0]
    nph = sp_ref.shape[0]
    nch = sc_ref.shape[0]
    x = x_ref[...]
    proj = _dot(x.astype(BF16), win_ref[...])
    a = proj[:, :aw]
    u = proj[:, aw:aw + bw] * jax.nn.sigmoid(proj[:, aw + bw:])

    pos = jnp.full((x.shape[0], 1), first_pos, I32)
    ya = _pool_groups(a, lambda k, c0, gw: sp_ref[nph - k, :, c0:c0 + gw], pos, pw_ref)
    ya = ya * ps_ref[...]

    acc = u * cw_ref[taps - 1:taps, :]
    for k in range(taps - 1):
        acc = acc + sc_ref[k + nch - (taps - 1)] * cw_ref[k:k + 1, :]
    yb = _silu(_ln(acc + cb_ref[...], cg_ref[...], cbeta_ref[...]))

    mix = _dot(ya.astype(BF16), wout_ref[0:aw, :]) + _dot(yb.astype(BF16), wout_ref[aw:, :])
    y_ref[...] = _ln(alpha * x + mix, g_ref[...], b_ref[...])
    for j in range(nph - 1):
        spo_ref[j] = sp_ref[j + 1]
    spo_ref[nph - 1] = a
    for j in range(nch - 1):
        sco_ref[j] = sc_ref[j + 1]
    sco_ref[nch - 1] = u


def _even_sample(alpha, first_pos, x, sp_t, sc_t, win, pw, ps, cw, cb, cg, cbeta, wout, g, b):
    ins = (x, sp_t, sc_t, win, pw, ps, cw, cb, cg, cbeta, wout, g, b)
    return pl.pallas_call(
        functools.partial(_even_sample_kernel, alpha, first_pos),
        grid=(1,),
        in_specs=[_const_spec(c.shape) for c in ins],
        out_specs=[_const_spec(x.shape), _const_spec(sp_t.shape), _const_spec(sc_t.shape)],
        out_shape=[jax.ShapeDtypeStruct(x.shape, F32), jax.ShapeDtypeStruct(sp_t.shape, F32),
                   jax.ShapeDtypeStruct(sc_t.shape, F32)],
        compiler_params=_cparams(("arbitrary",)),
        name="even_mixer_sample",
    )(*ins)


def _ffn_kernel(alpha, x_ref, w1_ref, w3_ref, w2_ref, g_ref, b_ref, y_ref, h_ref):
    x = x_ref[...]
    xb = x.astype(BF16)
    ff = w1_ref.shape[1]
    for c in range(0, ff, FFN_CHUNK):
        h1 = _dot(xb, w1_ref[:, c:c + FFN_CHUNK])
        h3 = _dot(xb, w3_ref[:, c:c + FFN_CHUNK])
        h_ref[:, c:c + FFN_CHUNK] = (_silu(h1) * h3).astype(BF16)
    y = _dot(h_ref[...], w2_ref[...])
    y_ref[...] = _ln(alpha * x + y, g_ref[...], b_ref[...])


def _ffn(alpha, x, w1, w3, w2, g, b):
    n, d = x.shape
    tm = min(TOKEN_TILE, n)
    ff = w1.shape[1]
    assert ff % FFN_CHUNK == 0 and n % tm == 0
    return pl.pallas_call(
        functools.partial(_ffn_kernel, alpha),
        grid=(n // tm,),
        in_specs=[pl.BlockSpec((tm, d), lambda i: (i, 0)),
                  _resident_spec(w1.shape), _resident_spec(w3.shape), _resident_spec(w2.shape),
                  _const_spec(g.shape), _const_spec(b.shape)],
        out_specs=pl.BlockSpec((tm, d), lambda i: (i, 0)),
        out_shape=jax.ShapeDtypeStruct((n, d), F32),
        scratch_shapes=[pltpu.VMEM((tm, ff), BF16)],
        compiler_params=_cparams(("arbitrary",)),
        name="ffn_swiglu",
    )(x, w1, w3, w2, g, b)


def _head_select_mats(n_heads, head_dim):
    hw = V7X_LANES
    one_col = 3 * n_heads
    sq = np.zeros((hw, n_heads * hw), np.float32)
    sk = np.zeros((hw, n_heads * hw), np.float32)
    for h in range(n_heads):
        o = h * hw + head_dim
        for part in range(3):
            sq[part * n_heads + h, o + part] = 1.0
            sq[one_col, o + 3 + part] = 1.0
            sk[one_col, o + part] = 1.0
            sk[part * n_heads + h, o + 3 + part] = -1.0
    return jnp.asarray(sq, BF16), jnp.asarray(sk, BF16)


def _odd_prompt_kernel(n_heads, head_dim, x_ref, wqkv_ref, wf_ref, whbc_ref, fb_ref, cw_ref, tri_ref,
                       sq_ref, sk_ref, k_ref, v_ref, lf_ref, qa_ref, ka_ref, va_ref, yd_ref, cd_ref,
                       gext, fcarry):
    t = pl.program_id(1)
    tT = x_ref.shape[0]
    cwid = n_heads * head_dim
    dwid = cw_ref.shape[-1]
    taps = cw_ref.shape[0]
    GH = CONV_D_HIST_ROWS
    hw = V7X_LANES

    @pl.when(t == 0)
    def _():
        gext[0:GH, :] = jnp.zeros((GH, dwid), F32)
        fcarry[...] = jnp.zeros(fcarry.shape, F32)

    @pl.when(t > 0)
    def _():
        gext[0:GH, :] = gext[tT:tT + GH, :]

    xb = x_ref[...].astype(BF16)
    qkv = _dot(xb, wqkv_ref[...])
    q = qkv[:, :cwid]
    k = qkv[:, cwid:2 * cwid]
    v = qkv[:, 2 * cwid:]
    k_ref[...] = k
    v_ref[...] = v

    lane = lax.broadcasted_iota(I32, (tT, hw), 1)
    fl = _dot(xb, wf_ref[...])
    logf = jnp.where(lane < n_heads, _log_sigmoid(fl + fb_ref[...]), 0.0)
    lf_ref[...] = logf[:, :n_heads]

    tri = tri_ref[...]
    l_hi, l_mid, l_lo = _split3(logf)
    F = _dot(tri, l_hi) + _dot(tri, l_mid) + _dot(tri, l_lo) + fcarry[...]
    fcarry[...] = F[tT - 1:tT, :]

    f_hi, f_mid, f_lo = _split3(F * LOG2E)
    fparts = (f_hi.astype(F32) + pltpu.roll(f_mid.astype(F32), n_heads, axis=1)
              + pltpu.roll(f_lo.astype(F32), 2 * n_heads, axis=1)
              + jnp.where(lane == 3 * n_heads, 1.0, 0.0)).astype(BF16)
    xq = _dot(fparts, sq_ref[...])
    xk = _dot(fparts, sk_ref[...])
    vone = jnp.where(lane == head_dim, 1.0, 0.0)
    scale = head_dim ** -0.5 * LOG2E
    for h in range(n_heads):
        p = (h * head_dim) // hw
        qp = q[:, p * hw:(p + 1) * hw] * scale
        kp = k[:, p * hw:(p + 1) * hw]
        vp = v[:, p * hw:(p + 1) * hw]
        if (h * head_dim) % hw:
            sh = hw - (h * head_dim) % hw
            qp = pltpu.roll(qp, sh, axis=1)
            kp = pltpu.roll(kp, sh, axis=1)
            vp = pltpu.roll(vp, sh, axis=1)
        qa_ref[h] = jnp.where(lane < head_dim, qp, xq[:, h * hw:(h + 1) * hw]).astype(BF16)
        ka_ref[h] = jnp.where(lane < head_dim, kp, xk[:, h * hw:(h + 1) * hw]).astype(BF16)
        va_ref[h] = jnp.where(lane < head_dim, vp, vone).astype(BF16)

    hbc = _dot(xb, whbc_ref[...])
    hh = hbc[:, :dwid]
    bg = hbc[:, dwid:2 * dwid]
    cg = hbc[:, 2 * dwid:]
    g = cg * hh
    gext[GH:, :] = g
    conv = g * cw_ref[taps - 1:taps, :]
    for kk in range(taps - 1):
        back = taps - 1 - kk
        conv = conv + gext[GH - back:GH - back + tT, :] * cw_ref[kk:kk + 1, :]
    yd_ref[...] = (bg * conv).astype(BF16)
    cd_ref[...] = gext[tT:tT + GH, :]


def _odd_prompt(n_heads, head_dim, x, wqkv, wf, whbc, fb, cw):
    B, T, D = x.shape
    tT = min(TOKEN_TILE, T)
    cwid = n_heads * head_dim
    dwid = cw.shape[-1]
    hw = V7X_LANES
    GH = CONV_D_HIST_ROWS
    tri = jnp.tri(tT, dtype=BF16)
    sq, sk = _head_select_mats(n_heads, head_dim)
    consts = (wqkv, wf, whbc, fb, cw, tri, sq, sk)
    tok = lambda w: pl.BlockSpec((None, tT, w), lambda bi, ti: (bi, ti, 0))
    head = pl.BlockSpec((None, n_heads, tT, hw), lambda bi, ti: (bi, 0, ti, 0))
    return pl.pallas_call(
        functools.partial(_odd_prompt_kernel, n_heads, head_dim),
        grid=(B, T // tT),
        in_specs=[tok(D)] + [_const_spec(c.shape) for c in consts],
        out_specs=[tok(cwid), tok(cwid), tok(n_heads), head, head, head, tok(dwid),
                   pl.BlockSpec((None, GH, dwid), lambda bi, ti: (bi, 0, 0))],
        out_shape=[jax.ShapeDtypeStruct((B, T, cwid), F32), jax.ShapeDtypeStruct((B, T, cwid), F32),
                   jax.ShapeDtypeStruct((B, T, n_heads), F32),
                   jax.ShapeDtypeStruct((B, n_heads, T, hw), BF16),
                   jax.ShapeDtypeStruct((B, n_heads, T, hw), BF16),
                   jax.ShapeDtypeStruct((B, n_heads, T, hw), BF16),
                   jax.ShapeDtypeStruct((B, T, dwid), BF16),
                   jax.ShapeDtypeStruct((B, GH, dwid), F32)],
        scratch_shapes=[pltpu.VMEM((GH + tT, dwid), F32), pltpu.VMEM((1, hw), F32)],
        compiler_params=_cparams(("arbitrary", "arbitrary")),
        name="odd_proj_prompt",
    )(x, *consts)


def _odd_sample_kernel(n_heads, head_dim, x_ref, sd_ref, wqkv_ref, wf_ref, whbc_ref, fb_ref, cw_ref,
                       q_ref, k_ref, v_ref, lf_ref, yd_ref, sdo_ref):
    cwid = n_heads * head_dim
    dwid = cw_ref.shape[-1]
    taps = cw_ref.shape[0]
    nh = sd_ref.shape[0]
    xb = x_ref[...].astype(BF16)
    qkv = _dot(xb, wqkv_ref[...])
    q_ref[...] = qkv[:, :cwid]
    k_ref[...] = qkv[:, cwid:2 * cwid]
    v_ref[...] = qkv[:, 2 * cwid:]
    fl = _dot(xb, wf_ref[...])
    lf_ref[...] = _log_sigmoid(fl + fb_ref[...])
    hbc = _dot(xb, whbc_ref[...])
    g = hbc[:, 2 * dwid:] * hbc[:, :dwid]
    conv = g * cw_ref[taps - 1:taps, :]
    for kk in range(taps - 1):
        conv = conv + sd_ref[kk + nh - (taps - 1)] * cw_ref[kk:kk + 1, :]
    yd_ref[...] = (hbc[:, dwid:2 * dwid] * conv).astype(BF16)
    for j in range(nh - 1):
        sdo_ref[j] = sd_ref[j + 1]
    sdo_ref[nh - 1] = g


def _odd_sample(n_heads, head_dim, x, sd_t, wqkv, wf, whbc, fb, cw):
    n = x.shape[0]
    cwid = n_heads * head_dim
    dwid = cw.shape[-1]
    ins = (x, sd_t, wqkv, wf, whbc, fb, cw)
    shapes = [((n, cwid), F32), ((n, cwid), F32), ((n, cwid), F32), ((n, V7X_LANES), F32),
              ((n, dwid), BF16), (sd_t.shape, F32)]
    return pl.pallas_call(
        functools.partial(_odd_sample_kernel, n_heads, head_dim),
        grid=(1,),
        in_specs=[_const_spec(c.shape) for c in ins],
        out_specs=[_const_spec(s) for s, _ in shapes],
        out_shape=[jax.ShapeDtypeStruct(s, d) for s, d in shapes],
        compiler_params=_cparams(("arbitrary",)),
        name="odd_proj_sample",
    )(*ins)


def _fox_prompt_kernel(head_dim, qa_ref, ka_ref, va_ref, o_ref, sa, sb):
    i = pl.program_id(2)
    tq = qa_ref.shape[1]
    hw = V7X_LANES
    lane = lax.broadcasted_iota(I32, (tq, hw), 1)
    row = lax.broadcasted_iota(I32, (tq, tq), 0)
    col = lax.broadcasted_iota(I32, (tq, tq), 1)
    nh = qa_ref.shape[0]
    qs = [qa_ref[hh] for hh in range(nh)]

    def scores(j, hh):
        return _dot_nt(qs[hh], ka_ref[hh, pl.ds(pl.multiple_of(j * tq, tq), tq), :])

    def update(j, hh, m, acc, s):
        m_new = jnp.maximum(m, jnp.max(s, axis=-1, keepdims=True))
        p = jnp.exp2(s - m_new)
        vt = va_ref[hh, pl.ds(pl.multiple_of(j * tq, tq), tq), :]
        return m_new, acc * jnp.exp2(m - m_new) + _dot(p.astype(BF16), vt)

    def step(j, carry, src, dst):
        new = []
        for hh in range(nh):
            dst[hh] = scores(j + 1, hh)
            new.append(update(j, hh, *carry[hh], src[hh]))
        return tuple(new)

    for hh in range(nh):
        sa[hh] = scores(0, hh)

    def pair(u, carry):
        return step(2 * u + 1, step(2 * u, carry, sa, sb), sb, sa)

    odd = i % 2
    carry = tuple((jnp.full((tq, 1), NEG, F32), jnp.zeros((tq, hw), F32)) for _ in range(nh))
    carry = lax.fori_loop(0, i // 2, pair, carry)
    carry = lax.fori_loop(0, odd, lambda _, c: step(i - 1, c, sa, sb), carry)
    def finish(src):
        o = None
        for hh in range(nh):
            _, acc = update(i, hh, *carry[hh], jnp.where(col <= row, src[hh], NEG))
            denom = jnp.sum(jnp.where(lane == head_dim, acc, 0.0), axis=-1, keepdims=True)
            out = acc / denom
            o = out if hh == 0 else jnp.where(lane < hh * head_dim, o, pltpu.roll(out, hh * head_dim, axis=1))
        o_ref[...] = o.astype(o_ref.dtype)

    @pl.when(odd == 0)
    def _():
        finish(sa)

    @pl.when(odd == 1)
    def _():
        finish(sb)


def _fox_prompt(head_dim, qa, ka, va):
    B, H, T, hw = qa.shape
    tq = min(TOKEN_TILE, T)
    hp = hw // head_dim
    return pl.pallas_call(
        functools.partial(_fox_prompt_kernel, head_dim),
        grid=(B, H // hp, T // tq),
        in_specs=[pl.BlockSpec((None, hp, tq, hw), lambda b, p, i: (b, p, i, 0)),
                  pl.BlockSpec((None, hp, T, hw), lambda b, p, i: (b, p, 0, 0)),
                  pl.BlockSpec((None, hp, T, hw), lambda b, p, i: (b, p, 0, 0))],
        out_specs=pl.BlockSpec((None, tq, hw), lambda b, p, i: (b, i, p)),
        out_shape=jax.ShapeDtypeStruct((B, T, H * head_dim), BF16),
        scratch_shapes=[pltpu.VMEM((hp, tq, tq), F32), pltpu.VMEM((hp, tq, tq), F32)],
        compiler_params=_cparams(("arbitrary", "arbitrary", "arbitrary")),
        name="fox_attention_prompt",
    )(qa, ka, va)


def _fox_sample_kernel(npp, pt_ref, *refs):
    k_refs = refs[0:npp]
    v_refs = refs[npp:2 * npp]
    lf_refs = refs[2 * npp:3 * npp]
    (qb_ref, kn_ref, vn_ref, lfn_ref, u_ref, o_ref, m_s, l_s, acc_s, carry_s) = refs[3 * npp:]
    i = pl.program_id(1)
    n_steps = pl.num_programs(1)
    H, dh, page = k_refs[0].shape
    cw = H * dh

    @pl.when(i == 0)
    def _():
        m_s[...] = jnp.full(m_s.shape, NEG, F32)
        l_s[...] = jnp.zeros(l_s.shape, F32)
        acc_s[...] = jnp.zeros(acc_s.shape, F32)
        carry_s[...] = lfn_ref[...]

    lf_all = jnp.concatenate([r[...] for r in lf_refs], axis=0)
    l_hi, l_mid, l_lo = _split3(lf_all)
    u = u_ref[...]
    g_all = _dot(l_hi, u) + _dot(l_mid, u) + _dot(l_lo, u)
    tot = jnp.sum(lf_all, axis=1, keepdims=True)
    later = carry_s[...]
    decay = [None] * npp
    for j in reversed(range(npp)):
        decay[j] = later
        later = later + tot[j * H:(j + 1) * H, :]
    carry_s[...] = later

    qb = qb_ref[...]
    qbb = qb.astype(BF16)
    s_pages = []
    for j in range(npp):
        k2 = k_refs[j][...].reshape(cw, page).astype(BF16)
        s_pages.append(_dot(qbb, k2) + g_all[j * H:(j + 1) * H, :] + decay[j])

    mx = s_pages[0]
    for j in range(1, npp):
        mx = jnp.maximum(mx, s_pages[j])
    m = m_s[...]
    m_new = jnp.maximum(m, jnp.max(mx, axis=1, keepdims=True))
    c = jnp.exp(m - m_new)
    p_pages = [jnp.exp(s - m_new) for s in s_pages]
    psum = p_pages[0]
    for j in range(1, npp):
        psum = psum + p_pages[j]
    l_s[...] = l_s[...] * c + psum
    m_s[...] = m_new
    acc = acc_s[...] * c
    for j in range(npp):
        v2 = v_refs[j][...].reshape(cw, page).astype(BF16)
        acc = acc + _dot_nt(p_pages[j].astype(BF16), v2)
    acc_s[...] = acc

    @pl.when(i == n_steps - 1)
    def _():
        s_new = jnp.sum(qb * kn_ref[...], axis=1, keepdims=True)
        m_fin = jnp.maximum(m_new, s_new)
        cf = jnp.exp(m_new - m_fin)
        p_new = jnp.exp(s_new - m_fin)
        denom = jnp.sum(l_s[...], axis=1, keepdims=True) * cf + p_new
        o_ref[...] = (acc * cf + p_new * vn_ref[...]) / denom


def _fox_sample(page_table, kview, vview, lfview, qb, kn, vn, lfn):
    nb, n_pages = page_table.shape
    _, H, dh, page = kview.shape
    npp = min(PAGES_PER_STEP, n_pages)
    assert n_pages % npp == 0
    n_steps = n_pages // npp
    u = jnp.asarray(np.tril(np.ones((page, page), np.float32), -1), BF16)

    def page_spec(shape, j):
        nd = len(shape)
        return pl.BlockSpec((None,) + tuple(shape),
                            lambda b, i, pt, j=j: (pt[b, n_pages - npp * (i + 1) + j],) + (0,) * nd)

    seq = lambda shape: pl.BlockSpec((None,) + tuple(shape), lambda b, i, pt: (b,) + (0,) * len(shape))
    in_specs = ([page_spec((H, dh, page), j) for j in range(npp)]
                + [page_spec((H, dh, page), j) for j in range(npp)]
                + [page_spec((H, page), j) for j in range(npp)]
                + [seq((H, H * dh)), seq((1, H * dh)), seq((1, H * dh)), seq((H, 1)),
                   pl.BlockSpec(u.shape, lambda b, i, pt: (0, 0))])
    grid_spec = pltpu.PrefetchScalarGridSpec(
        num_scalar_prefetch=1, grid=(nb, n_steps), in_specs=in_specs,
        out_specs=seq((H, H * dh)),
        scratch_shapes=[pltpu.VMEM((H, 1), F32), pltpu.VMEM((H, page), F32), pltpu.VMEM((H, H * dh), F32),
                        pltpu.VMEM((H, 1), F32)])
    return pl.pallas_call(
        functools.partial(_fox_sample_kernel, npp),
        grid_spec=grid_spec,
        out_shape=jax.ShapeDtypeStruct((nb, H, H * dh), F32),
        compiler_params=_cparams(("arbitrary", "arbitrary")),
        name="fox_attention_sample",
    )(page_table, *([kview] * npp), *([vview] * npp), *([lfview] * npp), qb, kn, vn, lfn, u)


def _top2(logits, axis, n_experts):
    idx = lax.broadcasted_iota(I32, logits.shape, axis)
    big = logits.shape[axis]
    lg = jnp.where(idx < n_experts, logits, -jnp.inf)
    v1 = jnp.max(lg, axis=axis, keepdims=True)
    i1 = jnp.min(jnp.where(lg == v1, idx, big), axis=axis, keepdims=True)
    lg2 = jnp.where(idx == i1, -jnp.inf, lg)
    v2 = jnp.max(lg2, axis=axis, keepdims=True)
    i2 = jnp.min(jnp.where(lg2 == v2, idx, big), axis=axis, keepdims=True)
    return idx, i1, i2, v1, v2


def _odd_out_kernel(alpha, n_experts, n_prompt_tiles, op_ref, ydp_ref, xp_ref, os_ref, yds_ref, xs_ref,
                    wout_ref, g_ref, b_ref, rw_ref, x3p_ref, x3s_ref, x3b_ref, lg_ref, cnt_ref):
    i = pl.program_id(0)
    cwid = op_ref.shape[1]

    def rows(o_ref, yd_ref, x_ref, x3_ref):
        n = o_ref.shape[0]
        mix = _dot(o_ref[...].astype(BF16), wout_ref[0:cwid, :]) + _dot(yd_ref[...], wout_ref[cwid:, :])
        x3 = _ln(alpha * x_ref[...] + mix, g_ref[...], b_ref[...])
        x3b = x3.astype(BF16)
        logits = _dot(x3b, rw_ref[...])
        x3_ref[...] = x3
        if n < x3b_ref.shape[0]:
            x3b_ref[...] = jnp.zeros(x3b_ref.shape, BF16)
            lg_ref[...] = jnp.zeros(lg_ref.shape, F32)
        x3b_ref[0:n, :] = x3b
        lg_ref[0:n, :] = logits
        idx, i1, i2, _, _ = _top2(logits, 1, n_experts)
        mask = jnp.where((idx == i1) | (idx == i2), 1.0, 0.0)
        cnt_ref[...] = jnp.sum(mask, axis=0, keepdims=True)

    @pl.when(i < n_prompt_tiles)
    def _():
        rows(op_ref, ydp_ref, xp_ref, x3p_ref)

    @pl.when(i >= n_prompt_tiles)
    def _():
        rows(os_ref, yds_ref, xs_ref, x3s_ref)


def _odd_out(alpha, n_experts, o_p, yd_p, x_p, o_s, yd_s, x_s, wout, g, b, rw):
    n, d = x_p.shape
    ns = x_s.shape[0]
    tm = TOKEN_TILE
    npt = n // tm
    ntiles = npt + 1
    hw = V7X_LANES
    prow = lambda w: pl.BlockSpec((tm, w), lambda i: (jnp.minimum(i, npt - 1), 0))
    return pl.pallas_call(
        functools.partial(_odd_out_kernel, alpha, n_experts, npt),
        grid=(ntiles,),
        in_specs=[prow(o_p.shape[1]), prow(yd_p.shape[1]), prow(d),
                  _const_spec(o_s.shape), _const_spec(yd_s.shape), _const_spec(x_s.shape)]
        + [_const_spec(c.shape) for c in (wout, g, b, rw)],
        out_specs=[prow(d), _const_spec((ns, d)),
                   pl.BlockSpec((tm, d), lambda i: (i, 0)),
                   pl.BlockSpec((tm, hw), lambda i: (i, 0)),
                   pl.BlockSpec((None, 1, hw), lambda i: (i, 0, 0))],
        out_shape=[jax.ShapeDtypeStruct((n, d), F32), jax.ShapeDtypeStruct((ns, d), F32),
                   jax.ShapeDtypeStruct((ntiles * tm, d), BF16),
                   jax.ShapeDtypeStruct((ntiles * tm, hw), F32),
                   jax.ShapeDtypeStruct((ntiles, 1, hw), F32)],
        compiler_params=_cparams(("arbitrary",)),
        name="odd_out_router",
    )(o_p, yd_p, x_p, o_s, yd_s, x_s, wout, g, b, rw)


def _segment_copies(tile, n_experts, npc_ref, loff_ref, goff_ref, make_copy):
    ratio = SEG_BIG // SEG_PAD
    for e in range(n_experts):
        n = npc_ref[tile * n_experts + e]
        lo = loff_ref[tile * n_experts + e]
        go = goff_ref[tile * n_experts + e]
        nbig = n // ratio

        def big(j, _, lo=lo, go=go):
            make_copy(pl.multiple_of(lo + j * SEG_BIG, SEG_PAD), pl.multiple_of(go + j * SEG_BIG, SEG_PAD), SEG_BIG)
            return 0

        def small(j, _, lo=lo, go=go, nbig=nbig):
            off = nbig * SEG_BIG + j * SEG_PAD
            make_copy(pl.multiple_of(lo + off, SEG_PAD), pl.multiple_of(go + off, SEG_PAD), SEG_PAD)
            return 0

        lax.fori_loop(0, nbig, big, 0)
        lax.fori_loop(0, n - nbig * ratio, small, 0)


def _dispatch_kernel(n_experts, n_valid, npc_ref, loff_ref, goff_ref, x_ref, lt_ref, lofft_ref, triu_ref,
                     xs_in_ref, xs_ref, stage, sem):
    del xs_in_ref
    tile = pl.program_id(0)
    tt = x_ref.shape[0]
    rl = stage.shape[0]
    sub, i1, i2, _, _ = _top2(lt_ref[...], 0, n_experts)
    valid = (tile * tt + lax.broadcasted_iota(I32, (1, tt), 1)) < n_valid
    sel1 = (sub == i1) & valid
    sel2 = (sub == i2) & valid
    mask = jnp.where(sel1 | sel2, 1.0, 0.0)
    rank = _dot(mask.astype(BF16), triu_ref[...])
    loc = rank + jnp.concatenate([lofft_ref[...]] * (tt // V7X_LANES), axis=1)
    lr1 = jnp.sum(jnp.where(sel1, loc, 0.0), axis=0, keepdims=True).astype(I32)
    lr2 = jnp.sum(jnp.where(sel2, loc, 0.0), axis=0, keepdims=True).astype(I32)
    lr1 = jnp.where(valid, lr1, -1)
    lr2 = jnp.where(valid, lr2, -1)
    r = lax.broadcasted_iota(I32, (rl, tt), 0)
    onehot = jnp.where((r == lr1) | (r == lr2), 1.0, 0.0).astype(BF16)
    stage[...] = _dot(onehot, x_ref[...]).astype(BF16)

    def copy(lo, go, nrows):
        return pltpu.make_async_copy(stage.at[pl.ds(lo, nrows), :], xs_ref.at[pl.ds(go, nrows), :], sem)

    _segment_copies(tile, n_experts, npc_ref, loff_ref, goff_ref, lambda lo, go, n: copy(lo, go, n).start())
    _segment_copies(tile, n_experts, npc_ref, loff_ref, goff_ref, lambda lo, go, n: copy(lo, go, n).wait())


def _dispatch(n_experts, n_valid, npc, loff, goff, x3b, lt, lofft, xs_zero):
    ntot, d = x3b.shape
    tt = TOKEN_TILE
    triu = jnp.asarray(np.triu(np.ones((tt, tt), np.float32), 1), BF16)
    grid_spec = pltpu.PrefetchScalarGridSpec(
        num_scalar_prefetch=3, grid=(ntot // tt,),
        in_specs=[pl.BlockSpec((tt, d), lambda i, *_: (i, 0)),
                  pl.BlockSpec((n_experts, tt), lambda i, *_: (0, i)),
                  pl.BlockSpec((None, n_experts, V7X_LANES), lambda i, *_: (i, 0, 0)),
                  pl.BlockSpec(triu.shape, lambda i, *_: (0, 0)),
                  pl.BlockSpec(memory_space=pl.ANY)],
        out_specs=pl.BlockSpec(memory_space=pl.ANY),
        scratch_shapes=[pltpu.VMEM((LOCAL_ROWS, d), BF16), pltpu.SemaphoreType.DMA(())])
    return pl.pallas_call(
        functools.partial(_dispatch_kernel, n_experts, n_valid),
        grid_spec=grid_spec,
        out_shape=jax.ShapeDtypeStruct(xs_zero.shape, BF16),
        input_output_aliases={7: 0},
        compiler_params=_cparams(("arbitrary",)),
        name="moe_dispatch",
    )(npc, loff, goff, x3b, lt, lofft, triu, xs_zero)


def _experts_kernel(te_ref, tv_ref, xr_ref, x_ref, w1_ref, w3_ref, w2_ref, yh_ref, yl_ref, h_ref):
    del te_ref, xr_ref
    i = pl.program_id(0)
    c = pl.program_id(1)
    last = pl.num_programs(1) - 1
    fc = w1_ref.shape[1]
    valid = tv_ref[i] > 0

    @pl.when(valid)
    def _():
        x = x_ref[...]
        h = (_silu(_dot(x, w1_ref[...])) * _dot(x, w3_ref[...])).astype(BF16)
        h_ref[:, pl.ds(pl.multiple_of(c * fc, fc), fc)] = h

        @pl.when(c == last)
        def _():
            y = _dot(h_ref[...], w2_ref[...])
            hi = y.astype(BF16)
            yh_ref[...] = hi
            yl_ref[...] = (y - hi.astype(F32)).astype(BF16)

    @pl.when(jnp.logical_not(valid) & (c == last))
    def _():
        yh_ref[...] = jnp.zeros(yh_ref.shape, BF16)
        yl_ref[...] = jnp.zeros(yl_ref.shape, BF16)


def _experts(te, tv, xr, xs, w1, w3, w2):
    rmax, d = xs.shape
    tm = EXPERT_TILE
    ff = w1.shape[-1]
    fc = min(EXPERT_FF_CHUNK, ff)
    nch = ff // fc
    assert ff % fc == 0 and rmax % tm == 0

    def chunk(i, c, tv):
        return jnp.where(tv[i] > 0, c, nch - 1)

    grid_spec = pltpu.PrefetchScalarGridSpec(
        num_scalar_prefetch=3, grid=(rmax // tm, nch),
        in_specs=[pl.BlockSpec((tm, d), lambda i, c, te, tv, xr: (xr[i], 0)),
                  pl.BlockSpec((None, d, fc), lambda i, c, te, tv, xr: (te[i], 0, chunk(i, c, tv))),
                  pl.BlockSpec((None, d, fc), lambda i, c, te, tv, xr: (te[i], 0, chunk(i, c, tv))),
                  pl.BlockSpec((None, ff, d), lambda i, c, te, tv, xr: (te[i], 0, 0))],
        out_specs=[pl.BlockSpec((tm, d), lambda i, c, te, tv, xr: (i, 0)),
                   pl.BlockSpec((tm, d), lambda i, c, te, tv, xr: (i, 0))],
        scratch_shapes=[pltpu.VMEM((tm, ff), BF16)])
    return pl.pallas_call(
        _experts_kernel,
        grid_spec=grid_spec,
        out_shape=[jax.ShapeDtypeStruct((rmax, d), BF16), jax.ShapeDtypeStruct((rmax, d), BF16)],
        compiler_params=_cparams(("arbitrary", "arbitrary")),
        name="moe_experts",
    )(te, tv, xr, xs, w1, w3, w2)


def _combine_kernel(alpha, n_experts, n_valid, n_prompt_tiles, npc_ref, loff_ref, goff_ref, lg_ref, xp_ref,
                    xs_ref, loffr_ref, tril_ref, g_ref, b_ref, yh_hbm, yl_hbm, yp_ref, ysm_ref,
                    sth, stl, sem):
    tile = pl.program_id(0)
    tt = lg_ref.shape[0]
    rl = sth.shape[0]
    d = sth.shape[1]

    def copies(fn):
        _segment_copies(tile, n_experts, npc_ref, loff_ref, goff_ref,
                        lambda lo, go, n: fn(pltpu.make_async_copy(yh_hbm.at[pl.ds(go, n), :],
                                                                  sth.at[pl.ds(lo, n), :], sem.at[0])))
        _segment_copies(tile, n_experts, npc_ref, loff_ref, goff_ref,
                        lambda lo, go, n: fn(pltpu.make_async_copy(yl_hbm.at[pl.ds(go, n), :],
                                                                  stl.at[pl.ds(lo, n), :], sem.at[1])))

    copies(lambda cp: cp.start())

    last = tile * n_experts + n_experts - 1
    used = loff_ref[last] // SEG_PAD + npc_ref[last]

    def clear(j, _):
        o = pl.multiple_of(j * SEG_PAD, SEG_PAD)
        sth[pl.ds(o, SEG_PAD), :] = jnp.zeros((SEG_PAD, d), BF16)
        stl[pl.ds(o, SEG_PAD), :] = jnp.zeros((SEG_PAD, d), BF16)
        return 0

    lax.fori_loop(used, rl // SEG_PAD, clear, 0)

    lane, i1, i2, v1, v2 = _top2(lg_ref[...], 1, n_experts)
    valid = (tile * tt + lax.broadcasted_iota(I32, (tt, 1), 0)) < n_valid
    sel1 = (lane == i1) & valid
    sel2 = (lane == i2) & valid
    mask = jnp.where(sel1 | sel2, 1.0, 0.0)
    rank = _dot(tril_ref[...], mask.astype(BF16))
    loc = rank + jnp.concatenate([loffr_ref[...]] * (tt // V7X_SUBLANES), axis=0)
    lr1 = jnp.sum(jnp.where(sel1, loc, 0.0), axis=1, keepdims=True).astype(I32)
    lr2 = jnp.sum(jnp.where(sel2, loc, 0.0), axis=1, keepdims=True).astype(I32)
    lr1 = jnp.where(valid, lr1, -1)
    lr2 = jnp.where(valid, lr2, -1)
    e21 = jnp.exp(v2 - v1)
    g1 = 1.0 / (1.0 + e21)
    g2 = e21 / (1.0 + e21)
    r = lax.broadcasted_iota(I32, (tt, rl), 1)
    gm = jnp.where(r == lr1, g1, 0.0) + jnp.where(r == lr2, g2, 0.0)
    gm_hi = gm.astype(BF16)
    gm_lo = (gm - gm_hi.astype(F32)).astype(BF16)

    copies(lambda cp: cp.wait())
    yh = sth[...]
    moe = _dot(gm_hi, yh) + _dot(gm_hi, stl[...]) + _dot(gm_lo, yh)

    @pl.when(tile < n_prompt_tiles)
    def _():
        yp_ref[...] = _ln(alpha * xp_ref[...] + moe, g_ref[...], b_ref[...])

    @pl.when(tile >= n_prompt_tiles)
    def _():
        ns = ysm_ref.shape[0]
        ysm_ref[...] = _ln(alpha * xs_ref[...] + moe[0:ns, :], g_ref[...], b_ref[...])


def _combine(alpha, n_experts, n_valid, npc, loff, goff, logits, x3p, x3s, loffr, g, b, yh, yl):
    npr, d = x3p.shape
    ns = x3s.shape[0]
    tt = TOKEN_TILE
    npt = npr // tt
    ntiles = logits.shape[0] // tt
    tril = jnp.asarray(np.tril(np.ones((tt, tt), np.float32), -1), BF16)
    grid_spec = pltpu.PrefetchScalarGridSpec(
        num_scalar_prefetch=3, grid=(ntiles,),
        in_specs=[pl.BlockSpec((tt, V7X_LANES), lambda i, *_: (i, 0)),
                  pl.BlockSpec((tt, d), lambda i, *_: (jnp.minimum(i, npt - 1), 0)),
                  pl.BlockSpec((ns, d), lambda i, *_: (0, 0)),
                  pl.BlockSpec((None, V7X_SUBLANES, V7X_LANES), lambda i, *_: (i, 0, 0)),
                  pl.BlockSpec(tril.shape, lambda i, *_: (0, 0)),
                  pl.BlockSpec(g.shape, lambda i, *_: (0, 0)),
                  pl.BlockSpec(b.shape, lambda i, *_: (0, 0)),
                  pl.BlockSpec(memory_space=pl.ANY),
                  pl.BlockSpec(memory_space=pl.ANY)],
        out_specs=[pl.BlockSpec((tt, d), lambda i, *_: (jnp.minimum(i, npt - 1), 0)),
                   pl.BlockSpec((ns, d), lambda i, *_: (0, 0))],
        scratch_shapes=[pltpu.VMEM((LOCAL_ROWS, d), BF16), pltpu.VMEM((LOCAL_ROWS, d), BF16),
                        pltpu.SemaphoreType.DMA((2,))])
    return pl.pallas_call(
        functools.partial(_combine_kernel, alpha, n_experts, n_valid, npt),
        grid_spec=grid_spec,
        out_shape=[jax.ShapeDtypeStruct((npr, d), F32), jax.ShapeDtypeStruct((ns, d), F32)],
        compiler_params=_cparams(("arbitrary",)),
        name="moe_combine",
    )(npc, loff, goff, logits, x3p, x3s, loffr, tril, g, b, yh, yl)


def _routing_plan(cnt, n_experts):
    ntiles = cnt.shape[0]
    pc = (cnt + SEG_PAD - 1) // SEG_PAD * SEG_PAD
    loff = jnp.cumsum(pc, axis=1) - pc
    per_expert = jnp.sum(pc, axis=0)
    gp = (per_expert + EXPERT_TILE - 1) // EXPERT_TILE * EXPERT_TILE
    gend = jnp.cumsum(gp)
    goff = (gend - gp)[None, :] + jnp.cumsum(pc, axis=0) - pc
    rmax = TOP_K * ntiles * TOKEN_TILE + ntiles * n_experts * (SEG_PAD - 1) + n_experts * (EXPERT_TILE - 1)
    nt_max = -(-rmax // EXPERT_TILE)
    tiles_used = gend[-1] // EXPERT_TILE
    ti = jnp.arange(nt_max, dtype=I32)
    tv = (ti < tiles_used).astype(I32)
    xr = jnp.maximum(jnp.minimum(ti, tiles_used - 1), 0)
    te = jnp.sum((xr[:, None] >= (gend // EXPERT_TILE)[None, :]).astype(I32), axis=1)
    te = jnp.minimum(te, n_experts - 1)
    flat = lambda a: a.reshape(-1).astype(I32)
    return (flat(pc // SEG_PAD), flat(loff), flat(goff), loff.astype(F32), te, tv, xr.astype(I32),
            nt_max * EXPERT_TILE)


def kernel(x_prompt, x_sample, state_pool, state_conv_b, cache_k, cache_v, cache_logf, state_conv_d, page_table,
           w_in_even, pool_w, pool_scale, conv_b_w, conv_b_bias, conv_ln_g, conv_ln_b, w_out_even, ln_mix_even_g,
           ln_mix_even_b, ffn_w1, ffn_w3, ffn_w2, ln_ffn_even_g, ln_ffn_even_b, w_in_odd, forget_bias, conv_d_w,
           w_out_odd, ln_mix_odd_g, ln_mix_odd_b, router_w, moe_w1, moe_w3, moe_w2, ln_ffn_odd_g, ln_ffn_odd_b):
    assert w_in_even.shape[0] == 1 and w_in_odd.shape[0] == 1, "one even and one odd layer are supported"
    depth = w_in_even.shape[0] + w_in_odd.shape[0]
    alpha = float((2 * depth) ** 0.25)
    B, T, D = x_prompt.shape
    nb = x_sample.shape[0]
    assert x_sample.shape[1] == 1 and T % TOKEN_TILE == 0 and nb <= TOKEN_TILE
    n_heads = forget_bias.shape[-1]
    cwid = cache_k.shape[-1] * cache_k.shape[-2]
    head_dim = cache_k.shape[-1]
    dwid = conv_d_w.shape[-1]
    n_experts = router_w.shape[-1]
    past_len = page_table.shape[1] * cache_k.shape[2]
    hw = V7X_LANES
    bf = lambda w: w.astype(BF16)
    rowv = lambda w: w.reshape(1, -1)

    ew = (bf(w_in_even[0]), bf(pool_w[0]), pool_scale, conv_b_w[0], conv_b_bias, conv_ln_g, conv_ln_b,
          bf(w_out_even[0]), ln_mix_even_g, ln_mix_even_b)
    xp, a_hist, u_hist = _even_prompt(alpha, x_prompt, *ew)
    xs, pool_s_t, convb_s_t = _even_sample(
        alpha, past_len, x_sample.reshape(nb, D), jnp.swapaxes(state_pool[0], 0, 1),
        jnp.swapaxes(state_conv_b[0], 0, 1), *ew)
    nph = state_pool.shape[2]
    nch = state_conv_b.shape[2]
    pool_p = a_hist[None, :, POOL_HIST_ROWS - nph:, :]
    convb_p = u_hist[None, :, CONV_B_HIST_ROWS - nch:, :]
    pool_s = jnp.swapaxes(pool_s_t, 0, 1)[None]
    convb_s = jnp.swapaxes(convb_s_t, 0, 1)[None]

    fw = (bf(ffn_w1[0]), bf(ffn_w3[0]), bf(ffn_w2[0]), ln_ffn_even_g, ln_ffn_even_b)
    xp = _ffn(alpha, xp.reshape(B * T, D), *fw)
    xs = _ffn(alpha, xs, *fw)

    w_in = w_in_odd[0]
    wqkv = bf(w_in[:, :3 * cwid])
    wf = bf(jnp.pad(w_in[:, 3 * cwid:3 * cwid + n_heads], ((0, 0), (0, hw - n_heads))))
    whbc = bf(w_in[:, 3 * cwid + n_heads:])
    fb = jnp.pad(forget_bias, ((0, 0), (0, hw - n_heads)))
    k_p, v_p, lf_p, qa, ka, va, yd_p, cd_p = _odd_prompt(
        n_heads, head_dim, xp.reshape(B, T, D), wqkv, wf, whbc, fb, conv_d_w[0])
    o_p = _fox_prompt(head_dim, qa, ka, va)

    q_s, k_s, v_s, lf_s, yd_s, convd_s_t = _odd_sample(
        n_heads, head_dim, xs, jnp.swapaxes(state_conv_d[0], 0, 1), wqkv, wf, whbc, fb, conv_d_w[0])
    head_of_col = jnp.arange(cwid, dtype=I32) // head_dim
    blockdiag = (head_of_col[None, :] == jnp.arange(n_heads, dtype=I32)[:, None]).astype(F32)
    lf_s = lf_s[:, :n_heads]
    o_s_all = _fox_sample(page_table, jnp.transpose(cache_k[0], (0, 2, 3, 1)),
                          jnp.transpose(cache_v[0], (0, 2, 3, 1)), jnp.transpose(cache_logf[0], (0, 2, 1)),
                          (q_s * head_dim ** -0.5)[:, None, :] * blockdiag[None], k_s[:, None, :], v_s[:, None, :],
                          lf_s.reshape(nb, n_heads, 1))
    o_s = jnp.sum(o_s_all * blockdiag[None], axis=1)

    n_prompt = B * T
    npt = n_prompt // TOKEN_TILE
    ntiles = npt + 1
    n_valid = n_prompt + nb
    ow = (bf(w_out_odd[0]), ln_mix_odd_g, ln_mix_odd_b,
          bf(jnp.pad(router_w[0], ((0, 0), (0, hw - n_experts)))))
    x3p, x3s, x3b, logits, cnt = _odd_out(alpha, n_experts, o_p.reshape(n_prompt, cwid),
                                          yd_p.reshape(n_prompt, dwid), xp, o_s, yd_s, xs, *ow)

    cnt = cnt[:, 0, :n_experts].astype(I32)
    npc, loff, goff, loff_f, te, tv, xr, rmax = _routing_plan(cnt, n_experts)
    lt = jnp.transpose(logits[:, :n_experts])
    lofft = jnp.broadcast_to(loff_f[:, :, None], (ntiles, n_experts, hw))
    loffr = jnp.broadcast_to(jnp.pad(loff_f, ((0, 0), (0, hw - n_experts)))[:, None, :],
                             (ntiles, V7X_SUBLANES, hw))
    xs_sorted = _dispatch(n_experts, n_valid, npc, loff, goff, x3b, lt, lofft, jnp.zeros((rmax, D), BF16))
    yh, yl = _experts(te, tv, xr, xs_sorted, bf(moe_w1[0]), bf(moe_w3[0]), bf(moe_w2[0]))
    y_p, y_s = _combine(alpha, n_experts, n_valid, npc, loff, goff, logits, x3p, x3s, loffr,
                        ln_ffn_odd_g, ln_ffn_odd_b, yh, yl)

    nd = state_conv_d.shape[2]
    return (y_p.reshape(B, T, D), y_s.reshape(nb, 1, D),
            pool_p, pool_s, convb_p, convb_s,
            k_p.reshape(1, B, T, n_heads, head_dim), k_s.reshape(1, nb, 1, n_heads, head_dim),
            v_p.reshape(1, B, T, n_heads, head_dim), v_s.reshape(1, nb, 1, n_heads, head_dim),
            lf_p.reshape(1, B, T, n_heads), lf_s.reshape(1, nb, 1, n_heads),
            cd_p[None, :, CONV_D_HIST_ROWS - nd:, :], jnp.swapaxes(convd_s_t, 0, 1)[None])
```

```python
import functools

import numpy as np
import jax
import jax.numpy as jnp
from jax import lax
from jax.experimental import pallas as pl
from jax.experimental.pallas import tpu as pltpu

F32 = jnp.float32
BF16 = jnp.bfloat16
I32 = jnp.int32

LN_EPS = 1e-5
POOL_WINDOWS = (2, 4, 8, 16)
TOP_K = 2
NEG = -1e30
LOG2E = 1.4426950408889634

V7X_VMEM_BYTES = 64 * 1024 * 1024
V7X_LANES = 128
V7X_SUBLANES = 8
BF16_ROWS_PER_TILE = 2 * V7X_SUBLANES

VMEM_LIMIT = V7X_VMEM_BYTES - 8 * 1024 * 1024

TOKEN_TILE = 512
EXPERT_TILE = 512
EXPERT_FF_CHUNK = 1792
FFN_CHUNK = 256
CONV_ROWS = 64
POOL_HIST_ROWS = 16
CONV_B_HIST_ROWS = 32
CONV_D_HIST_ROWS = 8
PAGES_PER_STEP = 16
SEG_PAD = BF16_ROWS_PER_TILE
SEG_BIG = 128
LOCAL_ROWS = TOP_K * TOKEN_TILE + 8 * SEG_PAD


def _cparams(sem):
    return pltpu.CompilerParams(dimension_semantics=sem, vmem_limit_bytes=VMEM_LIMIT)


def _dot(a, b):
    return jnp.dot(a, b, preferred_element_type=F32)


def _dot_nt(a, b):
    return lax.dot_general(a, b, (((1,), (1,)), ((), ())), preferred_element_type=F32)


def _dot_tn(a, b):
    return lax.dot_general(a, b, (((0,), (0,)), ((), ())), preferred_element_type=F32)


def _ln(z, g, b):
    mu = jnp.mean(z, axis=-1, keepdims=True)
    d = z - mu
    var = jnp.mean(d * d, axis=-1, keepdims=True)
    return d * lax.rsqrt(var + LN_EPS) * g + b


def _silu(x):
    return x * jax.nn.sigmoid(x)


def _log_sigmoid(z):
    return jnp.minimum(z, 0.0) - jnp.log1p(jnp.exp(-jnp.abs(z)))


def _split3(x):
    hi = x.astype(BF16)
    r = x - hi.astype(F32)
    mid = r.astype(BF16)
    lo = (r - mid.astype(F32)).astype(BF16)
    return hi, mid, lo


def _const_spec(shape):
    nd = len(shape)
    return pl.BlockSpec(shape, lambda *_: (0,) * nd)


def _resident_spec(shape):
    nd = len(shape)
    return pl.BlockSpec(shape, lambda *_: (0,) * nd, pipeline_mode=pl.Buffered(1))


def _pool_groups(a, hist_fn, pos, pw_ref):
    outs = []
    gw = a.shape[1] // len(POOL_WINDOWS)
    for g, w in enumerate(POOL_WINDOWS):
        c0 = g * gw
        cur = a[:, c0:c0 + gw]
        win = cur
        for k in range(1, w):
            win = win + hist_fn(k, c0, gw)
        cnt = jnp.minimum(pos + 1, w).astype(F32)
        pooled = win / cnt - cur
        outs.append(_dot(pooled.astype(BF16), pw_ref[g]))
    return jnp.concatenate(outs, axis=-1)


def _even_prompt_kernel(alpha, x_ref, win_ref, pw_ref, ps_ref, cw_ref, cb_ref, cg_ref, cbeta_ref,
                        wout_ref, g_ref, b_ref, y_ref, ah_ref, uh_ref, aext, uext, ush, ybuf):
    t = pl.program_id(1)
    tT = x_ref.shape[0]
    aw = ps_ref.shape[-1]
    bw = cw_ref.shape[-1]
    taps = cw_ref.shape[0]
    AH, UH = POOL_HIST_ROWS, CONV_B_HIST_ROWS

    @pl.when(t == 0)
    def _():
        aext[0:AH, :] = jnp.zeros((AH, aw), F32)
        uext[0:UH, :] = jnp.zeros((UH, bw), F32)

    @pl.when(t > 0)
    def _():
        aext[0:AH, :] = aext[tT:tT + AH, :]
        uext[0:UH, :] = uext[tT:tT + UH, :]

    x = x_ref[...]
    proj = _dot(x.astype(BF16), win_ref[...])
    a = proj[:, :aw]
    u = proj[:, aw:aw + bw] * jax.nn.sigmoid(proj[:, aw + bw:])
    aext[AH:, :] = a
    uext[UH:, :] = u

    pos = t * tT + lax.broadcasted_iota(I32, (tT, 1), 0)
    ya = _pool_groups(a, lambda k, c0, gw: aext[AH - k:AH - k + tT, c0:c0 + gw], pos, pw_ref)
    ya = ya * ps_ref[...]

    base = UH - (taps - 1)
    ns = V7X_SUBLANES
    span = UH + tT - ns
    for s in range(1, ns):
        ush[s - 1, 0:span, :] = uext[s:s + span, :]

    def tap(k, r0):
        off = base + k
        a, s = off - off % ns + r0, off % ns
        rows = uext[a:a + CONV_ROWS, :] if s == 0 else ush[s - 1, a:a + CONV_ROWS, :]
        return rows * cw_ref[k:k + 1, :]

    for r0 in range(0, tT, CONV_ROWS):
        acc = tap(0, r0)
        for k in range(1, taps):
            acc = acc + tap(k, r0)
        yb = _silu(_ln(acc + cb_ref[...], cg_ref[...], cbeta_ref[...]))
        ybuf[r0:r0 + CONV_ROWS, :] = yb.astype(BF16)

    mix = _dot(ya.astype(BF16), wout_ref[0:aw, :]) + _dot(ybuf[...], wout_ref[aw:, :])
    y_ref[...] = _ln(alpha * x + mix, g_ref[...], b_ref[...])
    ah_ref[...] = aext[tT:tT + AH, :]
    uh_ref[...] = uext[tT:tT + UH, :]


def _even_prompt(alpha, x, win, pw, ps, cw, cb, cg, cbeta, wout, g, b):
    B, T, D = x.shape
    tT = min(TOKEN_TILE, T)
    aw, bw = ps.shape[-1], cw.shape[-1]
    AH, UH = POOL_HIST_ROWS, CONV_B_HIST_ROWS
    consts = (win, pw, ps, cw, cb, cg, cbeta, wout, g, b)
    return pl.pallas_call(
        functools.partial(_even_prompt_kernel, alpha),
        grid=(B, T // tT),
        in_specs=[pl.BlockSpec((None, tT, D), lambda bi, ti: (bi, ti, 0))]
        + [_const_spec(c.shape) for c in consts],
        out_specs=[pl.BlockSpec((None, tT, D), lambda bi, ti: (bi, ti, 0)),
                   pl.BlockSpec((None, AH, aw), lambda bi, ti: (bi, 0, 0)),
                   pl.BlockSpec((None, UH, bw), lambda bi, ti: (bi, 0, 0))],
        out_shape=[jax.ShapeDtypeStruct((B, T, D), F32),
                   jax.ShapeDtypeStruct((B, AH, aw), F32),
                   jax.ShapeDtypeStruct((B, UH, bw), F32)],
        scratch_shapes=[pltpu.VMEM((AH + tT, aw), F32), pltpu.VMEM((UH + tT, bw), F32),
                        pltpu.VMEM((V7X_SUBLANES - 1, UH + tT, bw), F32), pltpu.VMEM((tT, bw), BF16)],
        compiler_params=_cparams(("arbitrary", "arbitrary")),
        name="even_mixer_prompt",
    )(x, *consts)


def _even_sample_kernel(alpha, first_pos, x_ref, sp_ref, sc_ref, win_ref, pw_ref, ps_ref, cw_ref, cb_ref,
                        cg_ref, cbeta_ref, wout_ref, g_ref, b_ref, y_ref, spo_ref, sco_ref):
    aw = ps_ref.shape[-1]
    bw = cw_ref.shape[-1]
    taps = cw_ref.shape[0]
    nph = sp_ref.shape[0]
    nch = sc_ref.shape[0]
    x = x_ref[...]
    proj = _dot(x.astype(BF16), win_ref[...])
    a = proj[:, :aw]
    u = proj[:, aw:aw + bw] * jax.nn.sigmoid(proj[:, aw + bw:])

    pos = jnp.full((x.shape[0], 1), first_pos, I32)
    ya = _pool_groups(a, lambda k, c0, gw: sp_ref[nph - k, :, c0:c0 + gw], pos, pw_ref)
    ya = ya * ps_ref[...]

    acc = u * cw_ref[taps - 1:taps, :]
    for k in range(taps - 1):
        acc = acc + sc_ref[k + nch - (taps - 1)] * cw_ref[k:k + 1, :]
    yb = _silu(_ln(acc + cb_ref[...], cg_ref[...], cbeta_ref[...]))

    mix = _dot(ya.astype(BF16), wout_ref[0:aw, :]) + _dot(yb.astype(BF16), wout_ref[aw:, :])
    y_ref[...] = _ln(alpha * x + mix, g_ref[...], b_ref[...])
    for j in range(nph - 1):
        spo_ref[j] = sp_ref[j + 1]
    spo_ref[nph - 1] = a
    for j in range(nch - 1):
        sco_ref[j] = sc_ref[j + 1]
    sco_ref[nch - 1] = u


def _even_sample(alpha, first_pos, x, sp_t, sc_t, win, pw, ps, cw, cb, cg, cbeta, wout, g, b):
    ins = (x, sp_t, sc_t, win, pw, ps, cw, cb, cg, cbeta, wout, g, b)
    return pl.pallas_call(
        functools.partial(_even_sample_kernel, alpha, first_pos),
        grid=(1,),
        in_specs=[_const_spec(c.shape) for c in ins],
        out_specs=[_const_spec(x.shape), _const_spec(sp_t.shape), _const_spec(sc_t.shape)],
        out_shape=[jax.ShapeDtypeStruct(x.shape, F32), jax.ShapeDtypeStruct(sp_t.shape, F32),
                   jax.ShapeDtypeStruct(sc_t.shape, F32)],
        compiler_params=_cparams(("arbitrary",)),
        name="even_mixer_sample",
    )(*ins)


def _ffn_kernel(alpha, x_ref, w1_ref, w3_ref, w2_ref, g_ref, b_ref, y_ref, h_ref):
    x = x_ref[...]
    xb = x.astype(BF16)
    ff = w1_ref.shape[1]
    for c in range(0, ff, FFN_CHUNK):
        h1 = _dot(xb, w1_ref[:, c:c + FFN_CHUNK])
        h3 = _dot(xb, w3_ref[:, c:c + FFN_CHUNK])
        h_ref[:, c:c + FFN_CHUNK] = (_silu(h1) * h3).astype(BF16)
    y = _dot(h_ref[...], w2_ref[...])
    y_ref[...] = _ln(alpha * x + y, g_ref[...], b_ref[...])


def _ffn(alpha, x, w1, w3, w2, g, b):
    n, d = x.shape
    tm = min(TOKEN_TILE, n)
    ff = w1.shape[1]
    assert ff % FFN_CHUNK == 0 and n % tm == 0
    return pl.pallas_call(
        functools.partial(_ffn_kernel, alpha),
        grid=(n // tm,),
        in_specs=[pl.BlockSpec((tm, d), lambda i: (i, 0)),
                  _resident_spec(w1.shape), _resident_spec(w3.shape), _resident_spec(w2.shape),
                  _const_spec(g.shape), _const_spec(b.shape)],
        out_specs=pl.BlockSpec((tm, d), lambda i: (i, 0)),
        out_shape=jax.ShapeDtypeStruct((n, d), F32),
        scratch_shapes=[pltpu.VMEM((tm, ff), BF16)],
        compiler_params=_cparams(("arbitrary",)),
        name="ffn_swiglu",
    )(x, w1, w3, w2, g, b)


def _head_select_mats(n_heads, head_dim):
    hw = V7X_LANES
    one_col = 3 * n_heads
    sq = np.zeros((hw, n_heads * hw), np.float32)
    sk = np.zeros((hw, n_heads * hw), np.float32)
    for h in range(n_heads):
        o = h * hw + head_dim
        for part in range(3):
            sq[part * n_heads + h, o + part] = 1.0
            sq[one_col, o + 3 + part] = 1.0
            sk[one_col, o + part] = 1.0
            sk[part * n_heads + h, o + 3 + part] = -1.0
    return jnp.asarray(sq, BF16), jnp.asarray(sk, BF16)


def _odd_prompt_kernel(n_heads, head_dim, x_ref, wqkv_ref, wf_ref, whbc_ref, fb_ref, cw_ref, tri_ref,
                       sq_ref, sk_ref, k_ref, v_ref, lf_ref, qa_ref, ka_ref, va_ref, yd_ref, cd_ref,
                       gext, fcarry):
    t = pl.program_id(1)
    tT = x_ref.shape[0]
    cwid = n_heads * head_dim
    dwid = cw_ref.shape[-1]
    taps = cw_ref.shape[0]
    GH = CONV_D_HIST_ROWS
    hw = V7X_LANES

    @pl.when(t == 0)
    def _():
        gext[0:GH, :] = jnp.zeros((GH, dwid), F32)
        fcarry[...] = jnp.zeros(fcarry.shape, F32)

    @pl.when(t > 0)
    def _():
        gext[0:GH, :] = gext[tT:tT + GH, :]

    xb = x_ref[...].astype(BF16)
    qkv = _dot(xb, wqkv_ref[...])
    q = qkv[:, :cwid]
    k = qkv[:, cwid:2 * cwid]
    v = qkv[:, 2 * cwid:]
    k_ref[...] = k
    v_ref[...] = v

    lane = lax.broadcasted_iota(I32, (tT, hw), 1)
    fl = _dot(xb, wf_ref[...])
    logf = jnp.where(lane < n_heads, _log_sigmoid(fl + fb_ref[...]), 0.0)
    lf_ref[...] = logf[:, :n_heads]

    tri = tri_ref[...]
    l_hi, l_mid, l_lo = _split3(logf)
    F = _dot(tri, l_hi) + _dot(tri, l_mid) + _dot(tri, l_lo) + fcarry[...]
    fcarry[...] = F[tT - 1:tT, :]

    f_hi, f_mid, f_lo = _split3(F * LOG2E)
    fparts = (f_hi.astype(F32) + pltpu.roll(f_mid.astype(F32), n_heads, axis=1)
              + pltpu.roll(f_lo.astype(F32), 2 * n_heads, axis=1)
              + jnp.where(lane == 3 * n_heads, 1.0, 0.0)).astype(BF16)
    xq = _dot(fparts, sq_ref[...])
    xk = _dot(fparts, sk_ref[...])
    vone = jnp.where(lane == head_dim, 1.0, 0.0)
    scale = head_dim ** -0.5 * LOG2E
    for h in range(n_heads):
        p = (h * head_dim) // hw
        qp = q[:, p * hw:(p + 1) * hw] * scale
        kp = k[:, p * hw:(p + 1) * hw]
        vp = v[:, p * hw:(p + 1) * hw]
        if (h * head_dim) % hw:
            sh = hw - (h * head_dim) % hw
            qp = pltpu.roll(qp, sh, axis=1)
            kp = pltpu.roll(kp, sh, axis=1)
            vp = pltpu.roll(vp, sh, axis=1)
        qa_ref[h] = jnp.where(lane < head_dim, qp, xq[:, h * hw:(h + 1) * hw]).astype(BF16)
        ka_ref[h] = jnp.where(lane < head_dim, kp, xk[:, h * hw:(h + 1) * hw]).astype(BF16)
        va_ref[h] = jnp.where(lane < head_dim, vp, vone).astype(BF16)

    hbc = _dot(xb, whbc_ref[...])
    hh = hbc[:, :dwid]
    bg = hbc[:, dwid:2 * dwid]
    cg = hbc[:, 2 * dwid:]
    g = cg * hh
    gext[GH:, :] = g
    conv = g * cw_ref[taps - 1:taps, :]
    for kk in range(taps - 1):
        back = taps - 1 - kk
        conv = conv + gext[GH - back:GH - back + tT, :] * cw_ref[kk:kk + 1, :]
    yd_ref[...] = (bg * conv).astype(BF16)
    cd_ref[...] = gext[tT:tT + GH, :]


def _odd_prompt(n_heads, head_dim, x, wqkv, wf, whbc, fb, cw):
    B, T, D = x.shape
    tT = min(TOKEN_TILE, T)
    cwid = n_heads * head_dim
    dwid = cw.shape[-1]
    hw = V7X_LANES
    GH = CONV_D_HIST_ROWS
    tri = jnp.tri(tT, dtype=BF16)
    sq, sk = _head_select_mats(n_heads, head_dim)
    consts = (wqkv, wf, whbc, fb, cw, tri, sq, sk)
    tok = lambda w: pl.BlockSpec((None, tT, w), lambda bi, ti: (bi, ti, 0))
    head = pl.BlockSpec((None, n_heads, tT, hw), lambda bi, ti: (bi, 0, ti, 0))
    return pl.pallas_call(
        functools.partial(_odd_prompt_kernel, n_heads, head_dim),
        grid=(B, T // tT),
        in_specs=[tok(D)] + [_const_spec(c.shape) for c in consts],
        out_specs=[tok(cwid), tok(cwid), tok(n_heads), head, head, head, tok(dwid),
                   pl.BlockSpec((None, GH, dwid), lambda bi, ti: (bi, 0, 0))],
        out_shape=[jax.ShapeDtypeStruct((B, T, cwid), F32), jax.ShapeDtypeStruct((B, T, cwid), F32),
                   jax.ShapeDtypeStruct((B, T, n_heads), F32),
                   jax.ShapeDtypeStruct((B, n_heads, T, hw), BF16),
                   jax.ShapeDtypeStruct((B, n_heads, T, hw), BF16),
                   jax.ShapeDtypeStruct((B, n_heads, T, hw), BF16),
                   jax.ShapeDtypeStruct((B, T, dwid), BF16),
                   jax.ShapeDtypeStruct((B, GH, dwid), F32)],
        scratch_shapes=[pltpu.VMEM((GH + tT, dwid), F32), pltpu.VMEM((1, hw), F32)],
        compiler_params=_cparams(("arbitrary", "arbitrary")),
        name="odd_proj_prompt",
    )(x, *consts)


def _odd_sample_kernel(n_heads, head_dim, x_ref, sd_ref, wqkv_ref, wf_ref, whbc_ref, fb_ref, cw_ref,
                       q_ref, k_ref, v_ref, lf_ref, yd_ref, sdo_ref):
    cwid = n_heads * head_dim
    dwid = cw_ref.shape[-1]
    taps = cw_ref.shape[0]
    nh = sd_ref.shape[0]
    xb = x_ref[...].astype(BF16)
    qkv = _dot(xb, wqkv_ref[...])
    q_ref[...] = qkv[:, :cwid]
    k_ref[...] = qkv[:, cwid:2 * cwid]
    v_ref[...] = qkv[:, 2 * cwid:]
    fl = _dot(xb, wf_ref[...])
    lf_ref[...] = _log_sigmoid(fl + fb_ref[...])
    hbc = _dot(xb, whbc_ref[...])
    g = hbc[:, 2 * dwid:] * hbc[:, :dwid]
    conv = g * cw_ref[taps - 1:taps, :]
    for kk in range(taps - 1):
        conv = conv + sd_ref[kk + nh - (taps - 1)] * cw_ref[kk:kk + 1, :]
    yd_ref[...] = (hbc[:, dwid:2 * dwid] * conv).astype(BF16)
    for j in range(nh - 1):
        sdo_ref[j] = sd_ref[j + 1]
    sdo_ref[nh - 1] = g


def _odd_sample(n_heads, head_dim, x, sd_t, wqkv, wf, whbc, fb, cw):
    n = x.shape[0]
    cwid = n_heads * head_dim
    dwid = cw.shape[-1]
    ins = (x, sd_t, wqkv, wf, whbc, fb, cw)
    shapes = [((n, cwid), F32), ((n, cwid), F32), ((n, cwid), F32), ((n, V7X_LANES), F32),
              ((n, dwid), BF16), (sd_t.shape, F32)]
    return pl.pallas_call(
        functools.partial(_odd_sample_kernel, n_heads, head_dim),
        grid=(1,),
        in_specs=[_const_spec(c.shape) for c in ins],
        out_specs=[_const_spec(s) for s, _ in shapes],
        out_shape=[jax.ShapeDtypeStruct(s, d) for s, d in shapes],
        compiler_params=_cparams(("arbitrary",)),
        name="odd_proj_sample",
    )(*ins)


def _fox_prompt_kernel(head_dim, n_side, qa_ref, ka_ref, va_ref, *refs):
    side_in = refs[:n_side]
    o_ref = refs[n_side]
    side_out = refs[n_side + 1:2 * n_side + 1]
    sa, sb = refs[2 * n_side + 1:]
    for src, dst in zip(side_in, side_out):
        dst[...] = src[...].astype(BF16)

    i = pl.program_id(2)
    tq = qa_ref.shape[1]
    hw = V7X_LANES
    lane = lax.broadcasted_iota(I32, (tq, hw), 1)
    row = lax.broadcasted_iota(I32, (tq, tq), 0)
    col = lax.broadcasted_iota(I32, (tq, tq), 1)
    nh = qa_ref.shape[0]
    qs = [qa_ref[hh] for hh in range(nh)]

    def scores(j, hh):
        return _dot_nt(qs[hh], ka_ref[hh, pl.ds(pl.multiple_of(j * tq, tq), tq), :])

    def update(j, hh, m, acc, s):
        m_new = jnp.maximum(m, jnp.max(s, axis=-1, keepdims=True))
        p = jnp.exp2(s - m_new)
        vt = va_ref[hh, pl.ds(pl.multiple_of(j * tq, tq), tq), :]
        return m_new, acc * jnp.exp2(m - m_new) + _dot(p.astype(BF16), vt)

    def step(j, carry, src, dst):
        new = []
        for hh in range(nh):
            dst[hh] = scores(j + 1, hh)
            new.append(update(j, hh, *carry[hh], src[hh]))
        return tuple(new)

    for hh in range(nh):
        sa[hh] = scores(0, hh)

    def pair(u, carry):
        return step(2 * u + 1, step(2 * u, carry, sa, sb), sb, sa)

    odd = i % 2
    carry = tuple((jnp.full((tq, 1), NEG, F32), jnp.zeros((tq, hw), F32)) for _ in range(nh))
    carry = lax.fori_loop(0, i // 2, pair, carry)
    carry = lax.fori_loop(0, odd, lambda _, c: step(i - 1, c, sa, sb), carry)
    def finish(src):
        o = None
        for hh in range(nh):
            _, acc = update(i, hh, *carry[hh], jnp.where(col <= row, src[hh], NEG))
            denom = jnp.sum(jnp.where(lane == head_dim, acc, 0.0), axis=-1, keepdims=True)
            out = acc / denom
            o = out if hh == 0 else jnp.where(lane < hh * head_dim, o, pltpu.roll(out, hh * head_dim, axis=1))
        o_ref[...] = o.astype(o_ref.dtype)

    @pl.when(odd == 0)
    def _():
        finish(sa)

    @pl.when(odd == 1)
    def _():
        finish(sb)


def _fox_prompt(head_dim, qa, ka, va, side=()):
    B, H, T, hw = qa.shape
    tq = min(TOKEN_TILE, T)
    hp = hw // head_dim
    grid = (B, H // hp, T // tq)
    n_steps = grid[0] * grid[1] * grid[2]

    def side_spec(w):
        rows = w.shape[0] // n_steps
        assert rows * n_steps == w.shape[0] and rows % BF16_ROWS_PER_TILE == 0
        return pl.BlockSpec((rows, w.shape[1]), lambda b, p, i: ((b * grid[1] + p) * grid[2] + i, 0))

    side_specs = [side_spec(w) for w in side]
    return pl.pallas_call(
        functools.partial(_fox_prompt_kernel, head_dim, len(side)),
        grid=grid,
        in_specs=[pl.BlockSpec((None, hp, tq, hw), lambda b, p, i: (b, p, i, 0)),
                  pl.BlockSpec((None, hp, T, hw), lambda b, p, i: (b, p, 0, 0)),
                  pl.BlockSpec((None, hp, T, hw), lambda b, p, i: (b, p, 0, 0))] + side_specs,
        out_specs=[pl.BlockSpec((None, tq, hw), lambda b, p, i: (b, i, p))] + side_specs,
        out_shape=[jax.ShapeDtypeStruct((B, T, H * head_dim), BF16)]
        + [jax.ShapeDtypeStruct(w.shape, BF16) for w in side],
        scratch_shapes=[pltpu.VMEM((hp, tq, tq), F32), pltpu.VMEM((hp, tq, tq), F32)],
        compiler_params=_cparams(("arbitrary", "arbitrary", "arbitrary")),
        name="fox_attention_prompt",
    )(qa, ka, va, *side)


def _fox_sample_kernel(npp, pt_ref, *refs):
    k_refs = refs[0:npp]
    v_refs = refs[npp:2 * npp]
    lf_refs = refs[2 * npp:3 * npp]
    (qb_ref, kn_ref, vn_ref, lfn_ref, u_ref, o_ref, m_s, l_s, acc_s, carry_s) = refs[3 * npp:]
    i = pl.program_id(1)
    n_steps = pl.num_programs(1)
    H, dh, page = k_refs[0].shape
    cw = H * dh

    @pl.when(i == 0)
    def _():
        m_s[...] = jnp.full(m_s.shape, NEG, F32)
        l_s[...] = jnp.zeros(l_s.shape, F32)
        acc_s[...] = jnp.zeros(acc_s.shape, F32)
        carry_s[...] = lfn_ref[...]

    lf_all = jnp.concatenate([r[...] for r in lf_refs], axis=0)
    l_hi, l_mid, l_lo = _split3(lf_all)
    u = u_ref[...]
    g_all = _dot(l_hi, u) + _dot(l_mid, u) + _dot(l_lo, u)
    tot = jnp.sum(lf_all, axis=1, keepdims=True)
    later = carry_s[...]
    decay = [None] * npp
    for j in reversed(range(npp)):
        decay[j] = later
        later = later + tot[j * H:(j + 1) * H, :]
    carry_s[...] = later

    qb = qb_ref[...]
    qbb = qb.astype(BF16)
    s_pages = []
    for j in range(npp):
        k2 = k_refs[j][...].reshape(cw, page).astype(BF16)
        s_pages.append(_dot(qbb, k2) + g_all[j * H:(j + 1) * H, :] + decay[j])

    mx = s_pages[0]
    for j in range(1, npp):
        mx = jnp.maximum(mx, s_pages[j])
    m = m_s[...]
    m_new = jnp.maximum(m, jnp.max(mx, axis=1, keepdims=True))
    c = jnp.exp(m - m_new)
    p_pages = [jnp.exp(s - m_new) for s in s_pages]
    psum = p_pages[0]
    for j in range(1, npp):
        psum = psum + p_pages[j]
    l_s[...] = l_s[...] * c + psum
    m_s[...] = m_new
    acc = acc_s[...] * c
    for j in range(npp):
        v2 = v_refs[j][...].reshape(cw, page).astype(BF16)
        acc = acc + _dot_nt(p_pages[j].astype(BF16), v2)
    acc_s[...] = acc

    @pl.when(i == n_steps - 1)
    def _():
        s_new = jnp.sum(qb * kn_ref[...], axis=1, keepdims=True)
        m_fin = jnp.maximum(m_new, s_new)
        cf = jnp.exp(m_new - m_fin)
        p_new = jnp.exp(s_new - m_fin)
        denom = jnp.sum(l_s[...], axis=1, keepdims=True) * cf + p_new
        o_ref[...] = (acc * cf + p_new * vn_ref[...]) / denom


def _fox_sample(page_table, kview, vview, lfview, qb, kn, vn, lfn):
    nb, n_pages = page_table.shape
    _, H, dh, page = kview.shape
    npp = min(PAGES_PER_STEP, n_pages)
    assert n_pages % npp == 0
    n_steps = n_pages // npp
    u = jnp.asarray(np.tril(np.ones((page, page), np.float32), -1), BF16)

    def page_spec(shape, j):
        nd = len(shape)
        return pl.BlockSpec((None,) + tuple(shape),
                            lambda b, i, pt, j=j: (pt[b, n_pages - npp * (i + 1) + j],) + (0,) * nd)

    seq = lambda shape: pl.BlockSpec((None,) + tuple(shape), lambda b, i, pt: (b,) + (0,) * len(shape))
    in_specs = ([page_spec((H, dh, page), j) for j in range(npp)]
                + [page_spec((H, dh, page), j) for j in range(npp)]
                + [page_spec((H, page), j) for j in range(npp)]
                + [seq((H, H * dh)), seq((1, H * dh)), seq((1, H * dh)), seq((H, 1)),
                   pl.BlockSpec(u.shape, lambda b, i, pt: (0, 0))])
    grid_spec = pltpu.PrefetchScalarGridSpec(
        num_scalar_prefetch=1, grid=(nb, n_steps), in_specs=in_specs,
        out_specs=seq((H, H * dh)),
        scratch_shapes=[pltpu.VMEM((H, 1), F32), pltpu.VMEM((H, page), F32), pltpu.VMEM((H, H * dh), F32),
                        pltpu.VMEM((H, 1), F32)])
    return pl.pallas_call(
        functools.partial(_fox_sample_kernel, npp),
        grid_spec=grid_spec,
        out_shape=jax.ShapeDtypeStruct((nb, H, H * dh), F32),
        compiler_params=_cparams(("arbitrary", "arbitrary")),
        name="fox_attention_sample",
    )(page_table, *([kview] * npp), *([vview] * npp), *([lfview] * npp), qb, kn, vn, lfn, u)


def _top2(logits, axis, n_experts):
    idx = lax.broadcasted_iota(I32, logits.shape, axis)
    big = logits.shape[axis]
    lg = jnp.where(idx < n_experts, logits, -jnp.inf)
    v1 = jnp.max(lg, axis=axis, keepdims=True)
    i1 = jnp.min(jnp.where(lg == v1, idx, big), axis=axis, keepdims=True)
    lg2 = jnp.where(idx == i1, -jnp.inf, lg)
    v2 = jnp.max(lg2, axis=axis, keepdims=True)
    i2 = jnp.min(jnp.where(lg2 == v2, idx, big), axis=axis, keepdims=True)
    return idx, i1, i2, v1, v2


def _odd_out_kernel(alpha, n_experts, n_prompt_tiles, op_ref, ydp_ref, xp_ref, os_ref, yds_ref, xs_ref,
                    wout_ref, g_ref, b_ref, rw_ref, x3p_ref, x3s_ref, x3b_ref, lg_ref, cnt_ref):
    i = pl.program_id(0)
    cwid = op_ref.shape[1]

    def rows(o_ref, yd_ref, x_ref, x3_ref):
        n = o_ref.shape[0]
        mix = _dot(o_ref[...].astype(BF16), wout_ref[0:cwid, :]) + _dot(yd_ref[...], wout_ref[cwid:, :])
        x3 = _ln(alpha * x_ref[...] + mix, g_ref[...], b_ref[...])
        x3b = x3.astype(BF16)
        logits = _dot(x3b, rw_ref[...])
        x3_ref[...] = x3
        if n < x3b_ref.shape[0]:
            x3b_ref[...] = jnp.zeros(x3b_ref.shape, BF16)
            lg_ref[...] = jnp.zeros(lg_ref.shape, F32)
        x3b_ref[0:n, :] = x3b
        lg_ref[0:n, :] = logits
        idx, i1, i2, _, _ = _top2(logits, 1, n_experts)
        mask = jnp.where((idx == i1) | (idx == i2), 1.0, 0.0)
        cnt_ref[...] = jnp.sum(mask, axis=0, keepdims=True)

    @pl.when(i < n_prompt_tiles)
    def _():
        rows(op_ref, ydp_ref, xp_ref, x3p_ref)

    @pl.when(i >= n_prompt_tiles)
    def _():
        rows(os_ref, yds_ref, xs_ref, x3s_ref)


def _odd_out(alpha, n_experts, o_p, yd_p, x_p, o_s, yd_s, x_s, wout, g, b, rw):
    n, d = x_p.shape
    ns = x_s.shape[0]
    tm = TOKEN_TILE
    npt = n // tm
    ntiles = npt + 1
    hw = V7X_LANES
    prow = lambda w: pl.BlockSpec((tm, w), lambda i: (jnp.minimum(i, npt - 1), 0))
    return pl.pallas_call(
        functools.partial(_odd_out_kernel, alpha, n_experts, npt),
        grid=(ntiles,),
        in_specs=[prow(o_p.shape[1]), prow(yd_p.shape[1]), prow(d),
                  _const_spec(o_s.shape), _const_spec(yd_s.shape), _const_spec(x_s.shape)]
        + [_const_spec(c.shape) for c in (wout, g, b, rw)],
        out_specs=[prow(d), _const_spec((ns, d)),
                   pl.BlockSpec((tm, d), lambda i: (i, 0)),
                   pl.BlockSpec((tm, hw), lambda i: (i, 0)),
                   pl.BlockSpec((None, 1, hw), lambda i: (i, 0, 0))],
        out_shape=[jax.ShapeDtypeStruct((n, d), F32), jax.ShapeDtypeStruct((ns, d), F32),
                   jax.ShapeDtypeStruct((ntiles * tm, d), BF16),
                   jax.ShapeDtypeStruct((ntiles * tm, hw), F32),
                   jax.ShapeDtypeStruct((ntiles, 1, hw), F32)],
        compiler_params=_cparams(("arbitrary",)),
        name="odd_out_router",
    )(o_p, yd_p, x_p, o_s, yd_s, x_s, wout, g, b, rw)


def _segment_copies(tile, n_experts, npc_ref, loff_ref, goff_ref, make_copy):
    ratio = SEG_BIG // SEG_PAD
    for e in range(n_experts):
        n = npc_ref[tile * n_experts + e]
        lo = loff_ref[tile * n_experts + e]
        go = goff_ref[tile * n_experts + e]
        nbig = n // ratio

        def big(j, _, lo=lo, go=go):
            make_copy(pl.multiple_of(lo + j * SEG_BIG, SEG_PAD), pl.multiple_of(go + j * SEG_BIG, SEG_PAD), SEG_BIG)
            return 0

        def small(j, _, lo=lo, go=go, nbig=nbig):
            off = nbig * SEG_BIG + j * SEG_PAD
            make_copy(pl.multiple_of(lo + off, SEG_PAD), pl.multiple_of(go + off, SEG_PAD), SEG_PAD)
            return 0

        lax.fori_loop(0, nbig, big, 0)
        lax.fori_loop(0, n - nbig * ratio, small, 0)


def _dispatch_kernel(n_experts, n_valid, npc_ref, loff_ref, goff_ref, x_ref, lt_ref, lofft_ref, triu_ref,
                     xs_in_ref, gs_in_ref, xs_ref, gs_ref, stage, gstage, sem):
    del xs_in_ref, gs_in_ref
    tile = pl.program_id(0)
    tt = x_ref.shape[0]
    rl = stage.shape[0]
    sub, i1, i2, v1, v2 = _top2(lt_ref[...], 0, n_experts)
    valid = (tile * tt + lax.broadcasted_iota(I32, (1, tt), 1)) < n_valid
    sel1 = (sub == i1) & valid
    sel2 = (sub == i2) & valid
    mask = jnp.where(sel1 | sel2, 1.0, 0.0)
    rank = _dot(mask.astype(BF16), triu_ref[...])
    loc = rank + jnp.concatenate([lofft_ref[...]] * (tt // V7X_LANES), axis=1)
    lr1 = jnp.sum(jnp.where(sel1, loc, 0.0), axis=0, keepdims=True).astype(I32)
    lr2 = jnp.sum(jnp.where(sel2, loc, 0.0), axis=0, keepdims=True).astype(I32)
    lr1 = jnp.where(valid, lr1, -1)
    lr2 = jnp.where(valid, lr2, -1)
    r = lax.broadcasted_iota(I32, (rl, tt), 0)
    hit1 = r == lr1
    hit2 = r == lr2
    onehot = jnp.where(hit1 | hit2, 1.0, 0.0).astype(BF16)
    stage[...] = _dot(onehot, x_ref[...]).astype(BF16)
    e21 = jnp.exp(v2 - v1)
    g1 = 1.0 / (1.0 + e21)
    g2 = e21 / (1.0 + e21)
    gstage[...] = jnp.sum(jnp.where(hit1, g1, 0.0) + jnp.where(hit2, g2, 0.0), axis=1, keepdims=True)

    def copies(fn):
        _segment_copies(tile, n_experts, npc_ref, loff_ref, goff_ref,
                        lambda lo, go, n: fn(pltpu.make_async_copy(stage.at[pl.ds(lo, n), :],
                                                                  xs_ref.at[pl.ds(go, n), :], sem.at[0])))
        _segment_copies(tile, n_experts, npc_ref, loff_ref, goff_ref,
                        lambda lo, go, n: fn(pltpu.make_async_copy(gstage.at[pl.ds(lo, n), :],
                                                                  gs_ref.at[pl.ds(go, n), :], sem.at[1])))

    copies(lambda cp: cp.start())
    copies(lambda cp: cp.wait())


def _dispatch(n_experts, n_valid, npc, loff, goff, x3b, lt, lofft, xs_zero, gs_zero):
    ntot, d = x3b.shape
    tt = TOKEN_TILE
    triu = jnp.asarray(np.triu(np.ones((tt, tt), np.float32), 1), BF16)
    grid_spec = pltpu.PrefetchScalarGridSpec(
        num_scalar_prefetch=3, grid=(ntot // tt,),
        in_specs=[pl.BlockSpec((tt, d), lambda i, *_: (i, 0)),
                  pl.BlockSpec((n_experts, tt), lambda i, *_: (0, i)),
                  pl.BlockSpec((None, n_experts, V7X_LANES), lambda i, *_: (i, 0, 0)),
                  pl.BlockSpec(triu.shape, lambda i, *_: (0, 0)),
                  pl.BlockSpec(memory_space=pl.ANY), pl.BlockSpec(memory_space=pl.ANY)],
        out_specs=[pl.BlockSpec(memory_space=pl.ANY), pl.BlockSpec(memory_space=pl.ANY)],
        scratch_shapes=[pltpu.VMEM((LOCAL_ROWS, d), BF16), pltpu.VMEM((LOCAL_ROWS, 1), F32),
                        pltpu.SemaphoreType.DMA((2,))])
    return pl.pallas_call(
        functools.partial(_dispatch_kernel, n_experts, n_valid),
        grid_spec=grid_spec,
        out_shape=[jax.ShapeDtypeStruct(xs_zero.shape, BF16), jax.ShapeDtypeStruct(gs_zero.shape, F32)],
        input_output_aliases={7: 0, 8: 1},
        compiler_params=_cparams(("arbitrary",)),
        name="moe_dispatch",
    )(npc, loff, goff, x3b, lt, lofft, triu, xs_zero, gs_zero)


def _experts_kernel(te_ref, tv_ref, xr_ref, x_ref, gate_ref, w1_ref, w3_ref, w2_ref, yh_ref, yl_ref, h_ref):
    del te_ref, xr_ref
    i = pl.program_id(0)
    c = pl.program_id(1)
    last = pl.num_programs(1) - 1
    fc = w1_ref.shape[1]
    valid = tv_ref[i] > 0

    @pl.when(valid)
    def _():
        x = x_ref[...]
        h = (_silu(_dot(x, w1_ref[...])) * _dot(x, w3_ref[...])).astype(BF16)
        h_ref[:, pl.ds(pl.multiple_of(c * fc, fc), fc)] = h

        @pl.when(c == last)
        def _():
            y = _dot(h_ref[...], w2_ref[...]) * gate_ref[...]
            hi = y.astype(BF16)
            yh_ref[...] = hi
            yl_ref[...] = (y - hi.astype(F32)).astype(BF16)

    @pl.when(jnp.logical_not(valid) & (c == last))
    def _():
        yh_ref[...] = jnp.zeros(yh_ref.shape, BF16)
        yl_ref[...] = jnp.zeros(yl_ref.shape, BF16)


def _experts(te, tv, xr, xs, gates, w1, w3, w2):
    rmax, d = xs.shape
    tm = EXPERT_TILE
    ff = w1.shape[-1]
    fc = min(EXPERT_FF_CHUNK, ff)
    nch = ff // fc
    assert ff % fc == 0 and rmax % tm == 0

    def chunk(i, c, tv):
        return jnp.where(tv[i] > 0, c, nch - 1)

    grid_spec = pltpu.PrefetchScalarGridSpec(
        num_scalar_prefetch=3, grid=(rmax // tm, nch),
        in_specs=[pl.BlockSpec((tm, d), lambda i, c, te, tv, xr: (xr[i], 0)),
                  pl.BlockSpec((tm, 1), lambda i, c, te, tv, xr: (xr[i], 0)),
                  pl.BlockSpec((None, d, fc), lambda i, c, te, tv, xr: (te[i], 0, chunk(i, c, tv))),
                  pl.BlockSpec((None, d, fc), lambda i, c, te, tv, xr: (te[i], 0, chunk(i, c, tv))),
                  pl.BlockSpec((None, ff, d), lambda i, c, te, tv, xr: (te[i], 0, 0))],
        out_specs=[pl.BlockSpec((tm, d), lambda i, c, te, tv, xr: (i, 0)),
                   pl.BlockSpec((tm, d), lambda i, c, te, tv, xr: (i, 0))],
        scratch_shapes=[pltpu.VMEM((tm, ff), BF16)])
    return pl.pallas_call(
        _experts_kernel,
        grid_spec=grid_spec,
        out_shape=[jax.ShapeDtypeStruct((rmax, d), BF16), jax.ShapeDtypeStruct((rmax, d), BF16)],
        compiler_params=_cparams(("arbitrary", "arbitrary")),
        name="moe_experts",
    )(te, tv, xr, xs, gates, w1, w3, w2)


def _combine_kernel(alpha, n_experts, n_valid, n_prompt_tiles, npc_ref, loff_ref, goff_ref, lg_ref, xp_ref,
                    xs_ref, loffr_ref, tril_ref, g_ref, b_ref, yh_hbm, yl_hbm, yp_ref, ysm_ref,
                    sth, stl, sem):
    tile = pl.program_id(0)
    tt = lg_ref.shape[0]
    rl = sth.shape[0]
    d = sth.shape[1]

    def copies(fn):
        _segment_copies(tile, n_experts, npc_ref, loff_ref, goff_ref,
                        lambda lo, go, n: fn(pltpu.make_async_copy(yh_hbm.at[pl.ds(go, n), :],
                                                                  sth.at[pl.ds(lo, n), :], sem.at[0])))
        _segment_copies(tile, n_experts, npc_ref, loff_ref, goff_ref,
                        lambda lo, go, n: fn(pltpu.make_async_copy(yl_hbm.at[pl.ds(go, n), :],
                                                                  stl.at[pl.ds(lo, n), :], sem.at[1])))

    copies(lambda cp: cp.start())

    last = tile * n_experts + n_experts - 1
    used = loff_ref[last] // SEG_PAD + npc_ref[last]

    def clear(j, _):
        o = pl.multiple_of(j * SEG_PAD, SEG_PAD)
        sth[pl.ds(o, SEG_PAD), :] = jnp.zeros((SEG_PAD, d), BF16)
        stl[pl.ds(o, SEG_PAD), :] = jnp.zeros((SEG_PAD, d), BF16)
        return 0

    lax.fori_loop(used, rl // SEG_PAD, clear, 0)

    lane, i1, i2, _, _ = _top2(lg_ref[...], 1, n_experts)
    valid = (tile * tt + lax.broadcasted_iota(I32, (tt, 1), 0)) < n_valid
    sel1 = (lane == i1) & valid
    sel2 = (lane == i2) & valid
    mask = jnp.where(sel1 | sel2, 1.0, 0.0)
    rank = _dot(tril_ref[...], mask.astype(BF16))
    loc = rank + jnp.concatenate([loffr_ref[...]] * (tt // V7X_SUBLANES), axis=0)
    lr1 = jnp.sum(jnp.where(sel1, loc, 0.0), axis=1, keepdims=True).astype(I32)
    lr2 = jnp.sum(jnp.where(sel2, loc, 0.0), axis=1, keepdims=True).astype(I32)
    lr1 = jnp.where(valid, lr1, -1)
    lr2 = jnp.where(valid, lr2, -1)
    r = lax.broadcasted_iota(I32, (tt, rl), 1)
    pick = jnp.where((r == lr1) | (r == lr2), 1.0, 0.0).astype(BF16)

    copies(lambda cp: cp.wait())
    moe = _dot(pick, sth[...]) + _dot(pick, stl[...])

    @pl.when(tile < n_prompt_tiles)
    def _():
        yp_ref[...] = _ln(alpha * xp_ref[...] + moe, g_ref[...], b_ref[...])

    @pl.when(tile >= n_prompt_tiles)
    def _():
        ns = ysm_ref.shape[0]
        ysm_ref[...] = _ln(alpha * xs_ref[...] + moe[0:ns, :], g_ref[...], b_ref[...])


def _combine(alpha, n_experts, n_valid, npc, loff, goff, logits, x3p, x3s, loffr, g, b, yh, yl):
    npr, d = x3p.shape
    ns = x3s.shape[0]
    tt = TOKEN_TILE
    npt = npr // tt
    ntiles = logits.shape[0] // tt
    tril = jnp.asarray(np.tril(np.ones((tt, tt), np.float32), -1), BF16)
    grid_spec = pltpu.PrefetchScalarGridSpec(
        num_scalar_prefetch=3, grid=(ntiles,),
        in_specs=[pl.BlockSpec((tt, V7X_LANES), lambda i, *_: (i, 0)),
                  pl.BlockSpec((tt, d), lambda i, *_: (jnp.minimum(i, npt - 1), 0)),
                  pl.BlockSpec((ns, d), lambda i, *_: (0, 0)),
                  pl.BlockSpec((None, V7X_SUBLANES, V7X_LANES), lambda i, *_: (i, 0, 0)),
                  pl.BlockSpec(tril.shape, lambda i, *_: (0, 0)),
                  pl.BlockSpec(g.shape, lambda i, *_: (0, 0)),
                  pl.BlockSpec(b.shape, lambda i, *_: (0, 0)),
                  pl.BlockSpec(memory_space=pl.ANY),
                  pl.BlockSpec(memory_space=pl.ANY)],
        out_specs=[pl.BlockSpec((tt, d), lambda i, *_: (jnp.minimum(i, npt - 1), 0)),
                   pl.BlockSpec((ns, d), lambda i, *_: (0, 0))],
        scratch_shapes=[pltpu.VMEM((LOCAL_ROWS, d), BF16), pltpu.VMEM((LOCAL_ROWS, d), BF16),
                        pltpu.SemaphoreType.DMA((2,))])
    return pl.pallas_call(
        functools.partial(_combine_kernel, alpha, n_experts, n_valid, npt),
        grid_spec=grid_spec,
        out_shape=[jax.ShapeDtypeStruct((npr, d), F32), jax.ShapeDtypeStruct((ns, d), F32)],
        compiler_params=_cparams(("arbitrary",)),
        name="moe_combine",
    )(npc, loff, goff, logits, x3p, x3s, loffr, tril, g, b, yh, yl)


def _routing_plan(cnt, n_experts):
    ntiles = cnt.shape[0]
    pc = (cnt + SEG_PAD - 1) // SEG_PAD * SEG_PAD
    loff = jnp.cumsum(pc, axis=1) - pc
    per_expert = jnp.sum(pc, axis=0)
    gp = (per_expert + EXPERT_TILE - 1) // EXPERT_TILE * EXPERT_TILE
    gend = jnp.cumsum(gp)
    goff = (gend - gp)[None, :] + jnp.cumsum(pc, axis=0) - pc
    rmax = TOP_K * ntiles * TOKEN_TILE + ntiles * n_experts * (SEG_PAD - 1) + n_experts * (EXPERT_TILE - 1)
    nt_max = -(-rmax // EXPERT_TILE)
    tiles_used = gend[-1] // EXPERT_TILE
    ti = jnp.arange(nt_max, dtype=I32)
    tv = (ti < tiles_used).astype(I32)
    xr = jnp.maximum(jnp.minimum(ti, tiles_used - 1), 0)
    te = jnp.sum((xr[:, None] >= (gend // EXPERT_TILE)[None, :]).astype(I32), axis=1)
    te = jnp.minimum(te, n_experts - 1)
    flat = lambda a: a.reshape(-1).astype(I32)
    return (flat(pc // SEG_PAD), flat(loff), flat(goff), loff.astype(F32), te, tv, xr.astype(I32),
            nt_max * EXPERT_TILE)


def kernel(x_prompt, x_sample, state_pool, state_conv_b, cache_k, cache_v, cache_logf, state_conv_d, page_table,
           w_in_even, pool_w, pool_scale, conv_b_w, conv_b_bias, conv_ln_g, conv_ln_b, w_out_even, ln_mix_even_g,
           ln_mix_even_b, ffn_w1, ffn_w3, ffn_w2, ln_ffn_even_g, ln_ffn_even_b, w_in_odd, forget_bias, conv_d_w,
           w_out_odd, ln_mix_odd_g, ln_mix_odd_b, router_w, moe_w1, moe_w3, moe_w2, ln_ffn_odd_g, ln_ffn_odd_b):
    assert w_in_even.shape[0] == 1 and w_in_odd.shape[0] == 1, "one even and one odd layer are supported"
    depth = w_in_even.shape[0] + w_in_odd.shape[0]
    alpha = float((2 * depth) ** 0.25)
    B, T, D = x_prompt.shape
    nb = x_sample.shape[0]
    assert x_sample.shape[1] == 1 and T % TOKEN_TILE == 0 and nb <= TOKEN_TILE
    n_heads = forget_bias.shape[-1]
    cwid = cache_k.shape[-1] * cache_k.shape[-2]
    head_dim = cache_k.shape[-1]
    dwid = conv_d_w.shape[-1]
    n_experts = router_w.shape[-1]
    past_len = page_table.shape[1] * cache_k.shape[2]
    hw = V7X_LANES
    bf = lambda w: w.astype(BF16)
    rowv = lambda w: w.reshape(1, -1)

    ew = (bf(w_in_even[0]), bf(pool_w[0]), pool_scale, conv_b_w[0], conv_b_bias, conv_ln_g, conv_ln_b,
          bf(w_out_even[0]), ln_mix_even_g, ln_mix_even_b)
    xp, a_hist, u_hist = _even_prompt(alpha, x_prompt, *ew)
    xs, pool_s_t, convb_s_t = _even_sample(
        alpha, past_len, x_sample.reshape(nb, D), jnp.swapaxes(state_pool[0], 0, 1),
        jnp.swapaxes(state_conv_b[0], 0, 1), *ew)
    nph = state_pool.shape[2]
    nch = state_conv_b.shape[2]
    pool_p = a_hist[None, :, POOL_HIST_ROWS - nph:, :]
    convb_p = u_hist[None, :, CONV_B_HIST_ROWS - nch:, :]
    pool_s = jnp.swapaxes(pool_s_t, 0, 1)[None]
    convb_s = jnp.swapaxes(convb_s_t, 0, 1)[None]

    fw = (bf(ffn_w1[0]), bf(ffn_w3[0]), bf(ffn_w2[0]), ln_ffn_even_g, ln_ffn_even_b)
    xp = _ffn(alpha, xp.reshape(B * T, D), *fw)
    xs = _ffn(alpha, xs, *fw)

    w_in = w_in_odd[0]
    wqkv = bf(w_in[:, :3 * cwid])
    wf = bf(jnp.pad(w_in[:, 3 * cwid:3 * cwid + n_heads], ((0, 0), (0, hw - n_heads))))
    whbc = bf(w_in[:, 3 * cwid + n_heads:])
    fb = jnp.pad(forget_bias, ((0, 0), (0, hw - n_heads)))
    k_p, v_p, lf_p, qa, ka, va, yd_p, cd_p = _odd_prompt(
        n_heads, head_dim, xp.reshape(B, T, D), wqkv, wf, whbc, fb, conv_d_w[0])
    ne, _, ffe = moe_w1.shape[1:]
    o_p, mw1, mw3, mw2 = _fox_prompt(head_dim, qa, ka, va, side=(
        moe_w1[0].reshape(ne * D, ffe), moe_w3[0].reshape(ne * D, ffe), moe_w2[0].reshape(ne * ffe, D)))
    mw1, mw3, mw2 = mw1.reshape(ne, D, ffe), mw3.reshape(ne, D, ffe), mw2.reshape(ne, ffe, D)

    q_s, k_s, v_s, lf_s, yd_s, convd_s_t = _odd_sample(
        n_heads, head_dim, xs, jnp.swapaxes(state_conv_d[0], 0, 1), wqkv, wf, whbc, fb, conv_d_w[0])
    head_of_col = jnp.arange(cwid, dtype=I32) // head_dim
    blockdiag = (head_of_col[None, :] == jnp.arange(n_heads, dtype=I32)[:, None]).astype(F32)
    lf_s = lf_s[:, :n_heads]
    o_s_all = _fox_sample(page_table, jnp.transpose(cache_k[0], (0, 2, 3, 1)),
                          jnp.transpose(cache_v[0], (0, 2, 3, 1)), jnp.transpose(cache_logf[0], (0, 2, 1)),
                          (q_s * head_dim ** -0.5)[:, None, :] * blockdiag[None], k_s[:, None, :], v_s[:, None, :],
                          lf_s.reshape(nb, n_heads, 1))
    o_s = jnp.sum(o_s_all * blockdiag[None], axis=1)

    n_prompt = B * T
    npt = n_prompt // TOKEN_TILE
    ntiles = npt + 1
    n_valid = n_prompt + nb
    ow = (bf(w_out_odd[0]), ln_mix_odd_g, ln_mix_odd_b,
          bf(jnp.pad(router_w[0], ((0, 0), (0, hw - n_experts)))))
    x3p, x3s, x3b, logits, cnt = _odd_out(alpha, n_experts, o_p.reshape(n_prompt, cwid),
                                          yd_p.reshape(n_prompt, dwid), xp, o_s, yd_s, xs, *ow)

    cnt = cnt[:, 0, :n_experts].astype(I32)
    npc, loff, goff, loff_f, te, tv, xr, rmax = _routing_plan(cnt, n_experts)
    lt = jnp.transpose(logits[:, :n_experts])
    lofft = jnp.broadcast_to(loff_f[:, :, None], (ntiles, n_experts, hw))
    loffr = jnp.broadcast_to(jnp.pad(loff_f, ((0, 0), (0, hw - n_experts)))[:, None, :],
                             (ntiles, V7X_SUBLANES, hw))
    xs_sorted, gates = _dispatch(n_experts, n_valid, npc, loff, goff, x3b, lt, lofft,
                                 jnp.zeros((rmax, D), BF16), jnp.zeros((rmax, 1), F32))
    yh, yl = _experts(te, tv, xr, xs_sorted, gates, mw1, mw3, mw2)
    y_p, y_s = _combine(alpha, n_experts, n_valid, npc, loff, goff, logits, x3p, x3s, loffr,
                        ln_ffn_odd_g, ln_ffn_odd_b, yh, yl)

    nd = state_conv_d.shape[2]
    return (y_p.reshape(B, T, D), y_s.reshape(nb, 1, D),
            pool_p, pool_s, convb_p, convb_s,
            k_p.reshape(1, B, T, n_heads, head_dim), k_s.reshape(1, nb, 1, n_heads, head_dim),
            v_p.reshape(1, B, T, n_heads, head_dim), v_s.reshape(1, nb, 1, n_heads, head_dim),
            lf_p.reshape(1, B, T, n_heads), lf_s.reshape(1, nb, 1, n_heads),
            cd_p[None, :, CONV_D_HIST_ROWS - nd:, :], jnp.swapaxes(convd_s_t, 0, 1)[None])
```

```python
import functools

import numpy as np
import jax
import jax.numpy as jnp
from jax import lax
from jax.experimental import pallas as pl
from jax.experimental.pallas import tpu as pltpu

F32 = jnp.float32
BF16 = jnp.bfloat16
I32 = jnp.int32

LN_EPS = 1e-5
POOL_WINDOWS = (2, 4, 8, 16)
TOP_K = 2
NEG = -1e30
LOG2E = 1.4426950408889634

V7X_VMEM_BYTES = 64 * 1024 * 1024
V7X_LANES = 128
V7X_SUBLANES = 8
BF16_ROWS_PER_TILE = 2 * V7X_SUBLANES

VMEM_LIMIT = V7X_VMEM_BYTES - 8 * 1024 * 1024

TOKEN_TILE = 512
EXPERT_TILE = 512
EXPERT_FF_CHUNK = 1792
FFN_CHUNK = 256
CONV_ROWS = 64
POOL_HIST_ROWS = 16
CONV_B_HIST_ROWS = 32
CONV_D_HIST_ROWS = 8
PAGES_PER_STEP = 16
SEG_PAD = BF16_ROWS_PER_TILE
SEG_BIG = 128
LOCAL_ROWS = TOP_K * TOKEN_TILE + 8 * SEG_PAD


def _cparams(sem):
    return pltpu.CompilerParams(dimension_semantics=sem, vmem_limit_bytes=VMEM_LIMIT)


def _dot(a, b):
    return jnp.dot(a, b, preferred_element_type=F32)


def _dot_nt(a, b):
    return lax.dot_general(a, b, (((1,), (1,)), ((), ())), preferred_element_type=F32)


def _dot_tn(a, b):
    return lax.dot_general(a, b, (((0,), (0,)), ((), ())), preferred_element_type=F32)


def _ln(z, g, b):
    mu = jnp.mean(z, axis=-1, keepdims=True)
    d = z - mu
    var = jnp.mean(d * d, axis=-1, keepdims=True)
    return d * lax.rsqrt(var + LN_EPS) * g + b


def _silu(x):
    return x * jax.nn.sigmoid(x)


def _log_sigmoid(z):
    return jnp.minimum(z, 0.0) - jnp.log1p(jnp.exp(-jnp.abs(z)))


def _split3(x):
    hi = x.astype(BF16)
    r = x - hi.astype(F32)
    mid = r.astype(BF16)
    lo = (r - mid.astype(F32)).astype(BF16)
    return hi, mid, lo


def _const_spec(shape):
    nd = len(shape)
    return pl.BlockSpec(shape, lambda *_: (0,) * nd)


def _resident_spec(shape):
    nd = len(shape)
    return pl.BlockSpec(shape, lambda *_: (0,) * nd, pipeline_mode=pl.Buffered(1))


def _pool_groups(a, hist_fn, pos, pw_ref):
    outs = []
    gw = a.shape[1] // len(POOL_WINDOWS)
    for g, w in enumerate(POOL_WINDOWS):
        c0 = g * gw
        cur = a[:, c0:c0 + gw]
        win = cur
        for k in range(1, w):
            win = win + hist_fn(k, c0, gw)
        cnt = jnp.minimum(pos + 1, w).astype(F32)
        pooled = win / cnt - cur
        outs.append(_dot(pooled.astype(BF16), pw_ref[g]))
    return jnp.concatenate(outs, axis=-1)


def _even_prompt_kernel(alpha, x_ref, win_ref, pw_ref, ps_ref, cw_ref, cb_ref, cg_ref, cbeta_ref,
                        wout_ref, g_ref, b_ref, y_ref, ah_ref, uh_ref, aext, uext, ush, ybuf):
    t = pl.program_id(1)
    tT = x_ref.shape[0]
    aw = ps_ref.shape[-1]
    bw = cw_ref.shape[-1]
    taps = cw_ref.shape[0]
    AH, UH = POOL_HIST_ROWS, CONV_B_HIST_ROWS

    @pl.when(t == 0)
    def _():
        aext[0:AH, :] = jnp.zeros((AH, aw), F32)
        uext[0:UH, :] = jnp.zeros((UH, bw), F32)

    @pl.when(t > 0)
    def _():
        aext[0:AH, :] = aext[tT:tT + AH, :]
        uext[0:UH, :] = uext[tT:tT + UH, :]

    x = x_ref[...]
    proj = _dot(x.astype(BF16), win_ref[...])
    a = proj[:, :aw]
    u = proj[:, aw:aw + bw] * jax.nn.sigmoid(proj[:, aw + bw:])
    aext[AH:, :] = a
    uext[UH:, :] = u

    pos = t * tT + lax.broadcasted_iota(I32, (tT, 1), 0)
    ya = _pool_groups(a, lambda k, c0, gw: aext[AH - k:AH - k + tT, c0:c0 + gw], pos, pw_ref)
    ya = ya * ps_ref[...]

    base = UH - (taps - 1)
    ns = V7X_SUBLANES
    span = UH + tT - ns
    for s in range(1, ns):
        ush[s - 1, 0:span, :] = uext[s:s + span, :]

    def tap(k, r0):
        off = base + k
        a, s = off - off % ns + r0, off % ns
        rows = uext[a:a + CONV_ROWS, :] if s == 0 else ush[s - 1, a:a + CONV_ROWS, :]
        return rows * cw_ref[k:k + 1, :]

    for r0 in range(0, tT, CONV_ROWS):
        acc = tap(0, r0)
        for k in range(1, taps):
            acc = acc + tap(k, r0)
        yb = _silu(_ln(acc + cb_ref[...], cg_ref[...], cbeta_ref[...]))
        ybuf[r0:r0 + CONV_ROWS, :] = yb.astype(BF16)

    mix = _dot(ya.astype(BF16), wout_ref[0:aw, :]) + _dot(ybuf[...], wout_ref[aw:, :])
    y_ref[...] = _ln(alpha * x + mix, g_ref[...], b_ref[...])
    ah_ref[...] = aext[tT:tT + AH, :]
    uh_ref[...] = uext[tT:tT + UH, :]


def _even_prompt(alpha, x, win, pw, ps, cw, cb, cg, cbeta, wout, g, b):
    B, T, D = x.shape
    tT = min(TOKEN_TILE, T)
    aw, bw = ps.shape[-1], cw.shape[-1]
    AH, UH = POOL_HIST_ROWS, CONV_B_HIST_ROWS
    consts = (win, pw, ps, cw, cb, cg, cbeta, wout, g, b)
    return pl.pallas_call(
        functools.partial(_even_prompt_kernel, alpha),
        grid=(B, T // tT),
        in_specs=[pl.BlockSpec((None, tT, D), lambda bi, ti: (bi, ti, 0))]
        + [_const_spec(c.shape) for c in consts],
        out_specs=[pl.BlockSpec((None, tT, D), lambda bi, ti: (bi, ti, 0)),
                   pl.BlockSpec((None, AH, aw), lambda bi, ti: (bi, 0, 0)),
                   pl.BlockSpec((None, UH, bw), lambda bi, ti: (bi, 0, 0))],
        out_shape=[jax.ShapeDtypeStruct((B, T, D), F32),
                   jax.ShapeDtypeStruct((B, AH, aw), F32),
                   jax.ShapeDtypeStruct((B, UH, bw), F32)],
        scratch_shapes=[pltpu.VMEM((AH + tT, aw), F32), pltpu.VMEM((UH + tT, bw), F32),
                        pltpu.VMEM((V7X_SUBLANES - 1, UH + tT, bw), F32), pltpu.VMEM((tT, bw), BF16)],
        compiler_params=_cparams(("arbitrary", "arbitrary")),
        name="even_mixer_prompt",
    )(x, *consts)


def _even_sample_kernel(alpha, first_pos, x_ref, sp_ref, sc_ref, win_ref, pw_ref, ps_ref, cw_ref, cb_ref,
                        cg_ref, cbeta_ref, wout_ref, g_ref, b_ref, y_ref, spo_ref, sco_ref):
    aw = ps_ref.shape[-1]
    bw = cw_ref.shape[-1]
    taps = cw_ref.shape[0]
    nph = sp_ref.shape[0]
    nch = sc_ref.shape[0]
    x = x_ref[...]
    proj = _dot(x.astype(BF16), win_ref[...])
    a = proj[:, :aw]
    u = proj[:, aw:aw + bw] * jax.nn.sigmoid(proj[:, aw + bw:])

    pos = jnp.full((x.shape[0], 1), first_pos, I32)
    ya = _pool_groups(a, lambda k, c0, gw: sp_ref[nph - k, :, c0:c0 + gw], pos, pw_ref)
    ya = ya * ps_ref[...]

    acc = u * cw_ref[taps - 1:taps, :]
    for k in range(taps - 1):
        acc = acc + sc_ref[k + nch - (taps - 1)] * cw_ref[k:k + 1, :]
    yb = _silu(_ln(acc + cb_ref[...], cg_ref[...], cbeta_ref[...]))

    mix = _dot(ya.astype(BF16), wout_ref[0:aw, :]) + _dot(yb.astype(BF16), wout_ref[aw:, :])
    y_ref[...] = _ln(alpha * x + mix, g_ref[...], b_ref[...])
    for j in range(nph - 1):
        spo_ref[j] = sp_ref[j + 1]
    spo_ref[nph - 1] = a
    for j in range(nch - 1):
        sco_ref[j] = sc_ref[j + 1]
    sco_ref[nch - 1] = u


def _even_sample(alpha, first_pos, x, sp_t, sc_t, win, pw, ps, cw, cb, cg, cbeta, wout, g, b):
    ins = (x, sp_t, sc_t, win, pw, ps, cw, cb, cg, cbeta, wout, g, b)
    return pl.pallas_call(
        functools.partial(_even_sample_kernel, alpha, first_pos),
        grid=(1,),
        in_specs=[_const_spec(c.shape) for c in ins],
        out_specs=[_const_spec(x.shape), _const_spec(sp_t.shape), _const_spec(sc_t.shape)],
        out_shape=[jax.ShapeDtypeStruct(x.shape, F32), jax.ShapeDtypeStruct(sp_t.shape, F32),
                   jax.ShapeDtypeStruct(sc_t.shape, F32)],
        compiler_params=_cparams(("arbitrary",)),
        name="even_mixer_sample",
    )(*ins)


def _ffn_kernel(alpha, x_ref, w1_ref, w3_ref, w2_ref, g_ref, b_ref, y_ref, h_ref):
    x = x_ref[...]
    xb = x.astype(BF16)
    ff = w1_ref.shape[1]
    for c in range(0, ff, FFN_CHUNK):
        h1 = _dot(xb, w1_ref[:, c:c + FFN_CHUNK])
        h3 = _dot(xb, w3_ref[:, c:c + FFN_CHUNK])
        h_ref[:, c:c + FFN_CHUNK] = (_silu(h1) * h3).astype(BF16)
    y = _dot(h_ref[...], w2_ref[...])
    y_ref[...] = _ln(alpha * x + y, g_ref[...], b_ref[...])


def _ffn(alpha, x, w1, w3, w2, g, b):
    n, d = x.shape
    tm = min(TOKEN_TILE, n)
    ff = w1.shape[1]
    assert ff % FFN_CHUNK == 0 and n % tm == 0
    return pl.pallas_call(
        functools.partial(_ffn_kernel, alpha),
        grid=(n // tm,),
        in_specs=[pl.BlockSpec((tm, d), lambda i: (i, 0)),
                  _resident_spec(w1.shape), _resident_spec(w3.shape), _resident_spec(w2.shape),
                  _const_spec(g.shape), _const_spec(b.shape)],
        out_specs=pl.BlockSpec((tm, d), lambda i: (i, 0)),
        out_shape=jax.ShapeDtypeStruct((n, d), F32),
        scratch_shapes=[pltpu.VMEM((tm, ff), BF16)],
        compiler_params=_cparams(("arbitrary",)),
        name="ffn_swiglu",
    )(x, w1, w3, w2, g, b)


def _head_select_mats(n_heads, head_dim):
    hw = V7X_LANES
    one_col = 3 * n_heads
    sq = np.zeros((hw, n_heads * hw), np.float32)
    sk = np.zeros((hw, n_heads * hw), np.float32)
    for h in range(n_heads):
        o = h * hw + head_dim
        for part in range(3):
            sq[part * n_heads + h, o + part] = 1.0
            sq[one_col, o + 3 + part] = 1.0
            sk[one_col, o + part] = 1.0
            sk[part * n_heads + h, o + 3 + part] = -1.0
    return jnp.asarray(sq, BF16), jnp.asarray(sk, BF16)


def _odd_prompt_kernel(n_heads, head_dim, x_ref, wqkv_ref, wf_ref, whbc_ref, fb_ref, cw_ref, tri_ref,
                       sq_ref, sk_ref, k_ref, v_ref, lf_ref, qa_ref, ka_ref, va_ref, yd_ref, cd_ref,
                       gext, fcarry):
    t = pl.program_id(1)
    tT = x_ref.shape[0]
    cwid = n_heads * head_dim
    dwid = cw_ref.shape[-1]
    taps = cw_ref.shape[0]
    GH = CONV_D_HIST_ROWS
    hw = V7X_LANES

    @pl.when(t == 0)
    def _():
        gext[0:GH, :] = jnp.zeros((GH, dwid), F32)
        fcarry[...] = jnp.zeros(fcarry.shape, F32)

    @pl.when(t > 0)
    def _():
        gext[0:GH, :] = gext[tT:tT + GH, :]

    xb = x_ref[...].astype(BF16)
    qkv = _dot(xb, wqkv_ref[...])
    q = qkv[:, :cwid]
    k = qkv[:, cwid:2 * cwid]
    v = qkv[:, 2 * cwid:]
    k_ref[...] = k
    v_ref[...] = v

    lane = lax.broadcasted_iota(I32, (tT, hw), 1)
    fl = _dot(xb, wf_ref[...])
    logf = jnp.where(lane < n_heads, _log_sigmoid(fl + fb_ref[...]), 0.0)
    lf_ref[...] = logf[:, :n_heads]

    tri = tri_ref[...]
    l_hi, l_mid, l_lo = _split3(logf)
    F = _dot(tri, l_hi) + _dot(tri, l_mid) + _dot(tri, l_lo) + fcarry[...]
    fcarry[...] = F[tT - 1:tT, :]

    f_hi, f_mid, f_lo = _split3(F * LOG2E)
    fparts = (f_hi.astype(F32) + pltpu.roll(f_mid.astype(F32), n_heads, axis=1)
              + pltpu.roll(f_lo.astype(F32), 2 * n_heads, axis=1)
              + jnp.where(lane == 3 * n_heads, 1.0, 0.0)).astype(BF16)
    xq = _dot(fparts, sq_ref[...])
    xk = _dot(fparts, sk_ref[...])
    vone = jnp.where(lane == head_dim, 1.0, 0.0)
    scale = head_dim ** -0.5 * LOG2E
    for h in range(n_heads):
        p = (h * head_dim) // hw
        qp = q[:, p * hw:(p + 1) * hw] * scale
        kp = k[:, p * hw:(p + 1) * hw]
        vp = v[:, p * hw:(p + 1) * hw]
        if (h * head_dim) % hw:
            sh = hw - (h * head_dim) % hw
            qp = pltpu.roll(qp, sh, axis=1)
            kp = pltpu.roll(kp, sh, axis=1)
            vp = pltpu.roll(vp, sh, axis=1)
        qa_ref[h] = jnp.where(lane < head_dim, qp, xq[:, h * hw:(h + 1) * hw]).astype(BF16)
        ka_ref[h] = jnp.where(lane < head_dim, kp, xk[:, h * hw:(h + 1) * hw]).astype(BF16)
        va_ref[h] = jnp.where(lane < head_dim, vp, vone).astype(BF16)

    hbc = _dot(xb, whbc_ref[...])
    hh = hbc[:, :dwid]
    bg = hbc[:, dwid:2 * dwid]
    cg = hbc[:, 2 * dwid:]
    g = cg * hh
    gext[GH:, :] = g
    conv = g * cw_ref[taps - 1:taps, :]
    for kk in range(taps - 1):
        back = taps - 1 - kk
        conv = conv + gext[GH - back:GH - back + tT, :] * cw_ref[kk:kk + 1, :]
    yd_ref[...] = (bg * conv).astype(BF16)
    cd_ref[...] = gext[tT:tT + GH, :]


def _odd_prompt(n_heads, head_dim, x, wqkv, wf, whbc, fb, cw):
    B, T, D = x.shape
    tT = min(TOKEN_TILE, T)
    cwid = n_heads * head_dim
    dwid = cw.shape[-1]
    hw = V7X_LANES
    GH = CONV_D_HIST_ROWS
    tri = jnp.tri(tT, dtype=BF16)
    sq, sk = _head_select_mats(n_heads, head_dim)
    consts = (wqkv, wf, whbc, fb, cw, tri, sq, sk)
    tok = lambda w: pl.BlockSpec((None, tT, w), lambda bi, ti: (bi, ti, 0))
    head = pl.BlockSpec((None, n_heads, tT, hw), lambda bi, ti: (bi, 0, ti, 0))
    return pl.pallas_call(
        functools.partial(_odd_prompt_kernel, n_heads, head_dim),
        grid=(B, T // tT),
        in_specs=[tok(D)] + [_const_spec(c.shape) for c in consts],
        out_specs=[tok(cwid), tok(cwid), tok(n_heads), head, head, head, tok(dwid),
                   pl.BlockSpec((None, GH, dwid), lambda bi, ti: (bi, 0, 0))],
        out_shape=[jax.ShapeDtypeStruct((B, T, cwid), F32), jax.ShapeDtypeStruct((B, T, cwid), F32),
                   jax.ShapeDtypeStruct((B, T, n_heads), F32),
                   jax.ShapeDtypeStruct((B, n_heads, T, hw), BF16),
                   jax.ShapeDtypeStruct((B, n_heads, T, hw), BF16),
                   jax.ShapeDtypeStruct((B, n_heads, T, hw), BF16),
                   jax.ShapeDtypeStruct((B, T, dwid), BF16),
                   jax.ShapeDtypeStruct((B, GH, dwid), F32)],
        scratch_shapes=[pltpu.VMEM((GH + tT, dwid), F32), pltpu.VMEM((1, hw), F32)],
        compiler_params=_cparams(("arbitrary", "arbitrary")),
        name="odd_proj_prompt",
    )(x, *consts)


def _odd_sample_kernel(n_heads, head_dim, x_ref, sd_ref, wqkv_ref, wf_ref, whbc_ref, fb_ref, cw_ref,
                       q_ref, k_ref, v_ref, lf_ref, yd_ref, sdo_ref):
    cwid = n_heads * head_dim
    dwid = cw_ref.shape[-1]
    taps = cw_ref.shape[0]
    nh = sd_ref.shape[0]
    xb = x_ref[...].astype(BF16)
    qkv = _dot(xb, wqkv_ref[...])
    q_ref[...] = qkv[:, :cwid]
    k_ref[...] = qkv[:, cwid:2 * cwid]
    v_ref[...] = qkv[:, 2 * cwid:]
    fl = _dot(xb, wf_ref[...])
    lf_ref[...] = _log_sigmoid(fl + fb_ref[...])
    hbc = _dot(xb, whbc_ref[...])
    g = hbc[:, 2 * dwid:] * hbc[:, :dwid]
    conv = g * cw_ref[taps - 1:taps, :]
    for kk in range(taps - 1):
        conv = conv + sd_ref[kk + nh - (taps - 1)] * cw_ref[kk:kk + 1, :]
    yd_ref[...] = (hbc[:, dwid:2 * dwid] * conv).astype(BF16)
    for j in range(nh - 1):
        sdo_ref[j] = sd_ref[j + 1]
    sdo_ref[nh - 1] = g


def _odd_sample(n_heads, head_dim, x, sd_t, wqkv, wf, whbc, fb, cw):
    n = x.shape[0]
    cwid = n_heads * head_dim
    dwid = cw.shape[-1]
    ins = (x, sd_t, wqkv, wf, whbc, fb, cw)
    shapes = [((n, cwid), F32), ((n, cwid), F32), ((n, cwid), F32), ((n, V7X_LANES), F32),
              ((n, dwid), BF16), (sd_t.shape, F32)]
    return pl.pallas_call(
        functools.partial(_odd_sample_kernel, n_heads, head_dim),
        grid=(1,),
        in_specs=[_const_spec(c.shape) for c in ins],
        out_specs=[_const_spec(s) for s, _ in shapes],
        out_shape=[jax.ShapeDtypeStruct(s, d) for s, d in shapes],
        compiler_params=_cparams(("arbitrary",)),
        name="odd_proj_sample",
    )(*ins)


def _fox_prompt_kernel(head_dim, n_side, qa_ref, ka_ref, va_ref, *refs):
    side_in = refs[:n_side]
    o_ref = refs[n_side]
    side_out = refs[n_side + 1:2 * n_side + 1]
    sa, sb = refs[2 * n_side + 1:]
    for src, dst in zip(side_in, side_out):
        dst[...] = src[...].astype(BF16)

    i = pl.program_id(2)
    tq = qa_ref.shape[1]
    hw = V7X_LANES
    lane = lax.broadcasted_iota(I32, (tq, hw), 1)
    row = lax.broadcasted_iota(I32, (tq, tq), 0)
    col = lax.broadcasted_iota(I32, (tq, tq), 1)
    nh = qa_ref.shape[0]
    qs = [qa_ref[hh] for hh in range(nh)]

    def scores(j, hh):
        return _dot_nt(qs[hh], ka_ref[hh, pl.ds(pl.multiple_of(j * tq, tq), tq), :])

    def update(j, hh, m, acc, s):
        m_new = jnp.maximum(m, jnp.max(s, axis=-1, keepdims=True))
        p = jnp.exp2(s - m_new)
        vt = va_ref[hh, pl.ds(pl.multiple_of(j * tq, tq), tq), :]
        return m_new, acc * jnp.exp2(m - m_new) + _dot(p.astype(BF16), vt)

    def step(j, carry, src, dst):
        new = []
        for hh in range(nh):
            dst[hh] = scores(j + 1, hh)
            new.append(update(j, hh, *carry[hh], src[hh]))
        return tuple(new)

    for hh in range(nh):
        sa[hh] = scores(0, hh)

    def pair(u, carry):
        return step(2 * u + 1, step(2 * u, carry, sa, sb), sb, sa)

    odd = i % 2
    carry = tuple((jnp.full((tq, 1), NEG, F32), jnp.zeros((tq, hw), F32)) for _ in range(nh))
    carry = lax.fori_loop(0, i // 2, pair, carry)
    carry = lax.fori_loop(0, odd, lambda _, c: step(i - 1, c, sa, sb), carry)
    def finish(src):
        o = None
        for hh in range(nh):
            _, acc = update(i, hh, *carry[hh], jnp.where(col <= row, src[hh], NEG))
            denom = jnp.sum(jnp.where(lane == head_dim, acc, 0.0), axis=-1, keepdims=True)
            out = acc / denom
            o = out if hh == 0 else jnp.where(lane < hh * head_dim, o, pltpu.roll(out, hh * head_dim, axis=1))
        o_ref[...] = o.astype(o_ref.dtype)

    @pl.when(odd == 0)
    def _():
        finish(sa)

    @pl.when(odd == 1)
    def _():
        finish(sb)


def _fox_prompt(head_dim, qa, ka, va, side=()):
    B, H, T, hw = qa.shape
    tq = min(TOKEN_TILE, T)
    hp = hw // head_dim
    grid = (B, H // hp, T // tq)
    n_steps = grid[0] * grid[1] * grid[2]

    def side_spec(w):
        rows = w.shape[0] // n_steps
        assert rows * n_steps == w.shape[0] and rows % BF16_ROWS_PER_TILE == 0
        return pl.BlockSpec((rows, w.shape[1]), lambda b, p, i: ((b * grid[1] + p) * grid[2] + i, 0))

    side_specs = [side_spec(w) for w in side]
    return pl.pallas_call(
        functools.partial(_fox_prompt_kernel, head_dim, len(side)),
        grid=grid,
        in_specs=[pl.BlockSpec((None, hp, tq, hw), lambda b, p, i: (b, p, i, 0)),
                  pl.BlockSpec((None, hp, T, hw), lambda b, p, i: (b, p, 0, 0)),
                  pl.BlockSpec((None, hp, T, hw), lambda b, p, i: (b, p, 0, 0))] + side_specs,
        out_specs=[pl.BlockSpec((None, tq, hw), lambda b, p, i: (b, i, p))] + side_specs,
        out_shape=[jax.ShapeDtypeStruct((B, T, H * head_dim), BF16)]
        + [jax.ShapeDtypeStruct(w.shape, BF16) for w in side],
        scratch_shapes=[pltpu.VMEM((hp, tq, tq), F32), pltpu.VMEM((hp, tq, tq), F32)],
        compiler_params=_cparams(("arbitrary", "arbitrary", "arbitrary")),
        name="fox_attention_prompt",
    )(qa, ka, va, *side)


def _fox_sample_kernel(npp, pt_ref, *refs):
    k_refs = refs[0:npp]
    v_refs = refs[npp:2 * npp]
    lf_refs = refs[2 * npp:3 * npp]
    (qb_ref, kn_ref, vn_ref, lfn_ref, u_ref, o_ref, m_s, l_s, acc_s, carry_s) = refs[3 * npp:]
    i = pl.program_id(1)
    n_steps = pl.num_programs(1)
    H, dh, page = k_refs[0].shape
    cw = H * dh

    @pl.when(i == 0)
    def _():
        m_s[...] = jnp.full(m_s.shape, NEG, F32)
        l_s[...] = jnp.zeros(l_s.shape, F32)
        acc_s[...] = jnp.zeros(acc_s.shape, F32)
        carry_s[...] = lfn_ref[...]

    lf_all = jnp.concatenate([r[...] for r in lf_refs], axis=0)
    l_hi, l_mid, l_lo = _split3(lf_all)
    u = u_ref[...]
    g_all = _dot(l_hi, u) + _dot(l_mid, u) + _dot(l_lo, u)
    tot = jnp.sum(lf_all, axis=1, keepdims=True)
    later = carry_s[...]
    decay = [None] * npp
    for j in reversed(range(npp)):
        decay[j] = later
        later = later + tot[j * H:(j + 1) * H, :]
    carry_s[...] = later

    qb = qb_ref[...]
    qbb = qb.astype(BF16)
    s_pages = []
    for j in range(npp):
        k2 = k_refs[j][...].reshape(cw, page).astype(BF16)
        s_pages.append(_dot(qbb, k2) + g_all[j * H:(j + 1) * H, :] + decay[j])

    mx = s_pages[0]
    for j in range(1, npp):
        mx = jnp.maximum(mx, s_pages[j])
    m = m_s[...]
    m_new = jnp.maximum(m, jnp.max(mx, axis=1, keepdims=True))
    c = jnp.exp(m - m_new)
    p_pages = [jnp.exp(s - m_new) for s in s_pages]
    psum = p_pages[0]
    for j in range(1, npp):
        psum = psum + p_pages[j]
    l_s[...] = l_s[...] * c + psum
    m_s[...] = m_new
    acc = acc_s[...] * c
    for j in range(npp):
        v2 = v_refs[j][...].reshape(cw, page).astype(BF16)
        acc = acc + _dot_nt(p_pages[j].astype(BF16), v2)
    acc_s[...] = acc

    @pl.when(i == n_steps - 1)
    def _():
        s_new = jnp.sum(qb * kn_ref[...], axis=1, keepdims=True)
        m_fin = jnp.maximum(m_new, s_new)
        cf = jnp.exp(m_new - m_fin)
        p_new = jnp.exp(s_new - m_fin)
        denom = jnp.sum(l_s[...], axis=1, keepdims=True) * cf + p_new
        o_ref[...] = (acc * cf + p_new * vn_ref[...]) / denom


def _fox_sample(page_table, kview, vview, lfview, qb, kn, vn, lfn):
    nb, n_pages = page_table.shape
    _, H, dh, page = kview.shape
    npp = min(PAGES_PER_STEP, n_pages)
    assert n_pages % npp == 0
    n_steps = n_pages // npp
    u = jnp.asarray(np.tril(np.ones((page, page), np.float32), -1), BF16)

    def page_spec(shape, j):
        nd = len(shape)
        return pl.BlockSpec((None,) + tuple(shape),
                            lambda b, i, pt, j=j: (pt[b, n_pages - npp * (i + 1) + j],) + (0,) * nd)

    seq = lambda shape: pl.BlockSpec((None,) + tuple(shape), lambda b, i, pt: (b,) + (0,) * len(shape))
    in_specs = ([page_spec((H, dh, page), j) for j in range(npp)]
                + [page_spec((H, dh, page), j) for j in range(npp)]
                + [page_spec((H, page), j) for j in range(npp)]
                + [seq((H, H * dh)), seq((1, H * dh)), seq((1, H * dh)), seq((H, 1)),
                   pl.BlockSpec(u.shape, lambda b, i, pt: (0, 0))])
    grid_spec = pltpu.PrefetchScalarGridSpec(
        num_scalar_prefetch=1, grid=(nb, n_steps), in_specs=in_specs,
        out_specs=seq((H, H * dh)),
        scratch_shapes=[pltpu.VMEM((H, 1), F32), pltpu.VMEM((H, page), F32), pltpu.VMEM((H, H * dh), F32),
                        pltpu.VMEM((H, 1), F32)])
    return pl.pallas_call(
        functools.partial(_fox_sample_kernel, npp),
        grid_spec=grid_spec,
        out_shape=jax.ShapeDtypeStruct((nb, H, H * dh), F32),
        compiler_params=_cparams(("arbitrary", "arbitrary")),
        name="fox_attention_sample",
    )(page_table, *([kview] * npp), *([vview] * npp), *([lfview] * npp), qb, kn, vn, lfn, u)


def _top2(logits, axis, n_experts):
    idx = lax.broadcasted_iota(I32, logits.shape, axis)
    big = logits.shape[axis]
    lg = jnp.where(idx < n_experts, logits, -jnp.inf)
    v1 = jnp.max(lg, axis=axis, keepdims=True)
    i1 = jnp.min(jnp.where(lg == v1, idx, big), axis=axis, keepdims=True)
    lg2 = jnp.where(idx == i1, -jnp.inf, lg)
    v2 = jnp.max(lg2, axis=axis, keepdims=True)
    i2 = jnp.min(jnp.where(lg2 == v2, idx, big), axis=axis, keepdims=True)
    return idx, i1, i2, v1, v2


def _odd_out_kernel(alpha, n_experts, n_prompt_tiles, op_ref, ydp_ref, xp_ref, os_ref, yds_ref, xs_ref,
                    wout_ref, g_ref, b_ref, rw_ref, x3p_ref, x3s_ref, x3b_ref, lg_ref, cnt_ref):
    i = pl.program_id(0)
    cwid = op_ref.shape[1]

    def rows(o_ref, yd_ref, x_ref, x3_ref):
        n = o_ref.shape[0]
        mix = _dot(o_ref[...].astype(BF16), wout_ref[0:cwid, :]) + _dot(yd_ref[...], wout_ref[cwid:, :])
        x3 = _ln(alpha * x_ref[...] + mix, g_ref[...], b_ref[...])
        x3b = x3.astype(BF16)
        logits = _dot(x3b, rw_ref[...])
        x3_ref[...] = x3
        if n < x3b_ref.shape[0]:
            x3b_ref[...] = jnp.zeros(x3b_ref.shape, BF16)
            lg_ref[...] = jnp.zeros(lg_ref.shape, F32)
        x3b_ref[0:n, :] = x3b
        lg_ref[0:n, :] = logits
        idx, i1, i2, _, _ = _top2(logits, 1, n_experts)
        mask = jnp.where((idx == i1) | (idx == i2), 1.0, 0.0)
        cnt_ref[...] = jnp.sum(mask, axis=0, keepdims=True)

    @pl.when(i < n_prompt_tiles)
    def _():
        rows(op_ref, ydp_ref, xp_ref, x3p_ref)

    @pl.when(i >= n_prompt_tiles)
    def _():
        rows(os_ref, yds_ref, xs_ref, x3s_ref)


def _odd_out(alpha, n_experts, o_p, yd_p, x_p, o_s, yd_s, x_s, wout, g, b, rw):
    n, d = x_p.shape
    ns = x_s.shape[0]
    tm = TOKEN_TILE
    npt = n // tm
    ntiles = npt + 1
    hw = V7X_LANES
    prow = lambda w: pl.BlockSpec((tm, w), lambda i: (jnp.minimum(i, npt - 1), 0))
    return pl.pallas_call(
        functools.partial(_odd_out_kernel, alpha, n_experts, npt),
        grid=(ntiles,),
        in_specs=[prow(o_p.shape[1]), prow(yd_p.shape[1]), prow(d),
                  _const_spec(o_s.shape), _const_spec(yd_s.shape), _const_spec(x_s.shape)]
        + [_const_spec(c.shape) for c in (wout, g, b, rw)],
        out_specs=[prow(d), _const_spec((ns, d)),
                   pl.BlockSpec((tm, d), lambda i: (i, 0)),
                   pl.BlockSpec((tm, hw), lambda i: (i, 0)),
                   pl.BlockSpec((None, 1, hw), lambda i: (i, 0, 0))],
        out_shape=[jax.ShapeDtypeStruct((n, d), F32), jax.ShapeDtypeStruct((ns, d), F32),
                   jax.ShapeDtypeStruct((ntiles * tm, d), BF16),
                   jax.ShapeDtypeStruct((ntiles * tm, hw), F32),
                   jax.ShapeDtypeStruct((ntiles, 1, hw), F32)],
        compiler_params=_cparams(("arbitrary",)),
        name="odd_out_router",
    )(o_p, yd_p, x_p, o_s, yd_s, x_s, wout, g, b, rw)


def _pieces(n, make_copy):
    ratio = SEG_BIG // SEG_PAD
    nbig = n // ratio

    def big(j, _):
        make_copy(j * SEG_BIG, SEG_BIG)
        return 0

    def small(j, _):
        make_copy(nbig * SEG_BIG + j * SEG_PAD, SEG_PAD)
        return 0

    lax.fori_loop(0, nbig, big, 0)
    lax.fori_loop(0, n - nbig * ratio, small, 0)


def _segment_copies(tile, n_experts, npc_ref, loff_ref, goff_ref, make_copy):
    for e in range(n_experts):
        k = tile * n_experts + e
        lo, go = loff_ref[k], goff_ref[k]
        _pieces(npc_ref[k], lambda off, n, lo=lo, go=go: make_copy(
            pl.multiple_of(lo + off, SEG_PAD), pl.multiple_of(go + off, SEG_PAD), n))


def _dispatch_kernel(n_experts, n_valid, npc_ref, loff_ref, goff_ref, zoff_ref, zcnt_ref, x_ref, lt_ref, lofft_ref,
                     triu_ref, xs_ref, gs_ref, stage, gstage, zx, zg, sem):
    tile = pl.program_id(0)
    tt = x_ref.shape[0]
    rl = stage.shape[0]
    sub, i1, i2, v1, v2 = _top2(lt_ref[...], 0, n_experts)
    valid = (tile * tt + lax.broadcasted_iota(I32, (1, tt), 1)) < n_valid
    sel1 = (sub == i1) & valid
    sel2 = (sub == i2) & valid
    mask = jnp.where(sel1 | sel2, 1.0, 0.0)
    rank = _dot(mask.astype(BF16), triu_ref[...])
    loc = rank + jnp.concatenate([lofft_ref[...]] * (tt // V7X_LANES), axis=1)
    lr1 = jnp.sum(jnp.where(sel1, loc, 0.0), axis=0, keepdims=True).astype(I32)
    lr2 = jnp.sum(jnp.where(sel2, loc, 0.0), axis=0, keepdims=True).astype(I32)
    lr1 = jnp.where(valid, lr1, -1)
    lr2 = jnp.where(valid, lr2, -1)
    r = lax.broadcasted_iota(I32, (rl, tt), 0)
    hit1 = r == lr1
    hit2 = r == lr2
    onehot = jnp.where(hit1 | hit2, 1.0, 0.0).astype(BF16)
    stage[...] = _dot(onehot, x_ref[...]).astype(BF16)
    e21 = jnp.exp(v2 - v1)
    g1 = 1.0 / (1.0 + e21)
    g2 = e21 / (1.0 + e21)
    gs = jnp.sum(jnp.where(hit1, g1, 0.0) + jnp.where(hit2, g2, 0.0), axis=1, keepdims=True)
    gstage[...] = jnp.broadcast_to(gs, gstage.shape)

    def copies(fn):
        _segment_copies(tile, n_experts, npc_ref, loff_ref, goff_ref,
                        lambda lo, go, n: fn(pltpu.make_async_copy(stage.at[pl.ds(lo, n), :],
                                                                  xs_ref.at[pl.ds(go, n), :], sem.at[0])))
        _segment_copies(tile, n_experts, npc_ref, loff_ref, goff_ref,
                        lambda lo, go, n: fn(pltpu.make_async_copy(gstage.at[pl.ds(lo, n), :],
                                                                  gs_ref.at[pl.ds(go, n), :], sem.at[1])))

    copies(lambda cp: cp.start())
    copies(lambda cp: cp.wait())

    @pl.when(tile == pl.num_programs(0) - 1)
    def _():
        zx[...] = jnp.zeros(zx.shape, BF16)
        zg[...] = jnp.zeros(zg.shape, F32)

        def fills(fn):
            for k in range(n_experts + 1):
                start = zoff_ref[k]

                def one(off, n, start=start):
                    dst = pl.ds(pl.multiple_of(start + off, SEG_PAD), n)
                    fn(pltpu.make_async_copy(zx.at[pl.ds(0, n), :], xs_ref.at[dst, :], sem.at[0]))
                    fn(pltpu.make_async_copy(zg.at[pl.ds(0, n), :], gs_ref.at[dst, :], sem.at[1]))

                _pieces(zcnt_ref[k], one)

        fills(lambda cp: cp.start())
        fills(lambda cp: cp.wait())


def _dispatch(n_experts, n_valid, rmax, npc, loff, goff, zoff, zcnt, x3b, lt, lofft):
    ntot, d = x3b.shape
    tt = TOKEN_TILE
    triu = jnp.asarray(np.triu(np.ones((tt, tt), np.float32), 1), BF16)
    grid_spec = pltpu.PrefetchScalarGridSpec(
        num_scalar_prefetch=5, grid=(ntot // tt,),
        in_specs=[pl.BlockSpec((tt, d), lambda i, *_: (i, 0)),
                  pl.BlockSpec((n_experts, tt), lambda i, *_: (0, i)),
                  pl.BlockSpec((None, n_experts, V7X_LANES), lambda i, *_: (i, 0, 0)),
                  pl.BlockSpec(triu.shape, lambda i, *_: (0, 0))],
        out_specs=[pl.BlockSpec(memory_space=pl.ANY), pl.BlockSpec(memory_space=pl.ANY)],
        scratch_shapes=[pltpu.VMEM((LOCAL_ROWS, d), BF16), pltpu.VMEM((LOCAL_ROWS, V7X_LANES), F32),
                        pltpu.VMEM((SEG_BIG, d), BF16), pltpu.VMEM((SEG_BIG, V7X_LANES), F32),
                        pltpu.SemaphoreType.DMA((2,))])
    return pl.pallas_call(
        functools.partial(_dispatch_kernel, n_experts, n_valid),
        grid_spec=grid_spec,
        out_shape=[jax.ShapeDtypeStruct((rmax, d), BF16), jax.ShapeDtypeStruct((rmax, V7X_LANES), F32)],
        compiler_params=_cparams(("arbitrary",)),
        name="moe_dispatch",
    )(npc, loff, goff, zoff, zcnt, x3b, lt, lofft, triu)


def _experts_kernel(te_ref, tv_ref, xr_ref, x_ref, gate_ref, w1_ref, w3_ref, w2_ref, yh_ref, yl_ref, h_ref):
    del te_ref, xr_ref
    i = pl.program_id(0)
    c = pl.program_id(1)
    last = pl.num_programs(1) - 1
    fc = w1_ref.shape[1]
    valid = tv_ref[i] > 0

    @pl.when(valid)
    def _():
        x = x_ref[...]
        h = (_silu(_dot(x, w1_ref[...])) * _dot(x, w3_ref[...])).astype(BF16)
        h_ref[:, pl.ds(pl.multiple_of(c * fc, fc), fc)] = h

        @pl.when(c == last)
        def _():
            y = _dot(h_ref[...], w2_ref[...]) * gate_ref[:, 0:1]
            hi = y.astype(BF16)
            yh_ref[...] = hi
            yl_ref[...] = (y - hi.astype(F32)).astype(BF16)

    @pl.when(jnp.logical_not(valid) & (c == last))
    def _():
        yh_ref[...] = jnp.zeros(yh_ref.shape, BF16)
        yl_ref[...] = jnp.zeros(yl_ref.shape, BF16)


def _experts(te, tv, xr, xs, gates, w1, w3, w2):
    rmax, d = xs.shape
    tm = EXPERT_TILE
    ff = w1.shape[-1]
    fc = min(EXPERT_FF_CHUNK, ff)
    nch = ff // fc
    assert ff % fc == 0 and rmax % tm == 0

    def chunk(i, c, tv):
        return jnp.where(tv[i] > 0, c, nch - 1)

    grid_spec = pltpu.PrefetchScalarGridSpec(
        num_scalar_prefetch=3, grid=(rmax // tm, nch),
        in_specs=[pl.BlockSpec((tm, d), lambda i, c, te, tv, xr: (xr[i], 0)),
                  pl.BlockSpec((tm, V7X_LANES), lambda i, c, te, tv, xr: (xr[i], 0)),
                  pl.BlockSpec((None, d, fc), lambda i, c, te, tv, xr: (te[i], 0, chunk(i, c, tv))),
                  pl.BlockSpec((None, d, fc), lambda i, c, te, tv, xr: (te[i], 0, chunk(i, c, tv))),
                  pl.BlockSpec((None, ff, d), lambda i, c, te, tv, xr: (te[i], 0, 0))],
        out_specs=[pl.BlockSpec((tm, d), lambda i, c, te, tv, xr: (i, 0)),
                   pl.BlockSpec((tm, d), lambda i, c, te, tv, xr: (i, 0))],
        scratch_shapes=[pltpu.VMEM((tm, ff), BF16)])
    return pl.pallas_call(
        _experts_kernel,
        grid_spec=grid_spec,
        out_shape=[jax.ShapeDtypeStruct((rmax, d), BF16), jax.ShapeDtypeStruct((rmax, d), BF16)],
        compiler_params=_cparams(("arbitrary", "arbitrary")),
        name="moe_experts",
    )(te, tv, xr, xs, gates, w1, w3, w2)


def _combine_kernel(alpha, n_experts, n_valid, n_prompt_tiles, npc_ref, loff_ref, goff_ref, lg_ref, xp_ref,
                    xs_ref, loffr_ref, tril_ref, g_ref, b_ref, yh_hbm, yl_hbm, yp_ref, ysm_ref,
                    sth, stl, sem):
    tile = pl.program_id(0)
    tt = lg_ref.shape[0]
    rl = sth.shape[0]
    d = sth.shape[1]

    def copies(fn):
        _segment_copies(tile, n_experts, npc_ref, loff_ref, goff_ref,
                        lambda lo, go, n: fn(pltpu.make_async_copy(yh_hbm.at[pl.ds(go, n), :],
                                                                  sth.at[pl.ds(lo, n), :], sem.at[0])))
        _segment_copies(tile, n_experts, npc_ref, loff_ref, goff_ref,
                        lambda lo, go, n: fn(pltpu.make_async_copy(yl_hbm.at[pl.ds(go, n), :],
                                                                  stl.at[pl.ds(lo, n), :], sem.at[1])))

    copies(lambda cp: cp.start())

    last = tile * n_experts + n_experts - 1
    used = loff_ref[last] // SEG_PAD + npc_ref[last]

    def clear(j, _):
        o = pl.multiple_of(j * SEG_PAD, SEG_PAD)
        sth[pl.ds(o, SEG_PAD), :] = jnp.zeros((SEG_PAD, d), BF16)
        stl[pl.ds(o, SEG_PAD), :] = jnp.zeros((SEG_PAD, d), BF16)
        return 0

    lax.fori_loop(used, rl // SEG_PAD, clear, 0)

    lane, i1, i2, _, _ = _top2(lg_ref[...], 1, n_experts)
    valid = (tile * tt + lax.broadcasted_iota(I32, (tt, 1), 0)) < n_valid
    sel1 = (lane == i1) & valid
    sel2 = (lane == i2) & valid
    mask = jnp.where(sel1 | sel2, 1.0, 0.0)
    rank = _dot(tril_ref[...], mask.astype(BF16))
    loc = rank + jnp.concatenate([loffr_ref[...]] * (tt // V7X_SUBLANES), axis=0)
    lr1 = jnp.sum(jnp.where(sel1, loc, 0.0), axis=1, keepdims=True).astype(I32)
    lr2 = jnp.sum(jnp.where(sel2, loc, 0.0), axis=1, keepdims=True).astype(I32)
    lr1 = jnp.where(valid, lr1, -1)
    lr2 = jnp.where(valid, lr2, -1)
    r = lax.broadcasted_iota(I32, (tt, rl), 1)
    pick = jnp.where((r == lr1) | (r == lr2), 1.0, 0.0).astype(BF16)

    copies(lambda cp: cp.wait())
    moe = _dot(pick, sth[...]) + _dot(pick, stl[...])

    @pl.when(tile < n_prompt_tiles)
    def _():
        yp_ref[...] = _ln(alpha * xp_ref[...] + moe, g_ref[...], b_ref[...])

    @pl.when(tile >= n_prompt_tiles)
    def _():
        ns = ysm_ref.shape[0]
        ysm_ref[...] = _ln(alpha * xs_ref[...] + moe[0:ns, :], g_ref[...], b_ref[...])


def _combine(alpha, n_experts, n_valid, npc, loff, goff, logits, x3p, x3s, loffr, g, b, yh, yl):
    npr, d = x3p.shape
    ns = x3s.shape[0]
    tt = TOKEN_TILE
    npt = npr // tt
    ntiles = logits.shape[0] // tt
    tril = jnp.asarray(np.tril(np.ones((tt, tt), np.float32), -1), BF16)
    grid_spec = pltpu.PrefetchScalarGridSpec(
        num_scalar_prefetch=3, grid=(ntiles,),
        in_specs=[pl.BlockSpec((tt, V7X_LANES), lambda i, *_: (i, 0)),
                  pl.BlockSpec((tt, d), lambda i, *_: (jnp.minimum(i, npt - 1), 0)),
                  pl.BlockSpec((ns, d), lambda i, *_: (0, 0)),
                  pl.BlockSpec((None, V7X_SUBLANES, V7X_LANES), lambda i, *_: (i, 0, 0)),
                  pl.BlockSpec(tril.shape, lambda i, *_: (0, 0)),
                  pl.BlockSpec(g.shape, lambda i, *_: (0, 0)),
                  pl.BlockSpec(b.shape, lambda i, *_: (0, 0)),
                  pl.BlockSpec(memory_space=pl.ANY),
                  pl.BlockSpec(memory_space=pl.ANY)],
        out_specs=[pl.BlockSpec((tt, d), lambda i, *_: (jnp.minimum(i, npt - 1), 0)),
                   pl.BlockSpec((ns, d), lambda i, *_: (0, 0))],
        scratch_shapes=[pltpu.VMEM((LOCAL_ROWS, d), BF16), pltpu.VMEM((LOCAL_ROWS, d), BF16),
                        pltpu.SemaphoreType.DMA((2,))])
    return pl.pallas_call(
        functools.partial(_combine_kernel, alpha, n_experts, n_valid, npt),
        grid_spec=grid_spec,
        out_shape=[jax.ShapeDtypeStruct((npr, d), F32), jax.ShapeDtypeStruct((ns, d), F32)],
        compiler_params=_cparams(("arbitrary",)),
        name="moe_combine",
    )(npc, loff, goff, logits, x3p, x3s, loffr, tril, g, b, yh, yl)


def _routing_plan(cnt, n_experts):
    ntiles = cnt.shape[0]
    pc = (cnt + SEG_PAD - 1) // SEG_PAD * SEG_PAD
    loff = jnp.cumsum(pc, axis=1) - pc
    per_expert = jnp.sum(pc, axis=0)
    gp = (per_expert + EXPERT_TILE - 1) // EXPERT_TILE * EXPERT_TILE
    gend = jnp.cumsum(gp)
    goff = (gend - gp)[None, :] + jnp.cumsum(pc, axis=0) - pc
    rmax = TOP_K * ntiles * TOKEN_TILE + ntiles * n_experts * (SEG_PAD - 1) + n_experts * (EXPERT_TILE - 1)
    nt_max = -(-rmax // EXPERT_TILE)
    tiles_used = gend[-1] // EXPERT_TILE
    ti = jnp.arange(nt_max, dtype=I32)
    tv = (ti < tiles_used).astype(I32)
    xr = jnp.maximum(jnp.minimum(ti, tiles_used - 1), 0)
    te = jnp.sum((xr[:, None] >= (gend // EXPERT_TILE)[None, :]).astype(I32), axis=1)
    te = jnp.minimum(te, n_experts - 1)
    flat = lambda a: a.reshape(-1).astype(I32)
    rows = nt_max * EXPERT_TILE
    zoff = jnp.concatenate([gend - gp + per_expert, gend[-1:]])
    zcnt = jnp.concatenate([gp - per_expert, rows - gend[-1:]]) // SEG_PAD
    return (flat(pc // SEG_PAD), flat(loff), flat(goff), flat(zoff), flat(zcnt), loff.astype(F32), te, tv,
            xr.astype(I32), rows)


def kernel(x_prompt, x_sample, state_pool, state_conv_b, cache_k, cache_v, cache_logf, state_conv_d, page_table,
           w_in_even, pool_w, pool_scale, conv_b_w, conv_b_bias, conv_ln_g, conv_ln_b, w_out_even, ln_mix_even_g,
           ln_mix_even_b, ffn_w1, ffn_w3, ffn_w2, ln_ffn_even_g, ln_ffn_even_b, w_in_odd, forget_bias, conv_d_w,
           w_out_odd, ln_mix_odd_g, ln_mix_odd_b, router_w, moe_w1, moe_w3, moe_w2, ln_ffn_odd_g, ln_ffn_odd_b):
    assert w_in_even.shape[0] == 1 and w_in_odd.shape[0] == 1, "one even and one odd layer are supported"
    depth = w_in_even.shape[0] + w_in_odd.shape[0]
    alpha = float((2 * depth) ** 0.25)
    B, T, D = x_prompt.shape
    nb = x_sample.shape[0]
    assert x_sample.shape[1] == 1 and T % TOKEN_TILE == 0 and nb <= TOKEN_TILE
    n_heads = forget_bias.shape[-1]
    cwid = cache_k.shape[-1] * cache_k.shape[-2]
    head_dim = cache_k.shape[-1]
    dwid = conv_d_w.shape[-1]
    n_experts = router_w.shape[-1]
    past_len = page_table.shape[1] * cache_k.shape[2]
    hw = V7X_LANES
    bf = lambda w: w.astype(BF16)
    rowv = lambda w: w.reshape(1, -1)

    ew = (bf(w_in_even[0]), bf(pool_w[0]), pool_scale, conv_b_w[0], conv_b_bias, conv_ln_g, conv_ln_b,
          bf(w_out_even[0]), ln_mix_even_g, ln_mix_even_b)
    xp, a_hist, u_hist = _even_prompt(alpha, x_prompt, *ew)
    xs, pool_s_t, convb_s_t = _even_sample(
        alpha, past_len, x_sample.reshape(nb, D), jnp.swapaxes(state_pool[0], 0, 1),
        jnp.swapaxes(state_conv_b[0], 0, 1), *ew)
    nph = state_pool.shape[2]
    nch = state_conv_b.shape[2]
    pool_p = a_hist[None, :, POOL_HIST_ROWS - nph:, :]
    convb_p = u_hist[None, :, CONV_B_HIST_ROWS - nch:, :]
    pool_s = jnp.swapaxes(pool_s_t, 0, 1)[None]
    convb_s = jnp.swapaxes(convb_s_t, 0, 1)[None]

    fw = (bf(ffn_w1[0]), bf(ffn_w3[0]), bf(ffn_w2[0]), ln_ffn_even_g, ln_ffn_even_b)
    xp = _ffn(alpha, xp.reshape(B * T, D), *fw)
    xs = _ffn(alpha, xs, *fw)

    w_in = w_in_odd[0]
    wqkv = bf(w_in[:, :3 * cwid])
    wf = bf(jnp.pad(w_in[:, 3 * cwid:3 * cwid + n_heads], ((0, 0), (0, hw - n_heads))))
    whbc = bf(w_in[:, 3 * cwid + n_heads:])
    fb = jnp.pad(forget_bias, ((0, 0), (0, hw - n_heads)))
    k_p, v_p, lf_p, qa, ka, va, yd_p, cd_p = _odd_prompt(
        n_heads, head_dim, xp.reshape(B, T, D), wqkv, wf, whbc, fb, conv_d_w[0])
    ne, _, ffe = moe_w1.shape[1:]
    o_p, mw1, mw3, mw2 = _fox_prompt(head_dim, qa, ka, va, side=(
        moe_w1[0].reshape(ne * D, ffe), moe_w3[0].reshape(ne * D, ffe), moe_w2[0].reshape(ne * ffe, D)))
    mw1, mw3, mw2 = mw1.reshape(ne, D, ffe), mw3.reshape(ne, D, ffe), mw2.reshape(ne, ffe, D)

    q_s, k_s, v_s, lf_s, yd_s, convd_s_t = _odd_sample(
        n_heads, head_dim, xs, jnp.swapaxes(state_conv_d[0], 0, 1), wqkv, wf, whbc, fb, conv_d_w[0])
    head_of_col = jnp.arange(cwid, dtype=I32) // head_dim
    blockdiag = (head_of_col[None, :] == jnp.arange(n_heads, dtype=I32)[:, None]).astype(F32)
    lf_s = lf_s[:, :n_heads]
    o_s_all = _fox_sample(page_table, jnp.transpose(cache_k[0], (0, 2, 3, 1)),
                          jnp.transpose(cache_v[0], (0, 2, 3, 1)), jnp.transpose(cache_logf[0], (0, 2, 1)),
                          (q_s * head_dim ** -0.5)[:, None, :] * blockdiag[None], k_s[:, None, :], v_s[:, None, :],
                          lf_s.reshape(nb, n_heads, 1))
    o_s = jnp.sum(o_s_all * blockdiag[None], axis=1)

    n_prompt = B * T
    npt = n_prompt // TOKEN_TILE
    ntiles = npt + 1
    n_valid = n_prompt + nb
    ow = (bf(w_out_odd[0]), ln_mix_odd_g, ln_mix_odd_b,
          bf(jnp.pad(router_w[0], ((0, 0), (0, hw - n_experts)))))
    x3p, x3s, x3b, logits, cnt = _odd_out(alpha, n_experts, o_p.reshape(n_prompt, cwid),
                                          yd_p.reshape(n_prompt, dwid), xp, o_s, yd_s, xs, *ow)

    cnt = cnt[:, 0, :n_experts].astype(I32)
    npc, loff, goff, zoff, zcnt, loff_f, te, tv, xr, rmax = _routing_plan(cnt, n_experts)
    lt = jnp.transpose(logits[:, :n_experts])
    lofft = jnp.broadcast_to(loff_f[:, :, None], (ntiles, n_experts, hw))
    loffr = jnp.broadcast_to(jnp.pad(loff_f, ((0, 0), (0, hw - n_experts)))[:, None, :],
                             (ntiles, V7X_SUBLANES, hw))
    xs_sorted, gates = _dispatch(n_experts, n_valid, rmax, npc, loff, goff, zoff, zcnt, x3b, lt, lofft)
    yh, yl = _experts(te, tv, xr, xs_sorted, gates, mw1, mw3, mw2)
    y_p, y_s = _combine(alpha, n_experts, n_valid, npc, loff, goff, logits, x3p, x3s, loffr,
                        ln_ffn_odd_g, ln_ffn_odd_b, yh, yl)

    nd = state_conv_d.shape[2]
    return (y_p.reshape(B, T, D), y_s.reshape(nb, 1, D),
            pool_p, pool_s, convb_p, convb_s,
            k_p.reshape(1, B, T, n_heads, head_dim), k_s.reshape(1, nb, 1, n_heads, head_dim),
            v_p.reshape(1, B, T, n_heads, head_dim), v_s.reshape(1, nb, 1, n_heads, head_dim),
            lf_p.reshape(1, B, T, n_heads), lf_s.reshape(1, nb, 1, n_heads),
            cd_p[None, :, CONV_D_HIST_ROWS - nd:, :], jnp.swapaxes(convd_s_t, 0, 1)[None])
```

```python
import functools

import numpy as np
import jax
import jax.numpy as jnp
from jax import lax
from jax.experimental import pallas as pl
from jax.experimental.pallas import tpu as pltpu

F32 = jnp.float32
BF16 = jnp.bfloat16
I32 = jnp.int32

LN_EPS = 1e-5
POOL_WINDOWS = (2, 4, 8, 16)
TOP_K = 2
NEG = -1e30
LOG2E = 1.4426950408889634

V7X_VMEM_BYTES = 64 * 1024 * 1024
V7X_LANES = 128
V7X_SUBLANES = 8
BF16_ROWS_PER_TILE = 2 * V7X_SUBLANES

VMEM_LIMIT = V7X_VMEM_BYTES - 8 * 1024 * 1024

TOKEN_TILE = 512
EXPERT_TILE = 512
EXPERT_FF_CHUNK = 1792
FFN_CHUNK = 256
CONV_ROWS = 64
POOL_HIST_ROWS = 16
CONV_B_HIST_ROWS = 32
CONV_D_HIST_ROWS = 8
PAGES_PER_STEP = 16
SEG_PAD = BF16_ROWS_PER_TILE
SEG_BIG = 128
LOCAL_ROWS = TOP_K * TOKEN_TILE + 8 * SEG_PAD


def _cparams(sem):
    return pltpu.CompilerParams(dimension_semantics=sem, vmem_limit_bytes=VMEM_LIMIT)


def _dot(a, b):
    return jnp.dot(a, b, preferred_element_type=F32)


def _dot_nt(a, b):
    return lax.dot_general(a, b, (((1,), (1,)), ((), ())), preferred_element_type=F32)


def _dot_tn(a, b):
    return lax.dot_general(a, b, (((0,), (0,)), ((), ())), preferred_element_type=F32)


def _ln(z, g, b):
    mu = jnp.mean(z, axis=-1, keepdims=True)
    d = z - mu
    var = jnp.mean(d * d, axis=-1, keepdims=True)
    return d * lax.rsqrt(var + LN_EPS) * g + b


def _silu(x):
    return x * jax.nn.sigmoid(x)


def _log_sigmoid(z):
    return jnp.minimum(z, 0.0) - jnp.log1p(jnp.exp(-jnp.abs(z)))


def _split3(x):
    hi = x.astype(BF16)
    r = x - hi.astype(F32)
    mid = r.astype(BF16)
    lo = (r - mid.astype(F32)).astype(BF16)
    return hi, mid, lo


def _const_spec(shape):
    nd = len(shape)
    return pl.BlockSpec(shape, lambda *_: (0,) * nd)


def _resident_spec(shape):
    nd = len(shape)
    return pl.BlockSpec(shape, lambda *_: (0,) * nd, pipeline_mode=pl.Buffered(1))


def _pool_groups(a, hist_fn, pos, pw_ref):
    outs = []
    gw = a.shape[1] // len(POOL_WINDOWS)
    for g, w in enumerate(POOL_WINDOWS):
        c0 = g * gw
        cur = a[:, c0:c0 + gw]
        win = cur
        for k in range(1, w):
            win = win + hist_fn(k, c0, gw)
        cnt = jnp.minimum(pos + 1, w).astype(F32)
        pooled = win / cnt - cur
        outs.append(_dot(pooled.astype(BF16), pw_ref[g]))
    return jnp.concatenate(outs, axis=-1)


def _even_prompt_kernel(alpha, n_side, x_ref, win_ref, pw_ref, ps_ref, cw_ref, cb_ref, cg_ref, cbeta_ref,
                        wout_ref, g_ref, b_ref, *refs):
    side_in = refs[:n_side]
    y_ref, ah_ref, uh_ref = refs[n_side:n_side + 3]
    side_out = refs[n_side + 3:2 * n_side + 3]
    aext, uext, ush, ybuf = refs[2 * n_side + 3:]
    for src, dst in zip(side_in, side_out):
        dst[...] = src[...].astype(BF16)

    t = pl.program_id(1)
    tT = x_ref.shape[0]
    aw = ps_ref.shape[-1]
    bw = cw_ref.shape[-1]
    taps = cw_ref.shape[0]
    AH, UH = POOL_HIST_ROWS, CONV_B_HIST_ROWS

    @pl.when(t == 0)
    def _():
        aext[0:AH, :] = jnp.zeros((AH, aw), F32)
        uext[0:UH, :] = jnp.zeros((UH, bw), F32)

    @pl.when(t > 0)
    def _():
        aext[0:AH, :] = aext[tT:tT + AH, :]
        uext[0:UH, :] = uext[tT:tT + UH, :]

    x = x_ref[...]
    proj = _dot(x.astype(BF16), win_ref[...])
    a = proj[:, :aw]
    u = proj[:, aw:aw + bw] * jax.nn.sigmoid(proj[:, aw + bw:])
    aext[AH:, :] = a
    uext[UH:, :] = u

    pos = t * tT + lax.broadcasted_iota(I32, (tT, 1), 0)
    ya = _pool_groups(a, lambda k, c0, gw: aext[AH - k:AH - k + tT, c0:c0 + gw], pos, pw_ref)
    ya = ya * ps_ref[...]

    base = UH - (taps - 1)
    ns = V7X_SUBLANES
    span = UH + tT - ns
    for s in range(1, ns):
        ush[s - 1, 0:span, :] = uext[s:s + span, :]

    def tap(k, r0):
        off = base + k
        a, s = off - off % ns + r0, off % ns
        rows = uext[a:a + CONV_ROWS, :] if s == 0 else ush[s - 1, a:a + CONV_ROWS, :]
        return rows * cw_ref[k:k + 1, :]

    for r0 in range(0, tT, CONV_ROWS):
        acc = tap(0, r0)
        for k in range(1, taps):
            acc = acc + tap(k, r0)
        yb = _silu(_ln(acc + cb_ref[...], cg_ref[...], cbeta_ref[...]))
        ybuf[r0:r0 + CONV_ROWS, :] = yb.astype(BF16)

    mix = _dot(ya.astype(BF16), wout_ref[0:aw, :]) + _dot(ybuf[...], wout_ref[aw:, :])
    y_ref[...] = _ln(alpha * x + mix, g_ref[...], b_ref[...])
    ah_ref[...] = aext[tT:tT + AH, :]
    uh_ref[...] = uext[tT:tT + UH, :]


def _row_block_spec(w, n_steps, step_of):
    rows = w.shape[0] // n_steps
    assert rows * n_steps == w.shape[0] and rows % BF16_ROWS_PER_TILE == 0
    return pl.BlockSpec((rows, w.shape[1]), lambda *idx: (step_of(*idx), 0))


def _even_prompt(alpha, x, win, pw, ps, cw, cb, cg, cbeta, wout, g, b, side=()):
    B, T, D = x.shape
    tT = min(TOKEN_TILE, T)
    n_t = T // tT
    aw, bw = ps.shape[-1], cw.shape[-1]
    AH, UH = POOL_HIST_ROWS, CONV_B_HIST_ROWS
    consts = (win, pw, ps, cw, cb, cg, cbeta, wout, g, b)
    side_specs = [_row_block_spec(w, B * n_t, lambda bi, ti: bi * n_t + ti) for w in side]
    return pl.pallas_call(
        functools.partial(_even_prompt_kernel, alpha, len(side)),
        grid=(B, n_t),
        in_specs=[pl.BlockSpec((None, tT, D), lambda bi, ti: (bi, ti, 0))]
        + [_const_spec(c.shape) for c in consts] + side_specs,
        out_specs=[pl.BlockSpec((None, tT, D), lambda bi, ti: (bi, ti, 0)),
                   pl.BlockSpec((None, AH, aw), lambda bi, ti: (bi, 0, 0)),
                   pl.BlockSpec((None, UH, bw), lambda bi, ti: (bi, 0, 0))] + side_specs,
        out_shape=[jax.ShapeDtypeStruct((B, T, D), F32),
                   jax.ShapeDtypeStruct((B, AH, aw), F32),
                   jax.ShapeDtypeStruct((B, UH, bw), F32)]
        + [jax.ShapeDtypeStruct(w.shape, BF16) for w in side],
        scratch_shapes=[pltpu.VMEM((AH + tT, aw), F32), pltpu.VMEM((UH + tT, bw), F32),
                        pltpu.VMEM((V7X_SUBLANES - 1, UH + tT, bw), F32), pltpu.VMEM((tT, bw), BF16)],
        compiler_params=_cparams(("arbitrary", "arbitrary")),
        name="even_mixer_prompt",
    )(x, *consts, *side)


def _even_sample_kernel(alpha, first_pos, x_ref, sp_ref, sc_ref, win_ref, pw_ref, ps_ref, cw_ref, cb_ref,
                        cg_ref, cbeta_ref, wout_ref, g_ref, b_ref, y_ref, spo_ref, sco_ref):
    aw = ps_ref.shape[-1]
    bw = cw_ref.shape[-1]
    taps = cw_ref.shape[0]
    nph = sp_ref.shape[0]
    nch = sc_ref.shape[0]
    x = x_ref[...]
    proj = _dot(x.astype(BF16), win_ref[...])
    a = proj[:, :aw]
    u = proj[:, aw:aw + bw] * jax.nn.sigmoid(proj[:, aw + bw:])

    pos = jnp.full((x.shape[0], 1), first_pos, I32)
    ya = _pool_groups(a, lambda k, c0, gw: sp_ref[nph - k, :, c0:c0 + gw], pos, pw_ref)
    ya = ya * ps_ref[...]

    acc = u * cw_ref[taps - 1:taps, :]
    for k in range(taps - 1):
        acc = acc + sc_ref[k + nch - (taps - 1)] * cw_ref[k:k + 1, :]
    yb = _silu(_ln(acc + cb_ref[...], cg_ref[...], cbeta_ref[...]))

    mix = _dot(ya.astype(BF16), wout_ref[0:aw, :]) + _dot(yb.astype(BF16), wout_ref[aw:, :])
    y_ref[...] = _ln(alpha * x + mix, g_ref[...], b_ref[...])
    for j in range(nph - 1):
        spo_ref[j] = sp_ref[j + 1]
    spo_ref[nph - 1] = a
    for j in range(nch - 1):
        sco_ref[j] = sc_ref[j + 1]
    sco_ref[nch - 1] = u


def _even_sample(alpha, first_pos, x, sp_t, sc_t, win, pw, ps, cw, cb, cg, cbeta, wout, g, b):
    ins = (x, sp_t, sc_t, win, pw, ps, cw, cb, cg, cbeta, wout, g, b)
    return pl.pallas_call(
        functools.partial(_even_sample_kernel, alpha, first_pos),
        grid=(1,),
        in_specs=[_const_spec(c.shape) for c in ins],
        out_specs=[_const_spec(x.shape), _const_spec(sp_t.shape), _const_spec(sc_t.shape)],
        out_shape=[jax.ShapeDtypeStruct(x.shape, F32), jax.ShapeDtypeStruct(sp_t.shape, F32),
                   jax.ShapeDtypeStruct(sc_t.shape, F32)],
        compiler_params=_cparams(("arbitrary",)),
        name="even_mixer_sample",
    )(*ins)


def _ffn_kernel(alpha, x_ref, w1_ref, w3_ref, w2_ref, g_ref, b_ref, y_ref, h_ref):
    x = x_ref[...]
    xb = x.astype(BF16)
    ff = w1_ref.shape[1]
    for c in range(0, ff, FFN_CHUNK):
        h1 = _dot(xb, w1_ref[:, c:c + FFN_CHUNK])
        h3 = _dot(xb, w3_ref[:, c:c + FFN_CHUNK])
        h_ref[:, c:c + FFN_CHUNK] = (_silu(h1) * h3).astype(BF16)
    y = _dot(h_ref[...], w2_ref[...])
    y_ref[...] = _ln(alpha * x + y, g_ref[...], b_ref[...])


def _ffn(alpha, x, w1, w3, w2, g, b):
    n, d = x.shape
    tm = min(TOKEN_TILE, n)
    ff = w1.shape[1]
    assert ff % FFN_CHUNK == 0 and n % tm == 0
    return pl.pallas_call(
        functools.partial(_ffn_kernel, alpha),
        grid=(n // tm,),
        in_specs=[pl.BlockSpec((tm, d), lambda i: (i, 0)),
                  _resident_spec(w1.shape), _resident_spec(w3.shape), _resident_spec(w2.shape),
                  _const_spec(g.shape), _const_spec(b.shape)],
        out_specs=pl.BlockSpec((tm, d), lambda i: (i, 0)),
        out_shape=jax.ShapeDtypeStruct((n, d), F32),
        scratch_shapes=[pltpu.VMEM((tm, ff), BF16)],
        compiler_params=_cparams(("arbitrary",)),
        name="ffn_swiglu",
    )(x, w1, w3, w2, g, b)


def _head_select_mats(n_heads, head_dim):
    hw = V7X_LANES
    one_col = 3 * n_heads
    sq = np.zeros((hw, n_heads * hw), np.float32)
    sk = np.zeros((hw, n_heads * hw), np.float32)
    for h in range(n_heads):
        o = h * hw + head_dim
        for part in range(3):
            sq[part * n_heads + h, o + part] = 1.0
            sq[one_col, o + 3 + part] = 1.0
            sk[one_col, o + part] = 1.0
            sk[part * n_heads + h, o + 3 + part] = -1.0
    return jnp.asarray(sq, BF16), jnp.asarray(sk, BF16)


def _odd_prompt_kernel(n_heads, head_dim, x_ref, wqkv_ref, wf_ref, whbc_ref, fb_ref, cw_ref, tri_ref,
                       sq_ref, sk_ref, k_ref, v_ref, lf_ref, qa_ref, ka_ref, va_ref, yd_ref, cd_ref,
                       gext, fcarry):
    t = pl.program_id(1)
    tT = x_ref.shape[0]
    cwid = n_heads * head_dim
    dwid = cw_ref.shape[-1]
    taps = cw_ref.shape[0]
    GH = CONV_D_HIST_ROWS
    hw = V7X_LANES

    @pl.when(t == 0)
    def _():
        gext[0:GH, :] = jnp.zeros((GH, dwid), F32)
        fcarry[...] = jnp.zeros(fcarry.shape, F32)

    @pl.when(t > 0)
    def _():
        gext[0:GH, :] = gext[tT:tT + GH, :]

    xb = x_ref[...].astype(BF16)
    qkv = _dot(xb, wqkv_ref[...])
    q = qkv[:, :cwid]
    k = qkv[:, cwid:2 * cwid]
    v = qkv[:, 2 * cwid:]
    k_ref[...] = k
    v_ref[...] = v

    lane = lax.broadcasted_iota(I32, (tT, hw), 1)
    fl = _dot(xb, wf_ref[...])
    logf = jnp.where(lane < n_heads, _log_sigmoid(fl + fb_ref[...]), 0.0)
    lf_ref[...] = logf[:, :n_heads]

    tri = tri_ref[...]
    l_hi, l_mid, l_lo = _split3(logf)
    F = _dot(tri, l_hi) + _dot(tri, l_mid) + _dot(tri, l_lo) + fcarry[...]
    fcarry[...] = F[tT - 1:tT, :]

    f_hi, f_mid, f_lo = _split3(F * LOG2E)
    fparts = (f_hi.astype(F32) + pltpu.roll(f_mid.astype(F32), n_heads, axis=1)
              + pltpu.roll(f_lo.astype(F32), 2 * n_heads, axis=1)
              + jnp.where(lane == 3 * n_heads, 1.0, 0.0)).astype(BF16)
    xq = _dot(fparts, sq_ref[...])
    xk = _dot(fparts, sk_ref[...])
    vone = jnp.where(lane == head_dim, 1.0, 0.0)
    scale = head_dim ** -0.5 * LOG2E
    for h in range(n_heads):
        p = (h * head_dim) // hw
        qp = q[:, p * hw:(p + 1) * hw] * scale
        kp = k[:, p * hw:(p + 1) * hw]
        vp = v[:, p * hw:(p + 1) * hw]
        if (h * head_dim) % hw:
            sh = hw - (h * head_dim) % hw
            qp = pltpu.roll(qp, sh, axis=1)
            kp = pltpu.roll(kp, sh, axis=1)
            vp = pltpu.roll(vp, sh, axis=1)
        qa_ref[h] = jnp.where(lane < head_dim, qp, xq[:, h * hw:(h + 1) * hw]).astype(BF16)
        ka_ref[h] = jnp.where(lane < head_dim, kp, xk[:, h * hw:(h + 1) * hw]).astype(BF16)
        va_ref[h] = jnp.where(lane < head_dim, vp, vone).astype(BF16)

    hbc = _dot(xb, whbc_ref[...])
    hh = hbc[:, :dwid]
    bg = hbc[:, dwid:2 * dwid]
    cg = hbc[:, 2 * dwid:]
    g = cg * hh
    gext[GH:, :] = g
    conv = g * cw_ref[taps - 1:taps, :]
    for kk in range(taps - 1):
        back = taps - 1 - kk
        conv = conv + gext[GH - back:GH - back + tT, :] * cw_ref[kk:kk + 1, :]
    yd_ref[...] = (bg * conv).astype(BF16)
    cd_ref[...] = gext[tT:tT + GH, :]


def _odd_prompt(n_heads, head_dim, x, wqkv, wf, whbc, fb, cw):
    B, T, D = x.shape
    tT = min(TOKEN_TILE, T)
    cwid = n_heads * head_dim
    dwid = cw.shape[-1]
    hw = V7X_LANES
    GH = CONV_D_HIST_ROWS
    tri = jnp.tri(tT, dtype=BF16)
    sq, sk = _head_select_mats(n_heads, head_dim)
    consts = (wqkv, wf, whbc, fb, cw, tri, sq, sk)
    tok = lambda w: pl.BlockSpec((None, tT, w), lambda bi, ti: (bi, ti, 0))
    head = pl.BlockSpec((None, n_heads, tT, hw), lambda bi, ti: (bi, 0, ti, 0))
    return pl.pallas_call(
        functools.partial(_odd_prompt_kernel, n_heads, head_dim),
        grid=(B, T // tT),
        in_specs=[tok(D)] + [_const_spec(c.shape) for c in consts],
        out_specs=[tok(cwid), tok(cwid), tok(n_heads), head, head, head, tok(dwid),
                   pl.BlockSpec((None, GH, dwid), lambda bi, ti: (bi, 0, 0))],
        out_shape=[jax.ShapeDtypeStruct((B, T, cwid), F32), jax.ShapeDtypeStruct((B, T, cwid), F32),
                   jax.ShapeDtypeStruct((B, T, n_heads), F32),
                   jax.ShapeDtypeStruct((B, n_heads, T, hw), BF16),
                   jax.ShapeDtypeStruct((B, n_heads, T, hw), BF16),
                   jax.ShapeDtypeStruct((B, n_heads, T, hw), BF16),
                   jax.ShapeDtypeStruct((B, T, dwid), BF16),
                   jax.ShapeDtypeStruct((B, GH, dwid), F32)],
        scratch_shapes=[pltpu.VMEM((GH + tT, dwid), F32), pltpu.VMEM((1, hw), F32)],
        compiler_params=_cparams(("arbitrary", "arbitrary")),
        name="odd_proj_prompt",
    )(x, *consts)


def _odd_sample_kernel(n_heads, head_dim, x_ref, sd_ref, wqkv_ref, wf_ref, whbc_ref, fb_ref, cw_ref,
                       q_ref, k_ref, v_ref, lf_ref, yd_ref, sdo_ref):
    cwid = n_heads * head_dim
    dwid = cw_ref.shape[-1]
    taps = cw_ref.shape[0]
    nh = sd_ref.shape[0]
    xb = x_ref[...].astype(BF16)
    qkv = _dot(xb, wqkv_ref[...])
    q_ref[...] = qkv[:, :cwid]
    k_ref[...] = qkv[:, cwid:2 * cwid]
    v_ref[...] = qkv[:, 2 * cwid:]
    fl = _dot(xb, wf_ref[...])
    lf_ref[...] = _log_sigmoid(fl + fb_ref[...])
    hbc = _dot(xb, whbc_ref[...])
    g = hbc[:, 2 * dwid:] * hbc[:, :dwid]
    conv = g * cw_ref[taps - 1:taps, :]
    for kk in range(taps - 1):
        conv = conv + sd_ref[kk + nh - (taps - 1)] * cw_ref[kk:kk + 1, :]
    yd_ref[...] = (hbc[:, dwid:2 * dwid] * conv).astype(BF16)
    for j in range(nh - 1):
        sdo_ref[j] = sd_ref[j + 1]
    sdo_ref[nh - 1] = g


def _odd_sample(n_heads, head_dim, x, sd_t, wqkv, wf, whbc, fb, cw):
    n = x.shape[0]
    cwid = n_heads * head_dim
    dwid = cw.shape[-1]
    ins = (x, sd_t, wqkv, wf, whbc, fb, cw)
    shapes = [((n, cwid), F32), ((n, cwid), F32), ((n, cwid), F32), ((n, V7X_LANES), F32),
              ((n, dwid), BF16), (sd_t.shape, F32)]
    return pl.pallas_call(
        functools.partial(_odd_sample_kernel, n_heads, head_dim),
        grid=(1,),
        in_specs=[_const_spec(c.shape) for c in ins],
        out_specs=[_const_spec(s) for s, _ in shapes],
        out_shape=[jax.ShapeDtypeStruct(s, d) for s, d in shapes],
        compiler_params=_cparams(("arbitrary",)),
        name="odd_proj_sample",
    )(*ins)


def _fox_prompt_kernel(head_dim, n_side, qa_ref, ka_ref, va_ref, *refs):
    side_in = refs[:n_side]
    o_ref = refs[n_side]
    side_out = refs[n_side + 1:2 * n_side + 1]
    sa, sb = refs[2 * n_side + 1:]
    for src, dst in zip(side_in, side_out):
        dst[...] = src[...].astype(BF16)

    i = pl.program_id(2)
    tq = qa_ref.shape[1]
    hw = V7X_LANES
    lane = lax.broadcasted_iota(I32, (tq, hw), 1)
    row = lax.broadcasted_iota(I32, (tq, tq), 0)
    col = lax.broadcasted_iota(I32, (tq, tq), 1)
    nh = qa_ref.shape[0]
    qs = [qa_ref[hh] for hh in range(nh)]

    def scores(j, hh):
        return _dot_nt(qs[hh], ka_ref[hh, pl.ds(pl.multiple_of(j * tq, tq), tq), :])

    def update(j, hh, m, acc, s):
        m_new = jnp.maximum(m, jnp.max(s, axis=-1, keepdims=True))
        p = jnp.exp2(s - m_new)
        vt = va_ref[hh, pl.ds(pl.multiple_of(j * tq, tq), tq), :]
        return m_new, acc * jnp.exp2(m - m_new) + _dot(p.astype(BF16), vt)

    def step(j, carry, src, dst):
        new = []
        for hh in range(nh):
            dst[hh] = scores(j + 1, hh)
            new.append(update(j, hh, *carry[hh], src[hh]))
        return tuple(new)

    for hh in range(nh):
        sa[hh] = scores(0, hh)

    def pair(u, carry):
        return step(2 * u + 1, step(2 * u, carry, sa, sb), sb, sa)

    odd = i % 2
    carry = tuple((jnp.full((tq, 1), NEG, F32), jnp.zeros((tq, hw), F32)) for _ in range(nh))
    carry = lax.fori_loop(0, i // 2, pair, carry)
    carry = lax.fori_loop(0, odd, lambda _, c: step(i - 1, c, sa, sb), carry)
    def finish(src):
        o = None
        for hh in range(nh):
            _, acc = update(i, hh, *carry[hh], jnp.where(col <= row, src[hh], NEG))
            denom = jnp.sum(jnp.where(lane == head_dim, acc, 0.0), axis=-1, keepdims=True)
            out = acc / denom
            o = out if hh == 0 else jnp.where(lane < hh * head_dim, o, pltpu.roll(out, hh * head_dim, axis=1))
        o_ref[...] = o.astype(o_ref.dtype)

    @pl.when(odd == 0)
    def _():
        finish(sa)

    @pl.when(odd == 1)
    def _():
        finish(sb)


def _fox_prompt(head_dim, qa, ka, va, side=()):
    B, H, T, hw = qa.shape
    tq = min(TOKEN_TILE, T)
    hp = hw // head_dim
    grid = (B, H // hp, T // tq)
    n_steps = grid[0] * grid[1] * grid[2]
    side_specs = [_row_block_spec(w, n_steps, lambda b, p, i: (b * grid[1] + p) * grid[2] + i) for w in side]
    return pl.pallas_call(
        functools.partial(_fox_prompt_kernel, head_dim, len(side)),
        grid=grid,
        in_specs=[pl.BlockSpec((None, hp, tq, hw), lambda b, p, i: (b, p, i, 0)),
                  pl.BlockSpec((None, hp, T, hw), lambda b, p, i: (b, p, 0, 0)),
                  pl.BlockSpec((None, hp, T, hw), lambda b, p, i: (b, p, 0, 0))] + side_specs,
        out_specs=[pl.BlockSpec((None, tq, hw), lambda b, p, i: (b, i, p))] + side_specs,
        out_shape=[jax.ShapeDtypeStruct((B, T, H * head_dim), BF16)]
        + [jax.ShapeDtypeStruct(w.shape, BF16) for w in side],
        scratch_shapes=[pltpu.VMEM((hp, tq, tq), F32), pltpu.VMEM((hp, tq, tq), F32)],
        compiler_params=_cparams(("arbitrary", "arbitrary", "arbitrary")),
        name="fox_attention_prompt",
    )(qa, ka, va, *side)


def _fox_sample_kernel(npp, pt_ref, *refs):
    k_refs = refs[0:npp]
    v_refs = refs[npp:2 * npp]
    lf_refs = refs[2 * npp:3 * npp]
    (qb_ref, kn_ref, vn_ref, lfn_ref, u_ref, o_ref, m_s, l_s, acc_s, carry_s) = refs[3 * npp:]
    i = pl.program_id(1)
    n_steps = pl.num_programs(1)
    H, dh, page = k_refs[0].shape
    cw = H * dh

    @pl.when(i == 0)
    def _():
        m_s[...] = jnp.full(m_s.shape, NEG, F32)
        l_s[...] = jnp.zeros(l_s.shape, F32)
        acc_s[...] = jnp.zeros(acc_s.shape, F32)
        carry_s[...] = lfn_ref[...]

    lf_all = jnp.concatenate([r[...] for r in lf_refs], axis=0)
    l_hi, l_mid, l_lo = _split3(lf_all)
    u = u_ref[...]
    g_all = _dot(l_hi, u) + _dot(l_mid, u) + _dot(l_lo, u)
    tot = jnp.sum(lf_all, axis=1, keepdims=True)
    later = carry_s[...]
    decay = [None] * npp
    for j in reversed(range(npp)):
        decay[j] = later
        later = later + tot[j * H:(j + 1) * H, :]
    carry_s[...] = later

    qb = qb_ref[...]
    qbb = qb.astype(BF16)
    s_pages = []
    for j in range(npp):
        k2 = k_refs[j][...].reshape(cw, page).astype(BF16)
        s_pages.append(_dot(qbb, k2) + g_all[j * H:(j + 1) * H, :] + decay[j])

    mx = s_pages[0]
    for j in range(1, npp):
        mx = jnp.maximum(mx, s_pages[j])
    m = m_s[...]
    m_new = jnp.maximum(m, jnp.max(mx, axis=1, keepdims=True))
    c = jnp.exp(m - m_new)
    p_pages = [jnp.exp(s - m_new) for s in s_pages]
    psum = p_pages[0]
    for j in range(1, npp):
        psum = psum + p_pages[j]
    l_s[...] = l_s[...] * c + psum
    m_s[...] = m_new
    acc = acc_s[...] * c
    for j in range(npp):
        v2 = v_refs[j][...].reshape(cw, page).astype(BF16)
        acc = acc + _dot_nt(p_pages[j].astype(BF16), v2)
    acc_s[...] = acc

    @pl.when(i == n_steps - 1)
    def _():
        s_new = jnp.sum(qb * kn_ref[...], axis=1, keepdims=True)
        m_fin = jnp.maximum(m_new, s_new)
        cf = jnp.exp(m_new - m_fin)
        p_new = jnp.exp(s_new - m_fin)
        denom = jnp.sum(l_s[...], axis=1, keepdims=True) * cf + p_new
        o_ref[...] = (acc * cf + p_new * vn_ref[...]) / denom


def _fox_sample(page_table, kview, vview, lfview, qb, kn, vn, lfn):
    nb, n_pages = page_table.shape
    _, H, dh, page = kview.shape
    npp = min(PAGES_PER_STEP, n_pages)
    assert n_pages % npp == 0
    n_steps = n_pages // npp
    u = jnp.asarray(np.tril(np.ones((page, page), np.float32), -1), BF16)

    def page_spec(shape, j):
        nd = len(shape)
        return pl.BlockSpec((None,) + tuple(shape),
                            lambda b, i, pt, j=j: (pt[b, n_pages - npp * (i + 1) + j],) + (0,) * nd)

    seq = lambda shape: pl.BlockSpec((None,) + tuple(shape), lambda b, i, pt: (b,) + (0,) * len(shape))
    in_specs = ([page_spec((H, dh, page), j) for j in range(npp)]
                + [page_spec((H, dh, page), j) for j in range(npp)]
                + [page_spec((H, page), j) for j in range(npp)]
                + [seq((H, H * dh)), seq((1, H * dh)), seq((1, H * dh)), seq((H, 1)),
                   pl.BlockSpec(u.shape, lambda b, i, pt: (0, 0))])
    grid_spec = pltpu.PrefetchScalarGridSpec(
        num_scalar_prefetch=1, grid=(nb, n_steps), in_specs=in_specs,
        out_specs=seq((H, H * dh)),
        scratch_shapes=[pltpu.VMEM((H, 1), F32), pltpu.VMEM((H, page), F32), pltpu.VMEM((H, H * dh), F32),
                        pltpu.VMEM((H, 1), F32)])
    return pl.pallas_call(
        functools.partial(_fox_sample_kernel, npp),
        grid_spec=grid_spec,
        out_shape=jax.ShapeDtypeStruct((nb, H, H * dh), F32),
        compiler_params=_cparams(("arbitrary", "arbitrary")),
        name="fox_attention_sample",
    )(page_table, *([kview] * npp), *([vview] * npp), *([lfview] * npp), qb, kn, vn, lfn, u)


def _top2(logits, axis, n_experts):
    idx = lax.broadcasted_iota(I32, logits.shape, axis)
    big = logits.shape[axis]
    lg = jnp.where(idx < n_experts, logits, -jnp.inf)
    v1 = jnp.max(lg, axis=axis, keepdims=True)
    i1 = jnp.min(jnp.where(lg == v1, idx, big), axis=axis, keepdims=True)
    lg2 = jnp.where(idx == i1, -jnp.inf, lg)
    v2 = jnp.max(lg2, axis=axis, keepdims=True)
    i2 = jnp.min(jnp.where(lg2 == v2, idx, big), axis=axis, keepdims=True)
    return idx, i1, i2, v1, v2


def _odd_out_kernel(alpha, n_experts, n_prompt_tiles, op_ref, ydp_ref, xp_ref, os_ref, yds_ref, xs_ref,
                    wout_ref, g_ref, b_ref, rw_ref, x3p_ref, x3s_ref, x3b_ref, lg_ref, cnt_ref):
    i = pl.program_id(0)
    cwid = op_ref.shape[1]

    def rows(o_ref, yd_ref, x_ref, x3_ref):
        n = o_ref.shape[0]
        mix = _dot(o_ref[...].astype(BF16), wout_ref[0:cwid, :]) + _dot(yd_ref[...], wout_ref[cwid:, :])
        x3 = _ln(alpha * x_ref[...] + mix, g_ref[...], b_ref[...])
        x3b = x3.astype(BF16)
        logits = _dot(x3b, rw_ref[...])
        x3_ref[...] = x3
        if n < x3b_ref.shape[0]:
            x3b_ref[...] = jnp.zeros(x3b_ref.shape, BF16)
            lg_ref[...] = jnp.zeros(lg_ref.shape, F32)
        x3b_ref[0:n, :] = x3b
        lg_ref[0:n, :] = logits
        idx, i1, i2, _, _ = _top2(logits, 1, n_experts)
        mask = jnp.where((idx == i1) | (idx == i2), 1.0, 0.0)
        cnt_ref[...] = jnp.sum(mask, axis=0, keepdims=True)

    @pl.when(i < n_prompt_tiles)
    def _():
        rows(op_ref, ydp_ref, xp_ref, x3p_ref)

    @pl.when(i >= n_prompt_tiles)
    def _():
        rows(os_ref, yds_ref, xs_ref, x3s_ref)


def _odd_out(alpha, n_experts, o_p, yd_p, x_p, o_s, yd_s, x_s, wout, g, b, rw):
    n, d = x_p.shape
    ns = x_s.shape[0]
    tm = TOKEN_TILE
    npt = n // tm
    ntiles = npt + 1
    hw = V7X_LANES
    prow = lambda w: pl.BlockSpec((tm, w), lambda i: (jnp.minimum(i, npt - 1), 0))
    return pl.pallas_call(
        functools.partial(_odd_out_kernel, alpha, n_experts, npt),
        grid=(ntiles,),
        in_specs=[prow(o_p.shape[1]), prow(yd_p.shape[1]), prow(d),
                  _const_spec(o_s.shape), _const_spec(yd_s.shape), _const_spec(x_s.shape)]
        + [_const_spec(c.shape) for c in (wout, g, b, rw)],
        out_specs=[prow(d), _const_spec((ns, d)),
                   pl.BlockSpec((tm, d), lambda i: (i, 0)),
                   pl.BlockSpec((tm, hw), lambda i: (i, 0)),
                   pl.BlockSpec((None, 1, hw), lambda i: (i, 0, 0))],
        out_shape=[jax.ShapeDtypeStruct((n, d), F32), jax.ShapeDtypeStruct((ns, d), F32),
                   jax.ShapeDtypeStruct((ntiles * tm, d), BF16),
                   jax.ShapeDtypeStruct((ntiles * tm, hw), F32),
                   jax.ShapeDtypeStruct((ntiles, 1, hw), F32)],
        compiler_params=_cparams(("arbitrary",)),
        name="odd_out_router",
    )(o_p, yd_p, x_p, o_s, yd_s, x_s, wout, g, b, rw)


def _pieces(n, make_copy):
    ratio = SEG_BIG // SEG_PAD
    nbig = n // ratio

    def big(j, _):
        make_copy(j * SEG_BIG, SEG_BIG)
        return 0

    def small(j, _):
        make_copy(nbig * SEG_BIG + j * SEG_PAD, SEG_PAD)
        return 0

    lax.fori_loop(0, nbig, big, 0)
    lax.fori_loop(0, n - nbig * ratio, small, 0)


def _segment_copies(tile, n_experts, npc_ref, loff_ref, goff_ref, make_copy):
    for e in range(n_experts):
        k = tile * n_experts + e
        lo, go = loff_ref[k], goff_ref[k]
        _pieces(npc_ref[k], lambda off, n, lo=lo, go=go: make_copy(
            pl.multiple_of(lo + off, SEG_PAD), pl.multiple_of(go + off, SEG_PAD), n))


def _dispatch_kernel(n_experts, n_valid, npc_ref, loff_ref, goff_ref, zoff_ref, zcnt_ref, x_ref, lt_ref, lofft_ref,
                     triu_ref, xs_ref, gs_ref, stage, gstage, zx, zg, sem):
    tile = pl.program_id(0)
    tt = x_ref.shape[0]
    rl = stage.shape[0]
    sub, i1, i2, v1, v2 = _top2(lt_ref[...], 0, n_experts)
    valid = (tile * tt + lax.broadcasted_iota(I32, (1, tt), 1)) < n_valid
    sel1 = (sub == i1) & valid
    sel2 = (sub == i2) & valid
    mask = jnp.where(sel1 | sel2, 1.0, 0.0)
    rank = _dot(mask.astype(BF16), triu_ref[...])
    loc = rank + jnp.concatenate([lofft_ref[...]] * (tt // V7X_LANES), axis=1)
    lr1 = jnp.sum(jnp.where(sel1, loc, 0.0), axis=0, keepdims=True).astype(I32)
    lr2 = jnp.sum(jnp.where(sel2, loc, 0.0), axis=0, keepdims=True).astype(I32)
    lr1 = jnp.where(valid, lr1, -1)
    lr2 = jnp.where(valid, lr2, -1)
    r = lax.broadcasted_iota(I32, (rl, tt), 0)
    hit1 = r == lr1
    hit2 = r == lr2
    onehot = jnp.where(hit1 | hit2, 1.0, 0.0).astype(BF16)
    stage[...] = _dot(onehot, x_ref[...]).astype(BF16)
    e21 = jnp.exp(v2 - v1)
    g1 = 1.0 / (1.0 + e21)
    g2 = e21 / (1.0 + e21)
    gs = jnp.sum(jnp.where(hit1, g1, 0.0) + jnp.where(hit2, g2, 0.0), axis=1, keepdims=True)
    gstage[...] = jnp.broadcast_to(gs, gstage.shape)

    def copies(fn):
        _segment_copies(tile, n_experts, npc_ref, loff_ref, goff_ref,
                        lambda lo, go, n: fn(pltpu.make_async_copy(stage.at[pl.ds(lo, n), :],
                                                                  xs_ref.at[pl.ds(go, n), :], sem.at[0])))
        _segment_copies(tile, n_experts, npc_ref, loff_ref, goff_ref,
                        lambda lo, go, n: fn(pltpu.make_async_copy(gstage.at[pl.ds(lo, n), :],
                                                                  gs_ref.at[pl.ds(go, n), :], sem.at[1])))

    copies(lambda cp: cp.start())
    copies(lambda cp: cp.wait())

    @pl.when(tile == pl.num_programs(0) - 1)
    def _():
        zx[...] = jnp.zeros(zx.shape, BF16)
        zg[...] = jnp.zeros(zg.shape, F32)

        def fills(fn):
            for k in range(n_experts + 1):
                start = zoff_ref[k]

                def one(off, n, start=start):
                    dst = pl.ds(pl.multiple_of(start + off, SEG_PAD), n)
                    fn(pltpu.make_async_copy(zx.at[pl.ds(0, n), :], xs_ref.at[dst, :], sem.at[0]))
                    fn(pltpu.make_async_copy(zg.at[pl.ds(0, n), :], gs_ref.at[dst, :], sem.at[1]))

                _pieces(zcnt_ref[k], one)

        fills(lambda cp: cp.start())
        fills(lambda cp: cp.wait())


def _dispatch(n_experts, n_valid, rmax, npc, loff, goff, zoff, zcnt, x3b, lt, lofft):
    ntot, d = x3b.shape
    tt = TOKEN_TILE
    triu = jnp.asarray(np.triu(np.ones((tt, tt), np.float32), 1), BF16)
    grid_spec = pltpu.PrefetchScalarGridSpec(
        num_scalar_prefetch=5, grid=(ntot // tt,),
        in_specs=[pl.BlockSpec((tt, d), lambda i, *_: (i, 0)),
                  pl.BlockSpec((n_experts, tt), lambda i, *_: (0, i)),
                  pl.BlockSpec((None, n_experts, V7X_LANES), lambda i, *_: (i, 0, 0)),
                  pl.BlockSpec(triu.shape, lambda i, *_: (0, 0))],
        out_specs=[pl.BlockSpec(memory_space=pl.ANY), pl.BlockSpec(memory_space=pl.ANY)],
        scratch_shapes=[pltpu.VMEM((LOCAL_ROWS, d), BF16), pltpu.VMEM((LOCAL_ROWS, V7X_LANES), F32),
                        pltpu.VMEM((SEG_BIG, d), BF16), pltpu.VMEM((SEG_BIG, V7X_LANES), F32),
                        pltpu.SemaphoreType.DMA((2,))])
    return pl.pallas_call(
        functools.partial(_dispatch_kernel, n_experts, n_valid),
        grid_spec=grid_spec,
        out_shape=[jax.ShapeDtypeStruct((rmax, d), BF16), jax.ShapeDtypeStruct((rmax, V7X_LANES), F32)],
        compiler_params=_cparams(("arbitrary",)),
        name="moe_dispatch",
    )(npc, loff, goff, zoff, zcnt, x3b, lt, lofft, triu)


def _experts_kernel(te_ref, tv_ref, xr_ref, x_ref, gate_ref, w1_ref, w3_ref, w2_ref, yh_ref, yl_ref, h_ref):
    del te_ref, xr_ref
    i = pl.program_id(0)
    c = pl.program_id(1)
    last = pl.num_programs(1) - 1
    fc = w1_ref.shape[1]
    valid = tv_ref[i] > 0

    @pl.when(valid)
    def _():
        x = x_ref[...]
        h = (_silu(_dot(x, w1_ref[...])) * _dot(x, w3_ref[...])).astype(BF16)
        h_ref[:, pl.ds(pl.multiple_of(c * fc, fc), fc)] = h

        @pl.when(c == last)
        def _():
            y = _dot(h_ref[...], w2_ref[...]) * gate_ref[:, 0:1]
            hi = y.astype(BF16)
            yh_ref[...] = hi
            yl_ref[...] = (y - hi.astype(F32)).astype(BF16)

    @pl.when(jnp.logical_not(valid) & (c == last))
    def _():
        yh_ref[...] = jnp.zeros(yh_ref.shape, BF16)
        yl_ref[...] = jnp.zeros(yl_ref.shape, BF16)


def _experts(te, tv, xr, xs, gates, w1, w3, w2):
    rmax, d = xs.shape
    tm = EXPERT_TILE
    ff = w1.shape[-1]
    fc = min(EXPERT_FF_CHUNK, ff)
    nch = ff // fc
    assert ff % fc == 0 and rmax % tm == 0

    def chunk(i, c, tv):
        return jnp.where(tv[i] > 0, c, nch - 1)

    grid_spec = pltpu.PrefetchScalarGridSpec(
        num_scalar_prefetch=3, grid=(rmax // tm, nch),
        in_specs=[pl.BlockSpec((tm, d), lambda i, c, te, tv, xr: (xr[i], 0)),
                  pl.BlockSpec((tm, V7X_LANES), lambda i, c, te, tv, xr: (xr[i], 0)),
                  pl.BlockSpec((None, d, fc), lambda i, c, te, tv, xr: (te[i], 0, chunk(i, c, tv))),
                  pl.BlockSpec((None, d, fc), lambda i, c, te, tv, xr: (te[i], 0, chunk(i, c, tv))),
                  pl.BlockSpec((None, ff, d), lambda i, c, te, tv, xr: (te[i], 0, 0))],
        out_specs=[pl.BlockSpec((tm, d), lambda i, c, te, tv, xr: (i, 0)),
                   pl.BlockSpec((tm, d), lambda i, c, te, tv, xr: (i, 0))],
        scratch_shapes=[pltpu.VMEM((tm, ff), BF16)])
    return pl.pallas_call(
        _experts_kernel,
        grid_spec=grid_spec,
        out_shape=[jax.ShapeDtypeStruct((rmax, d), BF16), jax.ShapeDtypeStruct((rmax, d), BF16)],
        compiler_params=_cparams(("arbitrary", "arbitrary")),
        name="moe_experts",
    )(te, tv, xr, xs, gates, w1, w3, w2)


def _combine_kernel(alpha, n_experts, n_valid, n_prompt_tiles, npc_ref, loff_ref, goff_ref, lg_ref, xp_ref,
                    xs_ref, loffr_ref, tril_ref, g_ref, b_ref, yh_hbm, yl_hbm, yp_ref, ysm_ref,
                    sth, stl, sem):
    tile = pl.program_id(0)
    tt = lg_ref.shape[0]
    rl = sth.shape[1]
    d = sth.shape[2]
    slot = tile % 2

    def copies(t, s, fn):
        _segment_copies(t, n_experts, npc_ref, loff_ref, goff_ref,
                        lambda lo, go, n: fn(pltpu.make_async_copy(yh_hbm.at[pl.ds(go, n), :],
                                                                  sth.at[s, pl.ds(lo, n), :], sem.at[s, 0])))
        _segment_copies(t, n_experts, npc_ref, loff_ref, goff_ref,
                        lambda lo, go, n: fn(pltpu.make_async_copy(yl_hbm.at[pl.ds(go, n), :],
                                                                  stl.at[s, pl.ds(lo, n), :], sem.at[s, 1])))

    def fetch(t, s):
        copies(t, s, lambda cp: cp.start())
        last = t * n_experts + n_experts - 1
        used = loff_ref[last] // SEG_PAD + npc_ref[last]

        def clear(j, _):
            o = pl.multiple_of(j * SEG_PAD, SEG_PAD)
            sth[s, pl.ds(o, SEG_PAD), :] = jnp.zeros((SEG_PAD, d), BF16)
            stl[s, pl.ds(o, SEG_PAD), :] = jnp.zeros((SEG_PAD, d), BF16)
            return 0

        lax.fori_loop(used, rl // SEG_PAD, clear, 0)

    @pl.when(tile == 0)
    def _():
        fetch(tile, slot)

    @pl.when(tile + 1 < pl.num_programs(0))
    def _():
        fetch(tile + 1, 1 - slot)

    lane, i1, i2, _, _ = _top2(lg_ref[...], 1, n_experts)
    valid = (tile * tt + lax.broadcasted_iota(I32, (tt, 1), 0)) < n_valid
    sel1 = (lane == i1) & valid
    sel2 = (lane == i2) & valid
    mask = jnp.where(sel1 | sel2, 1.0, 0.0)
    rank = _dot(tril_ref[...], mask.astype(BF16))
    loc = rank + jnp.concatenate([loffr_ref[...]] * (tt // V7X_SUBLANES), axis=0)
    lr1 = jnp.sum(jnp.where(sel1, loc, 0.0), axis=1, keepdims=True).astype(I32)
    lr2 = jnp.sum(jnp.where(sel2, loc, 0.0), axis=1, keepdims=True).astype(I32)
    lr1 = jnp.where(valid, lr1, -1)
    lr2 = jnp.where(valid, lr2, -1)
    r = lax.broadcasted_iota(I32, (tt, rl), 1)
    pick = jnp.where((r == lr1) | (r == lr2), 1.0, 0.0).astype(BF16)

    copies(tile, slot, lambda cp: cp.wait())
    moe = _dot(pick, sth[slot]) + _dot(pick, stl[slot])

    @pl.when(tile < n_prompt_tiles)
    def _():
        yp_ref[...] = _ln(alpha * xp_ref[...] + moe, g_ref[...], b_ref[...])

    @pl.when(tile >= n_prompt_tiles)
    def _():
        ns = ysm_ref.shape[0]
        ysm_ref[...] = _ln(alpha * xs_ref[...] + moe[0:ns, :], g_ref[...], b_ref[...])


def _combine(alpha, n_experts, n_valid, npc, loff, goff, logits, x3p, x3s, loffr, g, b, yh, yl):
    npr, d = x3p.shape
    ns = x3s.shape[0]
    tt = TOKEN_TILE
    npt = npr // tt
    ntiles = logits.shape[0] // tt
    tril = jnp.asarray(np.tril(np.ones((tt, tt), np.float32), -1), BF16)
    grid_spec = pltpu.PrefetchScalarGridSpec(
        num_scalar_prefetch=3, grid=(ntiles,),
        in_specs=[pl.BlockSpec((tt, V7X_LANES), lambda i, *_: (i, 0)),
                  pl.BlockSpec((tt, d), lambda i, *_: (jnp.minimum(i, npt - 1), 0)),
                  pl.BlockSpec((ns, d), lambda i, *_: (0, 0)),
                  pl.BlockSpec((None, V7X_SUBLANES, V7X_LANES), lambda i, *_: (i, 0, 0)),
                  pl.BlockSpec(tril.shape, lambda i, *_: (0, 0)),
                  pl.BlockSpec(g.shape, lambda i, *_: (0, 0)),
                  pl.BlockSpec(b.shape, lambda i, *_: (0, 0)),
                  pl.BlockSpec(memory_space=pl.ANY),
                  pl.BlockSpec(memory_space=pl.ANY)],
        out_specs=[pl.BlockSpec((tt, d), lambda i, *_: (jnp.minimum(i, npt - 1), 0)),
                   pl.BlockSpec((ns, d), lambda i, *_: (0, 0))],
        scratch_shapes=[pltpu.VMEM((2, LOCAL_ROWS, d), BF16), pltpu.VMEM((2, LOCAL_ROWS, d), BF16),
                        pltpu.SemaphoreType.DMA((2, 2))])
    return pl.pallas_call(
        functools.partial(_combine_kernel, alpha, n_experts, n_valid, npt),
        grid_spec=grid_spec,
        out_shape=[jax.ShapeDtypeStruct((npr, d), F32), jax.ShapeDtypeStruct((ns, d), F32)],
        compiler_params=_cparams(("arbitrary",)),
        name="moe_combine",
    )(npc, loff, goff, logits, x3p, x3s, loffr, tril, g, b, yh, yl)


def _routing_plan(cnt, n_experts):
    ntiles = cnt.shape[0]
    pc = (cnt + SEG_PAD - 1) // SEG_PAD * SEG_PAD
    loff = jnp.cumsum(pc, axis=1) - pc
    per_expert = jnp.sum(pc, axis=0)
    gp = (per_expert + EXPERT_TILE - 1) // EXPERT_TILE * EXPERT_TILE
    gend = jnp.cumsum(gp)
    goff = (gend - gp)[None, :] + jnp.cumsum(pc, axis=0) - pc
    rmax = TOP_K * ntiles * TOKEN_TILE + ntiles * n_experts * (SEG_PAD - 1) + n_experts * (EXPERT_TILE - 1)
    nt_max = -(-rmax // EXPERT_TILE)
    tiles_used = gend[-1] // EXPERT_TILE
    ti = jnp.arange(nt_max, dtype=I32)
    tv = (ti < tiles_used).astype(I32)
    xr = jnp.maximum(jnp.minimum(ti, tiles_used - 1), 0)
    te = jnp.sum((xr[:, None] >= (gend // EXPERT_TILE)[None, :]).astype(I32), axis=1)
    te = jnp.minimum(te, n_experts - 1)
    flat = lambda a: a.reshape(-1).astype(I32)
    rows = nt_max * EXPERT_TILE
    zoff = jnp.concatenate([gend - gp + per_expert, gend[-1:]])
    zcnt = jnp.concatenate([gp - per_expert, rows - gend[-1:]]) // SEG_PAD
    return (flat(pc // SEG_PAD), flat(loff), flat(goff), flat(zoff), flat(zcnt), loff.astype(F32), te, tv,
            xr.astype(I32), rows)


def kernel(x_prompt, x_sample, state_pool, state_conv_b, cache_k, cache_v, cache_logf, state_conv_d, page_table,
           w_in_even, pool_w, pool_scale, conv_b_w, conv_b_bias, conv_ln_g, conv_ln_b, w_out_even, ln_mix_even_g,
           ln_mix_even_b, ffn_w1, ffn_w3, ffn_w2, ln_ffn_even_g, ln_ffn_even_b, w_in_odd, forget_bias, conv_d_w,
           w_out_odd, ln_mix_odd_g, ln_mix_odd_b, router_w, moe_w1, moe_w3, moe_w2, ln_ffn_odd_g, ln_ffn_odd_b):
    assert w_in_even.shape[0] == 1 and w_in_odd.shape[0] == 1, "one even and one odd layer are supported"
    depth = w_in_even.shape[0] + w_in_odd.shape[0]
    alpha = float((2 * depth) ** 0.25)
    B, T, D = x_prompt.shape
    nb = x_sample.shape[0]
    assert x_sample.shape[1] == 1 and T % TOKEN_TILE == 0 and nb <= TOKEN_TILE
    n_heads = forget_bias.shape[-1]
    cwid = cache_k.shape[-1] * cache_k.shape[-2]
    head_dim = cache_k.shape[-1]
    dwid = conv_d_w.shape[-1]
    n_experts = router_w.shape[-1]
    past_len = page_table.shape[1] * cache_k.shape[2]
    hw = V7X_LANES
    bf = lambda w: w.astype(BF16)
    rowv = lambda w: w.reshape(1, -1)

    ew = (bf(w_in_even[0]), bf(pool_w[0]), pool_scale, conv_b_w[0], conv_b_bias, conv_ln_g, conv_ln_b,
          bf(w_out_even[0]), ln_mix_even_g, ln_mix_even_b)
    ffd = ffn_w1.shape[-1]
    xp, a_hist, u_hist, fw1, fw3, fw2 = _even_prompt(alpha, x_prompt, *ew, side=(
        ffn_w1[0], ffn_w3[0], ffn_w2[0].reshape(2 * ffd, D // 2)))
    xs, pool_s_t, convb_s_t = _even_sample(
        alpha, past_len, x_sample.reshape(nb, D), jnp.swapaxes(state_pool[0], 0, 1),
        jnp.swapaxes(state_conv_b[0], 0, 1), *ew)
    nph = state_pool.shape[2]
    nch = state_conv_b.shape[2]
    pool_p = a_hist[None, :, POOL_HIST_ROWS - nph:, :]
    convb_p = u_hist[None, :, CONV_B_HIST_ROWS - nch:, :]
    pool_s = jnp.swapaxes(pool_s_t, 0, 1)[None]
    convb_s = jnp.swapaxes(convb_s_t, 0, 1)[None]

    fw = (fw1, fw3, fw2.reshape(ffd, D), ln_ffn_even_g, ln_ffn_even_b)
    xp = _ffn(alpha, xp.reshape(B * T, D), *fw)
    xs = _ffn(alpha, xs, *fw)

    w_in = w_in_odd[0]
    wqkv = bf(w_in[:, :3 * cwid])
    wf = bf(jnp.pad(w_in[:, 3 * cwid:3 * cwid + n_heads], ((0, 0), (0, hw - n_heads))))
    whbc = bf(w_in[:, 3 * cwid + n_heads:])
    fb = jnp.pad(forget_bias, ((0, 0), (0, hw - n_heads)))
    k_p, v_p, lf_p, qa, ka, va, yd_p, cd_p = _odd_prompt(
        n_heads, head_dim, xp.reshape(B, T, D), wqkv, wf, whbc, fb, conv_d_w[0])
    ne, _, ffe = moe_w1.shape[1:]
    o_p, mw1, mw3, mw2 = _fox_prompt(head_dim, qa, ka, va, side=(
        moe_w1[0].reshape(ne * D, ffe), moe_w3[0].reshape(ne * D, ffe), moe_w2[0].reshape(ne * ffe, D)))
    mw1, mw3, mw2 = mw1.reshape(ne, D, ffe), mw3.reshape(ne, D, ffe), mw2.reshape(ne, ffe, D)

    q_s, k_s, v_s, lf_s, yd_s, convd_s_t = _odd_sample(
        n_heads, head_dim, xs, jnp.swapaxes(state_conv_d[0], 0, 1), wqkv, wf, whbc, fb, conv_d_w[0])
    head_of_col = jnp.arange(cwid, dtype=I32) // head_dim
    blockdiag = (head_of_col[None, :] == jnp.arange(n_heads, dtype=I32)[:, None]).astype(F32)
    lf_s = lf_s[:, :n_heads]
    o_s_all = _fox_sample(page_table, jnp.transpose(cache_k[0], (0, 2, 3, 1)),
                          jnp.transpose(cache_v[0], (0, 2, 3, 1)), jnp.transpose(cache_logf[0], (0, 2, 1)),
                          (q_s * head_dim ** -0.5)[:, None, :] * blockdiag[None], k_s[:, None, :], v_s[:, None, :],
                          lf_s.reshape(nb, n_heads, 1))
    o_s = jnp.sum(o_s_all * blockdiag[None], axis=1)

    n_prompt = B * T
    npt = n_prompt // TOKEN_TILE
    ntiles = npt + 1
    n_valid = n_prompt + nb
    ow = (bf(w_out_odd[0]), ln_mix_odd_g, ln_mix_odd_b,
          bf(jnp.pad(router_w[0], ((0, 0), (0, hw - n_experts)))))
    x3p, x3s, x3b, logits, cnt = _odd_out(alpha, n_experts, o_p.reshape(n_prompt, cwid),
                                          yd_p.reshape(n_prompt, dwid), xp, o_s, yd_s, xs, *ow)

    cnt = cnt[:, 0, :n_experts].astype(I32)
    npc, loff, goff, zoff, zcnt, loff_f, te, tv, xr, rmax = _routing_plan(cnt, n_experts)
    lt = jnp.transpose(logits[:, :n_experts])
    lofft = jnp.broadcast_to(loff_f[:, :, None], (ntiles, n_experts, hw))
    loffr = jnp.broadcast_to(jnp.pad(loff_f, ((0, 0), (0, hw - n_experts)))[:, None, :],
                             (ntiles, V7X_SUBLANES, hw))
    xs_sorted, gates = _dispatch(n_experts, n_valid, rmax, npc, loff, goff, zoff, zcnt, x3b, lt, lofft)
    yh, yl = _experts(te, tv, xr, xs_sorted, gates, mw1, mw3, mw2)
    y_p, y_s = _combine(alpha, n_experts, n_valid, npc, loff, goff, logits, x3p, x3s, loffr,
                        ln_ffn_odd_g, ln_ffn_odd_b, yh, yl)

    nd = state_conv_d.shape[2]
    return (y_p.reshape(B, T, D), y_s.reshape(nb, 1, D),
            pool_p, pool_s, convb_p, convb_s,
            k_p.reshape(1, B, T, n_heads, head_dim), k_s.reshape(1, nb, 1, n_heads, head_dim),
            v_p.reshape(1, B, T, n_heads, head_dim), v_s.reshape(1, nb, 1, n_heads, head_dim),
            lf_p.reshape(1, B, T, n_heads), lf_s.reshape(1, nb, 1, n_heads),
            cd_p[None, :, CONV_D_HIST_ROWS - nd:, :], jnp.swapaxes(convd_s_t, 0, 1)[None])
```
